```python
import math, functools
import jax, jax.numpy as jnp
from jax import lax
import numpy as np

D_MODEL = 1024
BATCH = 2
SEQ = 8192
DEPTH = 1
DEC_BATCH = 128
DEC_SEQ = 1
PAST_LEN = 8192
PAGE_SIZE = 128

N_HEADS_A = 8
N_KV_A = 2
HEAD_DIM_A = 64
GROUP_A = N_HEADS_A // N_KV_A
IDX_HEADS = 8
IDX_DIM = 64
TOPK_MAX = 256
Q_BLOCK = 128
ROPE_THETA = 500000.0
H_B = 4
DK_B = 128
DV_B = 128
CONV_K = 4
CHUNK = 64
CONV_DIM = 2 * H_B * DK_B + H_B * DV_B
BRANCH_WIDTH = N_HEADS_A * HEAD_DIM_A
D_FF = -(-8 * D_MODEL // (3 * 256)) * 256
NORM_EPS = 1e-6
NEG_INF = -1e30
IN_SIZES = (N_HEADS_A * HEAD_DIM_A, N_KV_A * HEAD_DIM_A, N_KV_A * HEAD_DIM_A,
            IDX_HEADS * IDX_DIM, IDX_DIM, IDX_HEADS,
            CONV_DIM, H_B, H_B, H_B * DV_B, 2 * D_MODEL)
D_IN = sum(IN_SIZES)

kernel_name = 'hybrid_dsa_gated_delta_gated_merge_step'


def rms_norm(x, gain):
    xf = x.astype(jnp.float32)
    y = xf * lax.rsqrt(jnp.mean(xf * xf, axis=-1, keepdims=True) + NORM_EPS)
    return (y * gain.astype(jnp.float32)).astype(x.dtype)


def l2_norm(x):
    xf = x.astype(jnp.float32)
    return xf * lax.rsqrt(jnp.sum(xf * xf, axis=-1, keepdims=True) + NORM_EPS)


def rope(x, pos):
    rot = x.shape[-1] // 4
    half = rot // 2
    inv_freq = ROPE_THETA ** (-jnp.arange(half, dtype=jnp.float32) / half)
    ang = pos.astype(jnp.float32)[:, None] * inv_freq[None, :]
    cos = jnp.cos(ang)[None, :, None, :]
    sin = jnp.sin(ang)[None, :, None, :]
    xf = x.astype(jnp.float32)
    x1, x2, rest = xf[..., :half], xf[..., half:rot], xf[..., rot:]
    out = jnp.concatenate([x1 * cos - x2 * sin, x2 * cos + x1 * sin, rest], axis=-1)
    return out.astype(x.dtype)


def take_rows(a, idx):
    return jax.vmap(lambda ab, ib: ab[ib])(a, idx)


def select_keys(iq, ik, iw, qpos, topk):
    dots = jnp.einsum('bqhd,bsd->bqhs', iq.astype(jnp.float32), ik.astype(jnp.float32))
    score = jnp.einsum('bqhs,bqh->bqs', jax.nn.relu(dots * IDX_DIM ** -0.5),
                       iw.astype(jnp.float32)) * IDX_HEADS ** -0.5
    kpos = jnp.arange(ik.shape[1])
    admissible = kpos[None, None, :] <= qpos[None, :, None]
    score = jnp.where(admissible, score, NEG_INF)
    _, sel = lax.top_k(score, topk)
    valid = sel <= qpos[None, :, None]
    return sel, valid


def sparse_attend(q, ksel, vsel, valid):
    B, Q = q.shape[:2]
    qg = q.astype(jnp.float32).reshape(B, Q, N_KV_A, GROUP_A, HEAD_DIM_A)
    logits = jnp.einsum('bqngd,bqknd->bqngk', qg, ksel.astype(jnp.float32)) * HEAD_DIM_A ** -0.5
    logits = jnp.where(valid[:, :, None, None, :], logits, NEG_INF)
    p = jax.nn.softmax(logits, axis=-1)
    o = jnp.einsum('bqngk,bqknd->bqngd', p, vsel.astype(jnp.float32))
    return o.reshape(B, Q, N_HEADS_A, HEAD_DIM_A).astype(q.dtype)


def dsa_prompt(q, k, v, iq, ik, iw):
    B, T = q.shape[:2]
    topk = min(TOPK_MAX, T // 4)

    def one_block(i):
        t0 = i * Q_BLOCK

        def cut(a):
            return lax.dynamic_slice_in_dim(a, t0, Q_BLOCK, axis=1)

        qpos = t0 + jnp.arange(Q_BLOCK)
        sel, valid = select_keys(cut(iq), ik, cut(iw), qpos, topk)
        return sparse_attend(cut(q), take_rows(k, sel), take_rows(v, sel), valid)

    o = lax.map(one_block, jnp.arange(T // Q_BLOCK))
    return jnp.moveaxis(o, 0, 1).reshape(B, T, N_HEADS_A, HEAD_DIM_A)


def dsa_sample(q, k, v, iq, ik, iw, cache_k, cache_v, cache_ik, page_table):
    DB, DS = q.shape[:2]
    past = page_table.shape[1] * PAGE_SIZE
    topk = min(TOPK_MAX, (past + DS) // 4)
    ik_past = cache_ik[page_table].reshape(DB, past, IDX_DIM)
    ik_all = jnp.concatenate([ik_past, ik.astype(ik_past.dtype)], axis=1)
    qpos = past + jnp.arange(DS)
    sel, valid = select_keys(iq, ik_all, iw, qpos, topk)
    in_past = (sel < past)[..., None, None]
    p = jnp.minimum(sel, past - 1)
    phys = jax.vmap(lambda pt, lp: pt[lp])(page_table, p // PAGE_SIZE)
    slot = p % PAGE_SIZE
    j = jnp.clip(sel - past, 0, DS - 1)
    ksel = jnp.where(in_past, cache_k[phys, slot], take_rows(k, j).astype(cache_k.dtype))
    vsel = jnp.where(in_past, cache_v[phys, slot], take_rows(v, j).astype(cache_v.dtype))
    return sparse_attend(q, ksel, vsel, valid)


def gated_delta_chunked(q, k, v, g, beta, s0):
    B, T, H, DK = q.shape
    DV = v.shape[-1]
    n = -(-T // CHUNK)
    pad = n * CHUNK - T

    def prep(a):
        a = a.astype(jnp.float32)
        a = jnp.pad(a, [(0, 0), (0, pad)] + [(0, 0)] * (a.ndim - 2))
        a = a.reshape((B, n, CHUNK) + a.shape[2:])
        return jnp.moveaxis(a, 3, 1)

    q, k, v, g, beta = prep(q), prep(k), prep(v), prep(g), prep(beta)
    g = jnp.cumsum(g, axis=-1)
    idx = jnp.arange(CHUNK)
    lower = idx[:, None] >= idx[None, :]
    strict = idx[:, None] > idx[None, :]
    diff = g[..., :, None] - g[..., None, :]
    decay = jnp.where(lower, jnp.exp(jnp.where(lower, diff, 0.0)), 0.0)
    kb = k * beta[..., None]
    vb = v * beta[..., None]
    a_mat = jnp.where(strict, jnp.einsum('bhncd,bhnsd->bhncs', kb, k) * decay, 0.0)
    eye = jnp.eye(CHUNK, dtype=jnp.float32)
    rhs = jnp.concatenate([vb, kb * jnp.exp(g)[..., None]], axis=-1)
    sol = lax.linalg.triangular_solve(eye + a_mat, rhs, left_side=True, lower=True)
    u, w = sol[..., :DV], sol[..., DV:]
    qk = jnp.einsum('bhncd,bhnsd->bhncs', q, k) * decay
    q_dec = q * jnp.exp(g)[..., None]
    k_dec = k * jnp.exp(g[..., -1:] - g)[..., None]
    g_last = jnp.exp(g[..., -1])

    def step(s, xs):
        u_i, w_i, qd_i, qk_i, kd_i, gl_i = xs
        v_new = u_i - jnp.einsum('bhcd,bhde->bhce', w_i, s)
        o = jnp.einsum('bhcd,bhde->bhce', qd_i, s) + jnp.einsum('bhcs,bhse->bhce', qk_i, v_new)
        s = s * gl_i[..., None, None] + jnp.einsum('bhcd,bhce->bhde', kd_i, v_new)
        return s, o

    xs = tuple(jnp.moveaxis(a, 2, 0) for a in (u, w, q_dec, qk, k_dec, g_last))
    s_fin, o = lax.scan(step, s0.astype(jnp.float32), xs)
    o = jnp.transpose(o, (1, 0, 3, 2, 4)).reshape(B, n * CHUNK, H, DV)[:, :T]
    return o, s_fin


def block(x, positions, attend, conv_buf, s0, norm_mix, w_in, q_norm, k_norm, w_conv, a_log,
          dt_bias, delta_norm, w_branch, w_out, norm_ffn, w_gate_up, w_down):
    B, T, _ = x.shape
    xn = rms_norm(x, norm_mix)
    h = xn @ w_in
    pts = np.cumsum(IN_SIZES)[:-1].tolist()
    q, k, v, iq, ik, iw, u, a, b, z, gl = jnp.split(h, pts, axis=-1)
    q = rope(rms_norm(q.reshape(B, T, N_HEADS_A, HEAD_DIM_A), q_norm), positions)
    k = rope(rms_norm(k.reshape(B, T, N_KV_A, HEAD_DIM_A), k_norm), positions)
    v = v.reshape(B, T, N_KV_A, HEAD_DIM_A)
    iq = rope(iq.reshape(B, T, IDX_HEADS, IDX_DIM), positions)
    ik = rope(ik.reshape(B, T, 1, IDX_DIM), positions).reshape(B, T, IDX_DIM)
    o_a = attend(q, k, v, iq, ik, iw).reshape(B, T, BRANCH_WIDTH)
    u_cat = jnp.concatenate([conv_buf.astype(u.dtype), u], axis=1)
    c = jax.nn.silu(sum(u_cat[:, j:j + T] * w_conv[j] for j in range(CONV_K)))
    new_conv = u_cat[:, T:]
    qb, kb, vb = jnp.split(c, [H_B * DK_B, 2 * H_B * DK_B], axis=-1)
    qb = l2_norm(qb.reshape(B, T, H_B, DK_B)) * DK_B ** -0.5
    kb = l2_norm(kb.reshape(B, T, H_B, DK_B))
    vb = vb.reshape(B, T, H_B, DV_B)
    g = -jnp.exp(a_log.astype(jnp.float32)) * jax.nn.softplus(a.astype(jnp.float32) + dt_bias.astype(jnp.float32))
    beta = jax.nn.sigmoid(b.astype(jnp.float32))
    o_b, s_new = gated_delta_chunked(qb, kb, vb, g, beta, s0)
    o_b = rms_norm(o_b.astype(x.dtype), delta_norm) * jax.nn.silu(z.reshape(B, T, H_B, DV_B))
    o_b = o_b.reshape(B, T, BRANCH_WIDTH)
    proj = jnp.einsum('btnc,ncd->btnd', jnp.stack([o_a, o_b], axis=2), w_branch)
    gates = jax.nn.sigmoid(gl.reshape(B, T, 2, D_MODEL))
    x = x + jnp.sum(gates * proj, axis=2) @ w_out
    hn = rms_norm(x, norm_ffn)
    gg, uu = jnp.split(hn @ w_gate_up, 2, axis=-1)
    x = x + (jax.nn.silu(gg) * uu) @ w_down
    return x, (k, v, ik, new_conv, s_new.astype(s0.dtype))


def setup_inputs(seed: int = 0) -> dict:
    key = jax.random.key(seed)
    ks = jax.random.split(key, 24)
    n_pages = PAST_LEN // PAGE_SIZE
    n_used = DEC_BATCH * n_pages
    n_pool = n_used + (n_used + 3) // 4

    def nrm(k, shape, scale=1.0):
        return scale * jax.random.normal(k, shape, jnp.float32)

    page_table = jax.random.permutation(ks[0], n_pool)[:n_used].reshape(DEC_BATCH, n_pages).astype(jnp.int32)
    dt = jnp.exp(jax.random.uniform(ks[1], (DEPTH, H_B), jnp.float32, math.log(1e-3), math.log(1e-1)))
    return {
        'x_prompt': nrm(ks[2], (BATCH, SEQ, D_MODEL)),
        'x_sample': nrm(ks[3], (DEC_BATCH, DEC_SEQ, D_MODEL)),
        'cache_k': nrm(ks[4], (DEPTH, n_pool, PAGE_SIZE, N_KV_A, HEAD_DIM_A)),
        'cache_v': nrm(ks[5], (DEPTH, n_pool, PAGE_SIZE, N_KV_A, HEAD_DIM_A)),
        'cache_idx_k': nrm(ks[6], (DEPTH, n_pool, PAGE_SIZE, IDX_DIM)),
        'state_conv': nrm(ks[7], (DEPTH, DEC_BATCH, CONV_K - 1, CONV_DIM)),
        'state_delta': nrm(ks[8], (DEPTH, DEC_BATCH, H_B, DK_B, DV_B), DK_B ** -0.5),
        'page_table': page_table,
        'norm_mix': 1.0 + nrm(ks[9], (DEPTH, D_MODEL), 0.02),
        'w_in': nrm(ks[10], (DEPTH, D_MODEL, D_IN), D_MODEL ** -0.5),
        'q_norm': 1.0 + nrm(ks[11], (DEPTH, HEAD_DIM_A), 0.02),
        'k_norm': 1.0 + nrm(ks[12], (DEPTH, HEAD_DIM_A), 0.02),
        'w_conv': nrm(ks[13], (DEPTH, CONV_K, CONV_DIM), CONV_K ** -0.5),
        'a_log': jnp.log(jax.random.uniform(ks[14], (DEPTH, H_B), jnp.float32, 1.0, 16.0)),
        'dt_bias': dt + jnp.log(-jnp.expm1(-dt)),
        'delta_norm': 1.0 + nrm(ks[15], (DEPTH, DV_B), 0.02),
        'w_branch': nrm(ks[16], (DEPTH, 2, BRANCH_WIDTH, D_MODEL), BRANCH_WIDTH ** -0.5),
        'w_out': nrm(ks[17], (DEPTH, D_MODEL, D_MODEL), D_MODEL ** -0.5),
        'norm_ffn': 1.0 + nrm(ks[18], (DEPTH, D_MODEL), 0.02),
        'w_gate_up': nrm(ks[19], (DEPTH, D_MODEL, 2 * D_FF), D_MODEL ** -0.5),
        'w_down': nrm(ks[20], (DEPTH, D_FF, D_MODEL), D_FF ** -0.5),
    }


def reference(x_prompt, x_sample, cache_k, cache_v, cache_idx_k, state_conv, state_delta, page_table,
              norm_mix, w_in, q_norm, k_norm, w_conv, a_log, dt_bias, delta_norm, w_branch, w_out,
              norm_ffn, w_gate_up, w_down):
    B, T, _ = x_prompt.shape
    DS = x_sample.shape[1]
    past = page_table.shape[1] * PAGE_SIZE
    pos_p = jnp.arange(T)
    pos_s = past + jnp.arange(DS)
    conv0 = jnp.zeros((B, CONV_K - 1, CONV_DIM), x_prompt.dtype)
    delta0 = jnp.zeros((B, H_B, DK_B, DV_B), x_prompt.dtype)
    y_prompt, y_sample = x_prompt, x_sample
    new_p, new_s = [], []
    for l in range(DEPTH):
        weights = (norm_mix[l], w_in[l], q_norm[l], k_norm[l], w_conv[l], a_log[l], dt_bias[l],
                   delta_norm[l], w_branch[l], w_out[l], norm_ffn[l], w_gate_up[l], w_down[l])
        y_prompt, st_p = block(y_prompt, pos_p, dsa_prompt, conv0, delta0, *weights)
        attend_s = functools.partial(dsa_sample, cache_k=cache_k[l], cache_v=cache_v[l],
                                     cache_ik=cache_idx_k[l], page_table=page_table)
        y_sample, st_s = block(y_sample, pos_s, attend_s, state_conv[l], state_delta[l], *weights)
        new_p.append(st_p)
        new_s.append(st_s)
    k_p, v_p, ik_p, conv_p, delta_p = [jnp.stack(a) for a in zip(*new_p)]
    k_s, v_s, ik_s, conv_s, delta_s = [jnp.stack(a) for a in zip(*new_s)]
    return (y_prompt, y_sample, k_p, v_p, ik_p, conv_p, delta_p, k_s, v_s, ik_s, conv_s, delta_s)
```

```python
import functools
import math

import jax
import jax.numpy as jnp
import numpy as np
from jax import lax
from jax.experimental import pallas as pl
from jax.experimental.pallas import tpu as pltpu

D_MODEL = 1024
PAGE_SIZE = 128
N_HEADS_A = 8
N_KV_A = 2
HEAD_DIM_A = 64
GROUP_A = N_HEADS_A // N_KV_A
IDX_HEADS = 8
IDX_DIM = 64
TOPK_MAX = 256
ROPE_THETA = 500000.0
H_B = 4
DK_B = 128
DV_B = 128
CONV_K = 4
CONV_DIM = 2 * H_B * DK_B + H_B * DV_B
BRANCH_WIDTH = N_HEADS_A * HEAD_DIM_A
D_FF = -(-8 * D_MODEL // (3 * 256)) * 256
NORM_EPS = 1e-6
NEG_INF = -1e30
IN_SIZES = (N_HEADS_A * HEAD_DIM_A, N_KV_A * HEAD_DIM_A, N_KV_A * HEAD_DIM_A,
            IDX_HEADS * IDX_DIM, IDX_DIM, IDX_HEADS,
            CONV_DIM, H_B, H_B, H_B * DV_B, 2 * D_MODEL)

LANES = 128
SUBLANES = 8
VMEM_LIMIT_BYTES = 56 * 1024 * 1024

MXU_DTYPE = jnp.bfloat16

SMALL_W = LANES
SEG_A = BRANCH_WIDTH + 2 * N_KV_A * HEAD_DIM_A + IDX_HEADS * IDX_DIM + IDX_DIM
SEG_A_PAD = -(-SEG_A // LANES) * LANES
OFF_SMALL = SEG_A_PAD
OFF_U = OFF_SMALL + SMALL_W
OFF_Z = OFF_U + CONV_DIM
OFF_GL = OFF_Z + H_B * DV_B
D_IN_PACKED = OFF_GL + 2 * D_MODEL


def _mm(a, b):
    return jnp.dot(a.astype(MXU_DTYPE), b.astype(MXU_DTYPE), preferred_element_type=jnp.float32)


def _mm_nt(a, b):
    return lax.dot_general(a.astype(MXU_DTYPE), b.astype(MXU_DTYPE), (((1,), (1,)), ((), ())),
                           preferred_element_type=jnp.float32)


def _mm_tn(a, b):
    return lax.dot_general(a.astype(MXU_DTYPE), b.astype(MXU_DTYPE), (((0,), (0,)), ((), ())),
                           preferred_element_type=jnp.float32)


def _split3(x):
    x = x.astype(jnp.float32)
    h = x.astype(MXU_DTYPE)
    r = x - h.astype(jnp.float32)
    m = r.astype(MXU_DTYPE)
    l = (r - m.astype(jnp.float32)).astype(MXU_DTYPE)
    return h, m, l


def _mm3(a, b, dims=(((1,), (0,)), ((), ()))):
    a1, a2, a3 = _split3(a)
    b1, b2, b3 = _split3(b)
    d = functools.partial(lax.dot_general, dimension_numbers=dims, preferred_element_type=jnp.float32)
    small = d(a1, b3) + d(a3, b1) + d(a2, b2)
    mid = d(a1, b2) + d(a2, b1)
    return d(a1, b1) + (mid + small)


def _mm3_exact_rhs(a, b):
    a1, a2, a3 = _split3(a)
    b = b.astype(MXU_DTYPE)
    d = functools.partial(jnp.dot, preferred_element_type=jnp.float32)
    return d(a1, b) + (d(a2, b) + d(a3, b))


def _sigmoid(x):
    return 1.0 / (1.0 + jnp.exp(-x))


def _silu(x):
    return x * _sigmoid(x)


def _softplus(x):
    return jnp.maximum(x, 0.0) + jnp.log(1.0 + jnp.exp(-jnp.abs(x)))


def _rope_tile(x, cos_t, sin_lo, sin_hi):
    half = HEAD_DIM_A // 8
    up = pltpu.roll(x, LANES - half, 1)
    dn = pltpu.roll(x, half, 1)
    return x * cos_t + up * sin_lo + dn * sin_hi


def _in_proj_kernel(x_ref, gain_ref, w_ref, bd_ref, qg_ref, kg_ref, cos_ref, slo_ref, shi_ref,
                    qexp_ref, kv_ref, kb_ref, vext_ref, iqhm_ref, ik_ref, ikb_ref, small_ref,
                    u_ref, z_ref, gl_ref):
    x = x_ref[...]
    ms = jnp.mean(x * x, axis=-1, keepdims=True)
    xn = (x * lax.rsqrt(ms + NORM_EPS) * gain_ref[...]).astype(MXU_DTYPE)

    cos_t, sin_lo, sin_hi = cos_ref[...], slo_ref[...], shi_ref[...]
    lane = lax.broadcasted_iota(jnp.int32, (x.shape[0], LANES), 1)
    lo_half = lane < HEAD_DIM_A

    def head_rms(t, gain):
        tt = t * t
        hi = tt.astype(MXU_DTYPE)
        lo = (tt - hi.astype(jnp.float32)).astype(MXU_DTYPE)
        bd = bd_ref[0:t.shape[1], 0:t.shape[1]]
        msq = (jnp.dot(hi, bd, preferred_element_type=jnp.float32)
               + jnp.dot(lo, bd, preferred_element_type=jnp.float32))
        return t * lax.rsqrt(msq + NORM_EPS) * gain

    q = jnp.dot(xn, w_ref[:, 0:BRANCH_WIDTH], preferred_element_type=jnp.float32)
    q = head_rms(q, qg_ref[...])
    for p in range(BRANCH_WIDTH // LANES):
        t = _rope_tile(q[:, p * LANES:(p + 1) * LANES], cos_t, sin_lo, sin_hi) * (HEAD_DIM_A ** -0.5)
        t_sw = pltpu.roll(t, HEAD_DIM_A, 1)
        for e in range(2):
            h = 2 * p + e
            n = h // GROUP_A
            src = t if e == n else t_sw
            keep = lo_half if n == 0 else jnp.logical_not(lo_half)
            qexp_ref[h] = jnp.where(keep, src, 0.0).astype(qexp_ref.dtype)

    c0 = BRANCH_WIDTH
    kw = N_KV_A * HEAD_DIM_A
    k = jnp.dot(xn, w_ref[:, c0:c0 + kw], preferred_element_type=jnp.float32)
    k = _rope_tile(head_rms(k, kg_ref[...]), cos_t, sin_lo, sin_hi)
    v = jnp.dot(xn, w_ref[:, c0 + kw:c0 + 2 * kw], preferred_element_type=jnp.float32)
    kv_ref[:, 0:kw] = k
    kv_ref[:, kw:2 * kw] = v
    kb_ref[...] = k.astype(kb_ref.dtype)
    vext_ref[:, 0:kw] = v.astype(vext_ref.dtype)
    vext_ref[:, kw:2 * kw] = jnp.ones((x.shape[0], kw), vext_ref.dtype)

    c1 = c0 + 2 * kw
    iqw = IDX_HEADS * IDX_DIM
    iq = jnp.dot(xn, w_ref[:, c1:c1 + iqw], preferred_element_type=jnp.float32)
    for p in range(iqw // LANES):
        t = _rope_tile(iq[:, p * LANES:(p + 1) * LANES], cos_t, sin_lo, sin_hi).astype(iqhm_ref.dtype)
        iqhm_ref[2 * p] = t[:, 0:IDX_DIM]
        iqhm_ref[2 * p + 1] = t[:, IDX_DIM:2 * IDX_DIM]

    c2 = c1 + iqw
    ik_sm = jnp.dot(xn, w_ref[:, c2:c2 + 2 * LANES], preferred_element_type=jnp.float32)
    ik = _rope_tile(ik_sm[:, 0:LANES], cos_t, sin_lo, sin_hi)[:, 0:IDX_DIM]
    ik_ref[...] = ik
    ikb_ref[...] = ik.astype(ikb_ref.dtype)
    small_ref[...] = ik_sm[:, LANES:2 * LANES]

    u_ref[...] = jnp.dot(xn, w_ref[:, OFF_U:OFF_U + CONV_DIM], preferred_element_type=jnp.float32)
    z_ref[...] = jnp.dot(xn, w_ref[:, OFF_Z:OFF_Z + H_B * DV_B], preferred_element_type=jnp.float32)
    gl_ref[...] = jnp.dot(xn, w_ref[:, OFF_GL:OFF_GL + 2 * D_MODEL], preferred_element_type=jnp.float32)


def _pack_w_in(w_in):
    pts = np.cumsum(IN_SIZES)[:-1].tolist()
    q, k, v, iq, ik, iw, u, a, b, z, gl = jnp.split(w_in, pts, axis=-1)
    d = w_in.shape[0]
    seg_a = jnp.concatenate([q, k, v, iq, ik, jnp.zeros((d, SEG_A_PAD - SEG_A), w_in.dtype)], axis=1)
    small = jnp.concatenate([iw, a, b, jnp.zeros((d, SMALL_W - IDX_HEADS - 2 * H_B), w_in.dtype)], axis=1)
    return jnp.concatenate([seg_a, small, u, z, gl], axis=1).astype(MXU_DTYPE)


def _rope_tables(pos):
    rot = HEAD_DIM_A // 4
    half = rot // 2
    inv_freq = ROPE_THETA ** (-jnp.arange(half, dtype=jnp.float32) / half)
    ang = pos.astype(jnp.float32)[:, None] * inv_freq[None, :]
    cos, sin = jnp.cos(ang), jnp.sin(ang)
    rows = pos.shape[0]
    one = jnp.ones((rows, HEAD_DIM_A - rot), jnp.float32)
    zero = jnp.zeros((rows, HEAD_DIM_A - rot), jnp.float32)
    zh = jnp.zeros((rows, half), jnp.float32)
    cos_h = jnp.concatenate([cos, cos, one], axis=1)
    slo_h = jnp.concatenate([-sin, zh, zero], axis=1)
    shi_h = jnp.concatenate([zh, sin, zero], axis=1)
    rep = LANES // HEAD_DIM_A
    return jnp.tile(cos_h, (1, rep)), jnp.tile(slo_h, (1, rep)), jnp.tile(shi_h, (1, rep))


def _in_proj(x2d, pos_tables, n_table_blocks, tm, norm_mix, w_packed, q_norm, k_norm):
    n = x2d.shape[0]
    assert n % tm == 0
    cos_t, sin_lo, sin_hi = pos_tables
    bd = jnp.kron(jnp.eye(BRANCH_WIDTH // HEAD_DIM_A, dtype=jnp.float32),
                  jnp.full((HEAD_DIM_A, HEAD_DIM_A), 1.0 / HEAD_DIM_A, jnp.float32)).astype(MXU_DTYPE)
    qg = jnp.tile(q_norm.astype(jnp.float32), BRANCH_WIDTH // HEAD_DIM_A)[None, :]
    kg = jnp.tile(k_norm.astype(jnp.float32), N_KV_A)[None, :]
    kw = N_KV_A * HEAD_DIM_A
    row = lambda w: pl.BlockSpec((tm, w), lambda i: (i, 0))
    full = lambda a: pl.BlockSpec(a.shape, lambda i: (0,) * a.ndim)
    tab = pl.BlockSpec((tm, LANES), lambda i: (i % n_table_blocks, 0))
    out_shape = (
        jax.ShapeDtypeStruct((N_HEADS_A, n, LANES), MXU_DTYPE),
        jax.ShapeDtypeStruct((n, 2 * kw), jnp.float32),
        jax.ShapeDtypeStruct((n, kw), MXU_DTYPE),
        jax.ShapeDtypeStruct((n, 2 * kw), MXU_DTYPE),
        jax.ShapeDtypeStruct((IDX_HEADS, n, IDX_DIM), MXU_DTYPE),
        jax.ShapeDtypeStruct((n, IDX_DIM), jnp.float32),
        jax.ShapeDtypeStruct((n, IDX_DIM), MXU_DTYPE),
        jax.ShapeDtypeStruct((n, SMALL_W), jnp.float32),
        jax.ShapeDtypeStruct((n, CONV_DIM), jnp.float32),
        jax.ShapeDtypeStruct((n, H_B * DV_B), jnp.float32),
        jax.ShapeDtypeStruct((n, 2 * D_MODEL), jnp.float32),
    )
    out_specs = (
        pl.BlockSpec((N_HEADS_A, tm, LANES), lambda i: (0, i, 0)),
        row(2 * kw), row(kw), row(2 * kw),
        pl.BlockSpec((IDX_HEADS, tm, IDX_DIM), lambda i: (0, i, 0)),
        row(IDX_DIM), row(IDX_DIM), row(SMALL_W), row(CONV_DIM), row(H_B * DV_B), row(2 * D_MODEL),
    )
    return pl.pallas_call(
        _in_proj_kernel,
        grid=(n // tm,),
        in_specs=[row(D_MODEL), full(norm_mix[None, :]), full(w_packed), full(bd), full(qg), full(kg),
                  tab, tab, tab],
        out_specs=out_specs,
        out_shape=out_shape,
        compiler_params=pltpu.CompilerParams(dimension_semantics=("arbitrary",),
                                             vmem_limit_bytes=VMEM_LIMIT_BYTES),
        name="in_proj",
    )(x2d, norm_mix[None, :].astype(jnp.float32), w_packed, bd, qg, kg, cos_t, sin_lo, sin_hi)


_INT_MAG = 0x7FFFFFFF


def _f32_key(x):
    b = lax.bitcast_convert_type(x, jnp.int32)
    return b ^ (lax.shift_right_arithmetic(b, 31) & _INT_MAG)


def _key_f32(k):
    b = k ^ (lax.shift_right_arithmetic(k, 31) & _INT_MAG)
    return lax.bitcast_convert_type(b, jnp.float32)


def _wide(x, reps):
    return x if reps == 1 else jnp.concatenate([x] * reps, axis=1)


def _fold(x):
    acc = x[:, 0:LANES]
    for j in range(1, x.shape[1] // LANES):
        acc = acc + x[:, j * LANES:(j + 1) * LANES]
    return acc


def _topk_threshold(count, row_min, row_max, n_adm, topk, n_keys, linear_steps=14):
    kf = jnp.float32(topk)
    need = n_adm > topk
    lo_k = _f32_key(row_min)
    hi_k = _f32_key(row_max) + 1
    thr = jnp.where(need, row_min, -jnp.inf)
    zero = jnp.zeros_like(row_min)
    state = (jnp.int32(0), jnp.int32(1), lo_k, hi_k, thr, zero, jnp.where(need, 0, 1).astype(jnp.int32),
             jnp.zeros_like(lo_k))

    def cond(st):
        it, active = st[0], st[1]
        return jnp.logical_and(it < 80, active > 0)

    def body(st):
        it, _, lo_k, hi_k, thr, cnt_hi, done, tie = st
        adjacent = hi_k == lo_k + 1
        lo_f, hi_f = _key_f32(lo_k), _key_f32(hi_k)
        mid_lin = _f32_key(lo_f + 0.5 * (hi_f - lo_f))
        mid_lin = jnp.minimum(jnp.maximum(mid_lin, lo_k + 1), hi_k - 1)
        mid_int = (lo_k & hi_k) + lax.shift_right_arithmetic(lo_k ^ hi_k, 1)
        mid = jnp.where(it < linear_steps, mid_lin, mid_int)
        mid_f = _key_f32(mid)
        cnt = count(lambda s, kpos: s >= _wide(mid_f, s.shape[1] // LANES))
        live = jnp.logical_and(done == 0, jnp.logical_not(adjacent))
        hit = jnp.logical_and(live, cnt == kf)
        up = jnp.logical_and(live, cnt > kf)
        dn = jnp.logical_and(live, cnt < kf)
        new_tie = jnp.logical_and(done == 0, adjacent)
        thr = jnp.where(hit, mid_f, jnp.where(new_tie, lo_f, thr))
        tie = jnp.where(new_tie, 1, tie)
        done = jnp.where(jnp.logical_or(hit, new_tie), 1, done)
        lo_k = jnp.where(up, mid, lo_k)
        hi_k = jnp.where(dn, mid, hi_k)
        cnt_hi = jnp.where(dn, cnt, cnt_hi)
        active = jnp.max(1 - done)
        return (it + 1, active, lo_k, hi_k, thr, cnt_hi, done, tie)

    st = lax.while_loop(cond, body, state)
    thr, cnt_hi, tie = st[4], st[5], st[7]

    need_ties = kf - cnt_hi
    n_bits = max(1, int(math.ceil(math.log2(n_keys + 1))))
    any_tie = jnp.max(tie)

    def tie_body(_, lm):
        lo_m, hi_m = lm
        mid = lax.shift_right_arithmetic(lo_m + hi_m, 1)
        g = count(lambda s, kpos: jnp.logical_and(s == _wide(thr, s.shape[1] // LANES),
                                                  kpos < _wide(mid, s.shape[1] // LANES)))
        ge = g >= need_ties
        return jnp.where(ge, lo_m, mid), jnp.where(ge, mid, hi_m)

    lo_m0 = jnp.zeros_like(lo_k)
    hi_m0 = jnp.full_like(lo_k, n_keys)
    _, hi_m = lax.fori_loop(0, jnp.where(any_tie > 0, n_bits + 1, 0), tie_body, (lo_m0, hi_m0))
    cut = jnp.where(tie > 0, hi_m, n_keys + 1)
    return thr, cut


Q_TILE = 128
K_TILE = 256


def _dsa_prompt_kernel(topk, qexp_ref, iq_ref, small_ref, kb_ref, vext_ref, ikb_ref, o_ref,
                       sc_ref, wb_ref, acc_ref, m_ref):
    i = pl.program_id(1)
    tq, kc = Q_TILE, K_TILE
    reps = kc // LANES
    n_keys = sc_ref.shape[0] * kc
    nchunk = (i + 2) // 2
    qpos = i * tq + lax.broadcasted_iota(jnp.int32, (tq, LANES), 0)
    qpos_w = _wide(qpos, reps)
    lane_w = lax.broadcasted_iota(jnp.int32, (tq, kc), 1)

    sm = small_ref[...]
    s_scale = IDX_DIM ** -0.5 * IDX_HEADS ** -0.5
    for h in range(IDX_HEADS):
        wb_ref[h] = jnp.broadcast_to(sm[:, h:h + 1], (tq, kc))
    iq_all = iq_ref[...].reshape(IDX_HEADS * tq, IDX_DIM)

    def score_tile(c, carry):
        k0 = pl.multiple_of(c * kc, kc)
        d = _mm_nt(iq_all, ikb_ref[pl.ds(k0, kc), :])
        s = wb_ref[0] * jnp.maximum(d[0:tq], 0.0)
        for h in range(1, IDX_HEADS):
            s = s + wb_ref[h] * jnp.maximum(d[h * tq:(h + 1) * tq], 0.0)
        sc_ref[c] = jnp.where(k0 + lane_w <= qpos_w, s * s_scale, NEG_INF)
        return carry

    lax.fori_loop(0, nchunk, score_tile, 0)

    def count(pred):
        def body(c, acc):
            s = sc_ref[c]
            kpos = c * kc + lane_w
            ok = jnp.logical_and(pred(s, kpos), kpos <= qpos_w)
            return acc + _fold(jnp.where(ok, 1.0, 0.0))
        acc = lax.fori_loop(0, nchunk, body, jnp.zeros((tq, LANES), jnp.float32))
        return jnp.broadcast_to(jnp.sum(acc, axis=1, keepdims=True), (tq, LANES))

    def minmax(c, mm):
        s = sc_ref[c]
        adm = c * kc + lane_w <= qpos_w
        lo = jnp.where(adm, s, jnp.inf)
        hi = jnp.where(adm, s, -jnp.inf)
        for j in range(reps):
            mm = (jnp.minimum(mm[0], lo[:, j * LANES:(j + 1) * LANES]),
                  jnp.maximum(mm[1], hi[:, j * LANES:(j + 1) * LANES]))
        return mm

    mn, mx = lax.fori_loop(0, nchunk, minmax, (jnp.full((tq, LANES), jnp.inf, jnp.float32),
                                               jnp.full((tq, LANES), -jnp.inf, jnp.float32)))
    row_min = jnp.broadcast_to(jnp.min(mn, axis=1, keepdims=True), (tq, LANES))
    row_max = jnp.broadcast_to(jnp.max(mx, axis=1, keepdims=True), (tq, LANES))
    thr, cut = _topk_threshold(count, row_min, row_max, qpos + 1, topk, n_keys)
    thr_w, cut_w = _wide(thr, reps), _wide(cut, reps)

    m_ref[...] = jnp.full(m_ref.shape, 0.5 * NEG_INF, jnp.float32)
    acc_ref[...] = jnp.zeros(acc_ref.shape, jnp.float32)
    q_all = qexp_ref[...].reshape(N_HEADS_A * tq, LANES)

    def attend_tile(c, carry):
        k0 = pl.multiple_of(c * kc, kc)
        s = sc_ref[c]
        kpos = k0 + lane_w
        sel = jnp.logical_or(s > thr_w, jnp.logical_and(s == thr_w, kpos < cut_w))
        sel = jnp.logical_and(sel, kpos <= qpos_w)
        bias = jnp.where(sel, 0.0, NEG_INF)
        lg = _mm_nt(q_all, kb_ref[pl.ds(k0, kc), :])
        ps = []
        alphas = []
        for h in range(N_HEADS_A):
            rows = slice(h * tq, (h + 1) * tq)
            lgh = lg[rows] + bias
            m_old = m_ref[rows]
            m_new = jnp.maximum(m_old, jnp.max(lgh, axis=1, keepdims=True))
            alphas.append(jnp.exp(m_old - m_new))
            ps.append(jnp.exp(lgh - _wide(m_new, reps)).astype(MXU_DTYPE))
            m_ref[rows] = m_new
        pv = jnp.dot(jnp.concatenate(ps, axis=0), vext_ref[pl.ds(k0, kc), :],
                     preferred_element_type=jnp.float32)
        for h in range(N_HEADS_A):
            rows = slice(h * tq, (h + 1) * tq)
            acc_ref[rows] = acc_ref[rows] * _wide(alphas[h], 2) + pv[rows]
        return carry

    lax.fori_loop(0, nchunk, attend_tile, 0)

    lo_half = lax.broadcasted_iota(jnp.int32, (tq, LANES), 1) < HEAD_DIM_A
    outs = []
    for h in range(N_HEADS_A):
        a = acc_ref[h * tq:(h + 1) * tq]
        outs.append(a[:, 0:LANES] / a[:, LANES:2 * LANES])
    for p in range(N_HEADS_A // 2):
        n = (2 * p) // GROUP_A
        even, odd = outs[2 * p], outs[2 * p + 1]
        if n == 0:
            t = jnp.where(lo_half, even, pltpu.roll(odd, HEAD_DIM_A, 1))
        else:
            t = jnp.where(lo_half, pltpu.roll(even, HEAD_DIM_A, 1), odd)
        o_ref[:, p * LANES:(p + 1) * LANES] = t.astype(o_ref.dtype)


def _dsa_prompt(qexp, iqhm, small, kb, vext, ikb, batch, seq):
    tq, kc = Q_TILE, K_TILE
    assert seq % kc == 0
    nq = seq // tq
    n = batch * seq
    topk = min(TOPK_MAX, seq // 4)
    return pl.pallas_call(
        functools.partial(_dsa_prompt_kernel, topk),
        grid=(batch, nq),
        in_specs=[
            pl.BlockSpec((N_HEADS_A, tq, LANES), lambda b, i: (0, b * nq + i, 0)),
            pl.BlockSpec((IDX_HEADS, tq, IDX_DIM), lambda b, i: (0, b * nq + i, 0)),
            pl.BlockSpec((tq, SMALL_W), lambda b, i: (b * nq + i, 0)),
            pl.BlockSpec((seq, LANES), lambda b, i: (b, 0)),
            pl.BlockSpec((seq, 2 * LANES), lambda b, i: (b, 0)),
            pl.BlockSpec((seq, IDX_DIM), lambda b, i: (b, 0)),
        ],
        out_specs=pl.BlockSpec((tq, BRANCH_WIDTH), lambda b, i: (b * nq + i, 0)),
        out_shape=jax.ShapeDtypeStruct((n, BRANCH_WIDTH), MXU_DTYPE),
        scratch_shapes=[
            pltpu.VMEM((seq // kc, tq, kc), jnp.float32),
            pltpu.VMEM((IDX_HEADS, tq, kc), jnp.float32),
            pltpu.VMEM((N_HEADS_A * tq, 2 * LANES), jnp.float32),
            pltpu.VMEM((N_HEADS_A * tq, LANES), jnp.float32),
        ],
        compiler_params=pltpu.CompilerParams(dimension_semantics=("arbitrary", "arbitrary"),
                                             vmem_limit_bytes=VMEM_LIMIT_BYTES),
        name="dsa_prompt",
    )(qexp, iqhm, small, kb, vext, ikb)


def _page_copy(pt_ref, cache_ref, buf_ref, sem_ref, seq, page, slot):
    return pltpu.make_async_copy(cache_ref.at[pt_ref[seq, page]],
                                 buf_ref.at[slot, pl.ds(page * PAGE_SIZE, PAGE_SIZE)],
                                 sem_ref.at[slot])


def _pages_start(pt_ref, cache_ref, buf_ref, sem_ref, seq, slot, n_pages):
    def body(p, c):
        _page_copy(pt_ref, cache_ref, buf_ref, sem_ref, seq, p, slot).start()
        return c
    lax.fori_loop(0, n_pages, body, 0)


def _pages_wait(pt_ref, cache_ref, buf_ref, sem_ref, seq, slot, n_pages):
    def body(p, c):
        _page_copy(pt_ref, cache_ref, buf_ref, sem_ref, seq, p, slot).wait()
        return c
    lax.fori_loop(0, n_pages, body, 0)


def _dsa_sample_score_kernel(topk, n_pages, pt_ref, iq_ref, iw_ref, iknew_ref, cache_ik_ref,
                             sc_ref, thr_ref, cut_ref, ikbuf_ref, sem_ref):
    s = pl.program_id(0)
    n_seq = pl.num_programs(0)
    past = n_pages * PAGE_SIZE
    n_tiles = sc_ref.shape[0]
    slot = s % 2

    @pl.when(s == 0)
    def _():
        _pages_start(pt_ref, cache_ik_ref, ikbuf_ref, sem_ref, 0, 0, n_pages)

    @pl.when(s + 1 < n_seq)
    def _():
        _pages_start(pt_ref, cache_ik_ref, ikbuf_ref, sem_ref, s + 1, 1 - slot, n_pages)

    _pages_wait(pt_ref, cache_ik_ref, ikbuf_ref, sem_ref, s, slot, n_pages)

    iq = iq_ref[0]
    w = iw_ref[0]
    s_scale = IDX_DIM ** -0.5 * IDX_HEADS ** -0.5
    d = _mm_nt(iq, ikbuf_ref[slot])
    srow = jnp.sum(w * jnp.maximum(d, 0.0), axis=0, keepdims=True) * s_scale
    ik_new = iknew_ref[0].astype(MXU_DTYPE).astype(jnp.float32)
    d_self = jnp.sum(iq.astype(jnp.float32) * ik_new, axis=1, keepdims=True)
    s_self = jnp.sum(w * jnp.maximum(d_self, 0.0), axis=0, keepdims=True) * s_scale
    for j in range(n_tiles - 1):
        sc_ref[j, pl.ds(s, 1), :] = srow[:, j * LANES:(j + 1) * LANES]
    lane1 = lax.broadcasted_iota(jnp.int32, (1, LANES), 1)
    sc_ref[n_tiles - 1, pl.ds(s, 1), :] = jnp.where(lane1 == 0, s_self, NEG_INF)

    @pl.when(s == n_seq - 1)
    def _():
        rows = sc_ref.shape[1]
        lane = lax.broadcasted_iota(jnp.int32, (rows, LANES), 1)

        def count(pred):
            def body(j, acc):
                kpos = j * LANES + lane
                ok = jnp.logical_and(pred(sc_ref[j], kpos), kpos <= past)
                return acc + jnp.where(ok, 1.0, 0.0)
            acc = lax.fori_loop(0, n_tiles, body, jnp.zeros((rows, LANES), jnp.float32))
            return jnp.broadcast_to(jnp.sum(acc, axis=1, keepdims=True), (rows, LANES))

        def minmax(j, mm):
            t = sc_ref[j]
            adm = j * LANES + lane <= past
            return (jnp.minimum(mm[0], jnp.where(adm, t, jnp.inf)),
                    jnp.maximum(mm[1], jnp.where(adm, t, -jnp.inf)))

        mn, mx = lax.fori_loop(0, n_tiles, minmax, (jnp.full((rows, LANES), jnp.inf, jnp.float32),
                                                    jnp.full((rows, LANES), -jnp.inf, jnp.float32)))
        row_min = jnp.broadcast_to(jnp.min(mn, axis=1, keepdims=True), (rows, LANES))
        row_max = jnp.broadcast_to(jnp.max(mx, axis=1, keepdims=True), (rows, LANES))
        n_adm = jnp.full((rows, LANES), past + 1, jnp.int32)
        thr, cut = _topk_threshold(count, row_min, row_max, n_adm, topk, n_tiles * LANES)
        thr_ref[...] = thr
        cut_ref[...] = cut


def _dsa_sample_attend_kernel(n_pages, pt_ref, q_ref, sc_ref, thr_ref, cut_ref, kvnew_ref,
                              cache_k_ref, cache_v_ref, o_ref, kbuf_ref, vbuf_ref, ksem_ref, vsem_ref):
    s = pl.program_id(0)
    n_seq = pl.num_programs(0)
    past = n_pages * PAGE_SIZE
    slot = s % 2

    def start(seq, sl):
        _pages_start(pt_ref, cache_k_ref, kbuf_ref, ksem_ref, seq, sl, n_pages)
        _pages_start(pt_ref, cache_v_ref, vbuf_ref, vsem_ref, seq, sl, n_pages)

    @pl.when(s == 0)
    def _():
        start(0, 0)

    @pl.when(s + 1 < n_seq)
    def _():
        start(s + 1, 1 - slot)

    _pages_wait(pt_ref, cache_k_ref, kbuf_ref, ksem_ref, s, slot, n_pages)
    _pages_wait(pt_ref, cache_v_ref, vbuf_ref, vsem_ref, s, slot, n_pages)

    q = q_ref[0]
    sc = sc_ref[0]
    thr = thr_ref[0][:, 0:1]
    cut = cut_ref[0][:, 0:1]
    kw = N_KV_A * HEAD_DIM_A
    k_new = kvnew_ref[0][:, 0:kw].astype(MXU_DTYPE).astype(jnp.float32)
    v_new = kvnew_ref[0][:, kw:2 * kw].astype(MXU_DTYPE).astype(jnp.float32)

    srow = sc[:, 0:past]
    kpos = lax.broadcasted_iota(jnp.int32, (1, past), 1)
    sel = jnp.logical_or(srow > thr, jnp.logical_and(srow == thr, kpos < cut))
    bias = jnp.where(sel, 0.0, NEG_INF)
    s_self = sc[:, past:past + 1]
    sel_self = jnp.logical_or(s_self > thr, jnp.logical_and(s_self == thr, past < cut))

    lg = _mm_nt(q, kbuf_ref[slot]) + bias
    lg_self = jnp.sum(q.astype(jnp.float32) * k_new, axis=1, keepdims=True)
    lg_self = jnp.where(sel_self, lg_self, NEG_INF)
    m = jnp.maximum(jnp.max(lg, axis=1, keepdims=True), lg_self)
    p = jnp.exp(lg - m)
    p_self = jnp.exp(lg_self - m)
    denom = jnp.sum(p, axis=1, keepdims=True) + p_self
    o = (_mm(p, vbuf_ref[slot]) + p_self * v_new) / denom
    parts = []
    for h in range(N_HEADS_A):
        n = h // GROUP_A
        parts.append(o[h:h + 1, n * HEAD_DIM_A:(n + 1) * HEAD_DIM_A])
    o_ref[0] = jnp.concatenate(parts, axis=1)


def _dsa_sample(qexp, iqhm, small, ik_new, kv_new, cache_k, cache_v, cache_ik, page_table):
    db, n_pages = page_table.shape
    past = n_pages * PAGE_SIZE
    n_pool = cache_ik.shape[0]
    topk = min(TOPK_MAX, (past + 1) // 4)
    n_tiles = past // LANES + 1
    width = n_tiles * LANES
    q_s = jnp.swapaxes(qexp, 0, 1)
    iq_s = jnp.swapaxes(iqhm, 0, 1)
    iw_s = small[:, 0:IDX_HEADS].reshape(db, IDX_HEADS, 1)
    cparams = pltpu.CompilerParams(dimension_semantics=("arbitrary",), vmem_limit_bytes=VMEM_LIMIT_BYTES)
    per_seq = lambda *shape: pl.BlockSpec((1,) + shape, lambda s, pt: (s,) + (0,) * len(shape))
    whole = lambda *shape: pl.BlockSpec(shape, lambda s, pt: (0,) * len(shape))
    any_spec = pl.BlockSpec(memory_space=pl.ANY)

    sc, thr, cut = pl.pallas_call(
        functools.partial(_dsa_sample_score_kernel, topk, n_pages),
        grid_spec=pltpu.PrefetchScalarGridSpec(
            num_scalar_prefetch=1,
            grid=(db,),
            in_specs=[per_seq(IDX_HEADS, IDX_DIM), per_seq(IDX_HEADS, 1), per_seq(1, IDX_DIM), any_spec],
            out_specs=(whole(n_tiles, db, LANES), whole(db, LANES), whole(db, LANES)),
            scratch_shapes=[pltpu.VMEM((2, past, IDX_DIM), jnp.float32), pltpu.SemaphoreType.DMA((2,))],
        ),
        out_shape=(jax.ShapeDtypeStruct((n_tiles, db, LANES), jnp.float32),
                   jax.ShapeDtypeStruct((db, LANES), jnp.float32),
                   jax.ShapeDtypeStruct((db, LANES), jnp.int32)),
        compiler_params=cparams,
        name="dsa_sample_score",
    )(page_table, iq_s, iw_s, ik_new.reshape(db, 1, IDX_DIM), cache_ik)

    sc_rows = jnp.swapaxes(sc, 0, 1).reshape(db, 1, width)
    kw = N_KV_A * HEAD_DIM_A
    o = pl.pallas_call(
        functools.partial(_dsa_sample_attend_kernel, n_pages),
        grid_spec=pltpu.PrefetchScalarGridSpec(
            num_scalar_prefetch=1,
            grid=(db,),
            in_specs=[per_seq(N_HEADS_A, LANES), per_seq(1, width), per_seq(1, LANES), per_seq(1, LANES),
                      per_seq(1, 2 * kw), any_spec, any_spec],
            out_specs=per_seq(1, BRANCH_WIDTH),
            scratch_shapes=[pltpu.VMEM((2, past, kw), jnp.float32), pltpu.VMEM((2, past, kw), jnp.float32),
                            pltpu.SemaphoreType.DMA((2,)), pltpu.SemaphoreType.DMA((2,))],
        ),
        out_shape=jax.ShapeDtypeStruct((db, 1, BRANCH_WIDTH), jnp.float32),
        compiler_params=cparams,
        name="dsa_sample_attend",
    )(page_table, q_s, sc_rows, thr.reshape(db, 1, LANES), cut.reshape(db, 1, LANES),
      kv_new.reshape(db, 1, 2 * kw), cache_k.reshape(n_pool, PAGE_SIZE, kw), cache_v.reshape(n_pool, PAGE_SIZE, kw))
    return o.reshape(db, BRANCH_WIDTH)


GDN_CHUNK = 128
A_LANE = IDX_HEADS
B_LANE = IDX_HEADS + H_B


def _split2(x):
    h = x.astype(MXU_DTYPE)
    return h, (x - h.astype(jnp.float32)).astype(MXU_DTYPE)


def _mm2(a, b):
    a1, a2 = _split2(a)
    b1, b2 = _split2(b)
    d = functools.partial(jnp.dot, preferred_element_type=jnp.float32)
    return d(a1, b1) + (d(a1, b2) + d(a2, b1))


def _unit_lower_inverse(a):
    n = a.shape[0]
    eye = (lax.broadcasted_iota(jnp.int32, (n, n), 0) == lax.broadcasted_iota(jnp.int32, (n, n), 1))
    s = jnp.where(eye, 1.0, 0.0) - a
    p = _mm2(a, a)
    k = 2
    while k < n:
        s = s + _mm2(s, p)
        k *= 2
        if k < n:
            p = _mm2(p, p)
    return s


def _l2norm(x):
    return x * lax.rsqrt(jnp.sum(x * x, axis=-1, keepdims=True) + NORM_EPS)


def _gdn_prompt_kernel(u_ref, small_ref, z_ref, conv0_ref, s0_ref, wconv_ref, alog_ref, dtb_ref, dnorm_ref,
                       o_ref, conv_out_ref, s_out_ref, ucat_ref, state_ref):
    n = pl.program_id(1)
    c = GDN_CHUNK
    head = SUBLANES
    tail = CONV_K - 1

    @pl.when(n == 0)
    def _():
        ucat_ref[head - tail:head, :] = conv0_ref[0]
        state_ref[...] = s0_ref[0]

    ucat_ref[head:head + c, :] = u_ref[...]
    y = wconv_ref[tail:tail + 1, :] * ucat_ref[head:head + c, :]
    for j in range(tail):
        y = y + wconv_ref[j:j + 1, :] * ucat_ref[head - tail + j:head - tail + j + c, :]
    cv = _silu(y)
    carry_rows = ucat_ref[head + c - tail:head + c, :]
    ucat_ref[head - tail:head, :] = carry_rows
    conv_out_ref[0] = carry_rows

    sm = small_ref[...]
    g_all = -jnp.exp(alog_ref[...]) * _softplus(sm + dtb_ref[...])
    beta_all = _sigmoid(sm)
    row = lax.broadcasted_iota(jnp.int32, (c, c), 0)
    col = lax.broadcasted_iota(jnp.int32, (c, c), 1)
    lower = row >= col
    strict = row > col
    a1, a2, a3 = _split3(g_all)
    ltri = jnp.where(lower, 1.0, 0.0).astype(MXU_DTYPE)
    d = functools.partial(jnp.dot, preferred_element_type=jnp.float32)
    gc_all = d(ltri, a1) + (d(ltri, a2) + d(ltri, a3))

    qw = H_B * DK_B
    for h in range(H_B):
        qh = _l2norm(cv[:, h * DK_B:(h + 1) * DK_B]) * (DK_B ** -0.5)
        kh = _l2norm(cv[:, qw + h * DK_B:qw + (h + 1) * DK_B])
        vh = cv[:, 2 * qw + h * DV_B:2 * qw + (h + 1) * DV_B]
        beta_b = jnp.broadcast_to(beta_all[:, B_LANE + h:B_LANE + h + 1], (c, LANES))
        gcol = jnp.broadcast_to(gc_all[:, A_LANE + h:A_LANE + h + 1], (c, c))
        grow = gcol.T
        diff = gcol - grow
        decay = jnp.where(lower, jnp.exp(jnp.where(lower, diff, 0.0)), 0.0)
        eg = jnp.exp(gcol)
        g_last = gcol[c - 1:c, :]
        kb = kh * beta_b
        vb = vh * beta_b
        a_mat = jnp.where(strict, _mm_nt(kb, kh) * decay, 0.0)
        t_inv = _unit_lower_inverse(a_mat)
        sol = _mm2(t_inv, jnp.concatenate([vb, kb * eg], axis=1))
        u_i, w_i = sol[:, 0:DV_B], sol[:, DV_B:DV_B + DK_B]
        qk = _mm_nt(qh, kh) * decay
        s_h = state_ref[h]
        v_new = u_i - _mm(w_i, s_h)
        o_h = _mm(qh * eg, s_h) + _mm(qk, v_new)
        k_dec = kh * jnp.exp(g_last - gcol)
        state_ref[h] = s_h * jnp.exp(g_last) + _mm_tn(k_dec, v_new)
        ms = jnp.mean(o_h * o_h, axis=-1, keepdims=True)
        o_n = o_h * lax.rsqrt(ms + NORM_EPS) * dnorm_ref[...]
        o_ref[:, h * DV_B:(h + 1) * DV_B] = (o_n * _silu(z_ref[:, h * DV_B:(h + 1) * DV_B])).astype(o_ref.dtype)

    @pl.when(n == pl.num_programs(1) - 1)
    def _():
        s_out_ref[0] = state_ref[...]


GDN_SEQ_TILE = 8


def _gdn_sample_kernel(u_ref, cb_ref, small_ref, z_ref, s0_ref, wconv_ref, alog_ref, dtb_ref, dnorm_ref,
                       o_ref, conv_out_ref, s_out_ref):
    ts = GDN_SEQ_TILE
    tail = CONV_K - 1
    u_new = u_ref[...]
    y = wconv_ref[tail:tail + 1, :] * u_new
    for j in range(tail):
        y = y + wconv_ref[j:j + 1, :] * cb_ref[j]
    cv = _silu(y)
    for j in range(tail - 1):
        conv_out_ref[j] = cb_ref[j + 1]
    conv_out_ref[tail - 1] = u_new

    sm = small_ref[...]
    eg_all = jnp.exp(-jnp.exp(alog_ref[...]) * _softplus(sm + dtb_ref[...]))
    beta_all = _sigmoid(sm)
    qw = H_B * DK_B
    for h in range(H_B):
        q = _l2norm(cv[:, h * DK_B:(h + 1) * DK_B]) * (DK_B ** -0.5)
        k = _l2norm(cv[:, qw + h * DK_B:qw + (h + 1) * DK_B])
        v = cv[:, 2 * qw + h * DV_B:2 * qw + (h + 1) * DV_B]
        eg = eg_all[:, A_LANE + h:A_LANE + h + 1]
        beta = beta_all[:, B_LANE + h:B_LANE + h + 1]
        qk = jnp.sum(q * k, axis=-1, keepdims=True)
        k_t, q_t = k.T, q.T
        rows = []
        for r in range(ts):
            s_old = s0_ref[r, h]
            kc = k_t[:, r:r + 1]
            ks = jnp.sum(s_old * kc, axis=0, keepdims=True)
            qs = jnp.sum(s_old * q_t[:, r:r + 1], axis=0, keepdims=True)
            eg_r = eg[r:r + 1, :]
            v_new = beta[r:r + 1, :] * (v[r:r + 1, :] - eg_r * ks)
            rows.append(eg_r * qs + qk[r:r + 1, :] * v_new)
            s_out_ref[r, h] = s_old * eg_r + kc * v_new
        o_h = jnp.concatenate(rows, axis=0)
        ms = jnp.mean(o_h * o_h, axis=-1, keepdims=True)
        o_n = o_h * lax.rsqrt(ms + NORM_EPS) * dnorm_ref[...]
        o_ref[:, h * DV_B:(h + 1) * DV_B] = (o_n * _silu(z_ref[:, h * DV_B:(h + 1) * DV_B])).astype(o_ref.dtype)


def _gdn_sample(u, small, z, conv_buf, s0, w_conv, a_log, dt_bias, delta_norm):
    db = u.shape[0]
    ts = GDN_SEQ_TILE
    assert db % ts == 0
    tail = CONV_K - 1
    alog_row, dtb_row = _gate_rows(a_log, dt_bias)
    row = lambda w: pl.BlockSpec((ts, w), lambda i: (i, 0))
    full = lambda *shape: pl.BlockSpec(shape, lambda i: (0,) * len(shape))
    cb_spec = pl.BlockSpec((tail, ts, CONV_DIM), lambda i: (0, i, 0))
    st_spec = pl.BlockSpec((ts, H_B, DK_B, DV_B), lambda i: (i, 0, 0, 0))
    o, conv_t, s_new = pl.pallas_call(
        _gdn_sample_kernel,
        grid=(db // ts,),
        in_specs=[row(CONV_DIM), cb_spec, row(SMALL_W), row(H_B * DV_B), st_spec,
                  full(CONV_K, CONV_DIM), full(1, SMALL_W), full(1, SMALL_W), full(1, DV_B)],
        out_specs=(row(H_B * DV_B), cb_spec, st_spec),
        out_shape=(jax.ShapeDtypeStruct((db, H_B * DV_B), jnp.float32),
                   jax.ShapeDtypeStruct((tail, db, CONV_DIM), jnp.float32),
                   jax.ShapeDtypeStruct((db, H_B, DK_B, DV_B), jnp.float32)),
        compiler_params=pltpu.CompilerParams(dimension_semantics=("arbitrary",),
                                             vmem_limit_bytes=VMEM_LIMIT_BYTES),
        name="gdn_sample",
    )(u, jnp.swapaxes(conv_buf, 0, 1), small, z, s0, w_conv.astype(jnp.float32), alog_row, dtb_row,
      delta_norm.astype(jnp.float32)[None, :])
    return o, jnp.swapaxes(conv_t, 0, 1), s_new


def _gate_rows(a_log, dt_bias):
    alog_row = jnp.zeros((1, SMALL_W), jnp.float32).at[0, A_LANE:A_LANE + H_B].set(a_log.astype(jnp.float32))
    dtb_row = jnp.zeros((1, SMALL_W), jnp.float32).at[0, A_LANE:A_LANE + H_B].set(dt_bias.astype(jnp.float32))
    return alog_row, dtb_row


def _gdn_prompt(u, small, z, conv0, s0, w_conv, a_log, dt_bias, delta_norm, batch, seq):
    c = GDN_CHUNK
    assert seq % c == 0
    nc = seq // c
    n = batch * seq
    alog_row, dtb_row = _gate_rows(a_log, dt_bias)
    row = lambda w: pl.BlockSpec((c, w), lambda b, i: (b * nc + i, 0))
    per_b = lambda *shape: pl.BlockSpec((1,) + shape, lambda b, i: (b,) + (0,) * len(shape))
    full = lambda *shape: pl.BlockSpec(shape, lambda b, i: (0,) * len(shape))
    return pl.pallas_call(
        _gdn_prompt_kernel,
        grid=(batch, nc),
        in_specs=[row(CONV_DIM), row(SMALL_W), row(H_B * DV_B), per_b(CONV_K - 1, CONV_DIM),
                  per_b(H_B, DK_B, DV_B), full(CONV_K, CONV_DIM), full(1, SMALL_W), full(1, SMALL_W),
                  full(1, DV_B)],
        out_specs=(row(H_B * DV_B), per_b(CONV_K - 1, CONV_DIM), per_b(H_B, DK_B, DV_B)),
        out_shape=(jax.ShapeDtypeStruct((n, H_B * DV_B), MXU_DTYPE),
                   jax.ShapeDtypeStruct((batch, CONV_K - 1, CONV_DIM), jnp.float32),
                   jax.ShapeDtypeStruct((batch, H_B, DK_B, DV_B), jnp.float32)),
        scratch_shapes=[pltpu.VMEM((SUBLANES + c, CONV_DIM), jnp.float32),
                        pltpu.VMEM((H_B, DK_B, DV_B), jnp.float32)],
        compiler_params=pltpu.CompilerParams(dimension_semantics=("arbitrary", "arbitrary"),
                                             vmem_limit_bytes=VMEM_LIMIT_BYTES),
        name="gdn_prompt",
    )(u, small, z, conv0, s0, w_conv.astype(jnp.float32), alog_row, dtb_row,
      delta_norm.astype(jnp.float32)[None, :])


def _merge_kernel(x_ref, oa_ref, ob_ref, gl_ref, wba_ref, wbb_ref, wout_ref, gain_ref, x1_ref, hn_ref):
    pa = _mm(oa_ref[...], wba_ref[...])
    pb = _mm(ob_ref[...], wbb_ref[...])
    mix = _sigmoid(gl_ref[:, 0:D_MODEL]) * pa + _sigmoid(gl_ref[:, D_MODEL:2 * D_MODEL]) * pb
    x1 = x_ref[...] + _mm(mix, wout_ref[...])
    x1_ref[...] = x1
    ms = jnp.mean(x1 * x1, axis=-1, keepdims=True)
    hn_ref[...] = (x1 * lax.rsqrt(ms + NORM_EPS) * gain_ref[...]).astype(hn_ref.dtype)


def _merge(x2d, o_a, o_b, gl, w_branch, w_out, norm_ffn, tm):
    n = x2d.shape[0]
    assert n % tm == 0
    row = lambda w: pl.BlockSpec((tm, w), lambda i: (i, 0))
    full = lambda *shape: pl.BlockSpec(shape, lambda i: (0,) * len(shape))
    return pl.pallas_call(
        _merge_kernel,
        grid=(n // tm,),
        in_specs=[row(D_MODEL), row(BRANCH_WIDTH), row(BRANCH_WIDTH), row(2 * D_MODEL),
                  full(BRANCH_WIDTH, D_MODEL), full(BRANCH_WIDTH, D_MODEL), full(D_MODEL, D_MODEL),
                  full(1, D_MODEL)],
        out_specs=(row(D_MODEL), row(D_MODEL)),
        out_shape=(jax.ShapeDtypeStruct((n, D_MODEL), jnp.float32),
                   jax.ShapeDtypeStruct((n, D_MODEL), MXU_DTYPE)),
        compiler_params=pltpu.CompilerParams(dimension_semantics=("arbitrary",),
                                             vmem_limit_bytes=VMEM_LIMIT_BYTES),
        name="merge",
    )(x2d, o_a, o_b, gl, w_branch[0].astype(MXU_DTYPE), w_branch[1].astype(MXU_DTYPE),
      w_out.astype(MXU_DTYPE), norm_ffn.astype(jnp.float32)[None, :])


FFN_TILE = D_FF // 2


def _ffn_kernel(hn_ref, x1_ref, wg_ref, wu_ref, wd_ref, y_ref, acc_ref):
    j = pl.program_id(1)

    @pl.when(j == 0)
    def _():
        acc_ref[...] = x1_ref[...]

    hn = hn_ref[...]
    g = jnp.dot(hn, wg_ref[...], preferred_element_type=jnp.float32)
    u = jnp.dot(hn, wu_ref[...], preferred_element_type=jnp.float32)
    acc_ref[...] += _mm(_silu(g) * u, wd_ref[...])

    @pl.when(j == pl.num_programs(1) - 1)
    def _():
        y_ref[...] = acc_ref[...]


def _ffn(hn, x1, w_gate_up, w_down, tm):
    n = hn.shape[0]
    tf = FFN_TILE
    assert n % tm == 0 and D_FF % tf == 0 and tf % LANES == 0
    nf = D_FF // tf
    wgu = w_gate_up.astype(MXU_DTYPE)
    return pl.pallas_call(
        _ffn_kernel,
        grid=(n // tm, nf),
        in_specs=[pl.BlockSpec((tm, D_MODEL), lambda i, j: (i, 0)),
                  pl.BlockSpec((tm, D_MODEL), lambda i, j: (i, 0)),
                  pl.BlockSpec((D_MODEL, tf), lambda i, j: (0, j)),
                  pl.BlockSpec((D_MODEL, tf), lambda i, j: (0, j + nf)),
                  pl.BlockSpec((tf, D_MODEL), lambda i, j: (j, 0))],
        out_specs=pl.BlockSpec((tm, D_MODEL), lambda i, j: (i, 0)),
        out_shape=jax.ShapeDtypeStruct((n, D_MODEL), jnp.float32),
        scratch_shapes=[pltpu.VMEM((tm, D_MODEL), jnp.float32)],
        compiler_params=pltpu.CompilerParams(dimension_semantics=("arbitrary", "arbitrary"),
                                             vmem_limit_bytes=VMEM_LIMIT_BYTES),
        name="ffn",
    )(hn, x1, wgu, wgu, w_down.astype(MXU_DTYPE))


IN_PROJ_TILE = 256
MERGE_TILE = 512
FFN_ROW_TILE = 512


def _layer(x_p, x_s, cache_k, cache_v, cache_ik, conv_s, delta_s, page_table, norm_mix, w_in, q_norm, k_norm,
           w_conv, a_log, dt_bias, delta_norm, w_branch, w_out, norm_ffn, w_gate_up, w_down):
    b, t, d = x_p.shape
    db = x_s.shape[0]
    past = page_table.shape[1] * PAGE_SIZE
    kw = N_KV_A * HEAD_DIM_A
    w_packed = _pack_w_in(w_in)

    xp2 = x_p.reshape(b * t, d)
    tm = min(IN_PROJ_TILE, t)
    assert t % tm == 0
    qexp, kv, kb, vext, iqhm, ik, ikb, small, u, z, gl = _in_proj(
        xp2, _rope_tables(jnp.arange(t)), t // tm, tm, norm_mix, w_packed, q_norm, k_norm)
    o_a = _dsa_prompt(qexp, iqhm, small, kb, vext, ikb, b, t)
    conv0 = jnp.zeros((b, CONV_K - 1, CONV_DIM), jnp.float32)
    delta0 = jnp.zeros((b, H_B, DK_B, DV_B), jnp.float32)
    o_b, conv_p, delta_p = _gdn_prompt(u, small, z, conv0, delta0, w_conv, a_log, dt_bias, delta_norm, b, t)
    x1, hn = _merge(xp2, o_a, o_b, gl, w_branch, w_out, norm_ffn, min(MERGE_TILE, b * t))
    y_p = _ffn(hn, x1, w_gate_up, w_down, min(FFN_ROW_TILE, b * t)).reshape(b, t, d)
    st_p = (kv[:, 0:kw].reshape(b, t, N_KV_A, HEAD_DIM_A), kv[:, kw:2 * kw].reshape(b, t, N_KV_A, HEAD_DIM_A),
            ik.reshape(b, t, IDX_DIM), conv_p, delta_p)

    xs2 = x_s.reshape(db, d)
    qexp, kv, kb, vext, iqhm, ik, ikb, small, u, z, gl = _in_proj(
        xs2, _rope_tables(jnp.full((db,), past, jnp.int32)), 1, db, norm_mix, w_packed, q_norm, k_norm)
    o_a = _dsa_sample(qexp, iqhm, small, ik, kv, cache_k, cache_v, cache_ik, page_table)
    o_b, conv_n, delta_n = _gdn_sample(u, small, z, conv_s, delta_s, w_conv, a_log, dt_bias, delta_norm)
    x1, hn = _merge(xs2, o_a, o_b, gl, w_branch, w_out, norm_ffn, db)
    y_s = _ffn(hn, x1, w_gate_up, w_down, db).reshape(db, 1, d)
    st_s = (kv[:, 0:kw].reshape(db, 1, N_KV_A, HEAD_DIM_A), kv[:, kw:2 * kw].reshape(db, 1, N_KV_A, HEAD_DIM_A),
            ik.reshape(db, 1, IDX_DIM), conv_n, delta_n)
    return y_p, y_s, st_p, st_s


def kernel(x_prompt, x_sample, cache_k, cache_v, cache_idx_k, state_conv, state_delta, page_table,
           norm_mix, w_in, q_norm, k_norm, w_conv, a_log, dt_bias, delta_norm, w_branch, w_out,
           norm_ffn, w_gate_up, w_down):
    assert x_sample.shape[1] == 1, "the sample group decodes one token per sequence"
    y_p, y_s = x_prompt, x_sample
    new_p, new_s = [], []
    for l in range(w_in.shape[0]):
        y_p, y_s, st_p, st_s = _layer(
            y_p, y_s, cache_k[l], cache_v[l], cache_idx_k[l], state_conv[l], state_delta[l], page_table,
            norm_mix[l], w_in[l], q_norm[l], k_norm[l], w_conv[l], a_log[l], dt_bias[l], delta_norm[l],
            w_branch[l], w_out[l], norm_ffn[l], w_gate_up[l], w_down[l])
        new_p.append(st_p)
        new_s.append(st_s)
    k_p, v_p, ik_p, conv_p, delta_p = [jnp.stack(a) for a in zip(*new_p)]
    k_s, v_s, ik_s, conv_s, delta_s = [jnp.stack(a) for a in zip(*new_s)]
    return (y_p, y_s, k_p, v_p, ik_p, conv_p, delta_p, k_s, v_s, ik_s, conv_s, delta_s)
```

```python
import functools
import math

import jax
import jax.numpy as jnp
import numpy as np
from jax import lax
from jax.experimental import pallas as pl
from jax.experimental.pallas import tpu as pltpu

D_MODEL = 1024
PAGE_SIZE = 128
N_HEADS_A = 8
N_KV_A = 2
HEAD_DIM_A = 64
GROUP_A = N_HEADS_A // N_KV_A
IDX_HEADS = 8
IDX_DIM = 64
TOPK_MAX = 256
ROPE_THETA = 500000.0
H_B = 4
DK_B = 128
DV_B = 128
CONV_K = 4
CONV_DIM = 2 * H_B * DK_B + H_B * DV_B
BRANCH_WIDTH = N_HEADS_A * HEAD_DIM_A
D_FF = -(-8 * D_MODEL // (3 * 256)) * 256
NORM_EPS = 1e-6
NEG_INF = -1e30
IN_SIZES = (N_HEADS_A * HEAD_DIM_A, N_KV_A * HEAD_DIM_A, N_KV_A * HEAD_DIM_A,
            IDX_HEADS * IDX_DIM, IDX_DIM, IDX_HEADS,
            CONV_DIM, H_B, H_B, H_B * DV_B, 2 * D_MODEL)

LANES = 128
SUBLANES = 8
VMEM_LIMIT_BYTES = 56 * 1024 * 1024

MXU_DTYPE = jnp.bfloat16

SMALL_W = LANES
SEG_A = BRANCH_WIDTH + 2 * N_KV_A * HEAD_DIM_A + IDX_HEADS * IDX_DIM + IDX_DIM
SEG_A_PAD = -(-SEG_A // LANES) * LANES
OFF_SMALL = SEG_A_PAD
OFF_U = OFF_SMALL + SMALL_W
OFF_Z = OFF_U + CONV_DIM
OFF_GL = OFF_Z + H_B * DV_B
D_IN_PACKED = OFF_GL + 2 * D_MODEL


def _mm(a, b):
    return jnp.dot(a.astype(MXU_DTYPE), b.astype(MXU_DTYPE), preferred_element_type=jnp.float32)


def _mm_nt(a, b):
    return lax.dot_general(a.astype(MXU_DTYPE), b.astype(MXU_DTYPE), (((1,), (1,)), ((), ())),
                           preferred_element_type=jnp.float32)


def _mm_tn(a, b):
    return lax.dot_general(a.astype(MXU_DTYPE), b.astype(MXU_DTYPE), (((0,), (0,)), ((), ())),
                           preferred_element_type=jnp.float32)


def _split3(x):
    x = x.astype(jnp.float32)
    h = x.astype(MXU_DTYPE)
    r = x - h.astype(jnp.float32)
    m = r.astype(MXU_DTYPE)
    l = (r - m.astype(jnp.float32)).astype(MXU_DTYPE)
    return h, m, l


def _mm3(a, b, dims=(((1,), (0,)), ((), ()))):
    a1, a2, a3 = _split3(a)
    b1, b2, b3 = _split3(b)
    d = functools.partial(lax.dot_general, dimension_numbers=dims, preferred_element_type=jnp.float32)
    small = d(a1, b3) + d(a3, b1) + d(a2, b2)
    mid = d(a1, b2) + d(a2, b1)
    return d(a1, b1) + (mid + small)


def _mm3_exact_rhs(a, b):
    a1, a2, a3 = _split3(a)
    b = b.astype(MXU_DTYPE)
    d = functools.partial(jnp.dot, preferred_element_type=jnp.float32)
    return d(a1, b) + (d(a2, b) + d(a3, b))


def _sigmoid(x):
    return 1.0 / (1.0 + jnp.exp(-x))


def _silu(x):
    return x * _sigmoid(x)


def _softplus(x):
    return jnp.maximum(x, 0.0) + jnp.log(1.0 + jnp.exp(-jnp.abs(x)))


def _rope_tile(x, cos_t, sin_lo, sin_hi):
    half = HEAD_DIM_A // 8
    up = pltpu.roll(x, LANES - half, 1)
    dn = pltpu.roll(x, half, 1)
    return x * cos_t + up * sin_lo + dn * sin_hi


def _in_proj_kernel(x_ref, gain_ref, w_ref, bd_ref, qg_ref, kg_ref, cos_ref, slo_ref, shi_ref,
                    qexp_ref, kv_ref, kb_ref, vext_ref, iqhm_ref, ik_ref, ikb_ref, small_ref,
                    u_ref, z_ref, gl_ref):
    x = x_ref[...]
    ms = jnp.mean(x * x, axis=-1, keepdims=True)
    xn = (x * lax.rsqrt(ms + NORM_EPS) * gain_ref[...]).astype(MXU_DTYPE)

    cos_t, sin_lo, sin_hi = cos_ref[...], slo_ref[...], shi_ref[...]
    lane = lax.broadcasted_iota(jnp.int32, (x.shape[0], LANES), 1)
    lo_half = lane < HEAD_DIM_A

    def head_rms(t, gain):
        tt = t * t
        hi = tt.astype(MXU_DTYPE)
        lo = (tt - hi.astype(jnp.float32)).astype(MXU_DTYPE)
        bd = bd_ref[0:t.shape[1], 0:t.shape[1]]
        msq = (jnp.dot(hi, bd, preferred_element_type=jnp.float32)
               + jnp.dot(lo, bd, preferred_element_type=jnp.float32))
        return t * lax.rsqrt(msq + NORM_EPS) * gain

    q = jnp.dot(xn, w_ref[:, 0:BRANCH_WIDTH], preferred_element_type=jnp.float32)
    q = head_rms(q, qg_ref[...])
    for p in range(BRANCH_WIDTH // LANES):
        t = _rope_tile(q[:, p * LANES:(p + 1) * LANES], cos_t, sin_lo, sin_hi) * (HEAD_DIM_A ** -0.5)
        t_sw = pltpu.roll(t, HEAD_DIM_A, 1)
        for e in range(2):
            h = 2 * p + e
            n = h // GROUP_A
            src = t if e == n else t_sw
            keep = lo_half if n == 0 else jnp.logical_not(lo_half)
            qexp_ref[h] = jnp.where(keep, src, 0.0).astype(qexp_ref.dtype)

    c0 = BRANCH_WIDTH
    kw = N_KV_A * HEAD_DIM_A
    k = jnp.dot(xn, w_ref[:, c0:c0 + kw], preferred_element_type=jnp.float32)
    k = _rope_tile(head_rms(k, kg_ref[...]), cos_t, sin_lo, sin_hi)
    v = jnp.dot(xn, w_ref[:, c0 + kw:c0 + 2 * kw], preferred_element_type=jnp.float32)
    kv_ref[:, 0:kw] = k
    kv_ref[:, kw:2 * kw] = v
    kb_ref[...] = k.astype(kb_ref.dtype)
    vext_ref[:, 0:kw] = v.astype(vext_ref.dtype)
    vext_ref[:, kw:2 * kw] = jnp.ones((x.shape[0], kw), vext_ref.dtype)

    c1 = c0 + 2 * kw
    iqw = IDX_HEADS * IDX_DIM
    iq = jnp.dot(xn, w_ref[:, c1:c1 + iqw], preferred_element_type=jnp.float32)
    for p in range(iqw // LANES):
        t = _rope_tile(iq[:, p * LANES:(p + 1) * LANES], cos_t, sin_lo, sin_hi).astype(iqhm_ref.dtype)
        iqhm_ref[2 * p] = t[:, 0:IDX_DIM]
        iqhm_ref[2 * p + 1] = t[:, IDX_DIM:2 * IDX_DIM]

    c2 = c1 + iqw
    ik_sm = jnp.dot(xn, w_ref[:, c2:c2 + 2 * LANES], preferred_element_type=jnp.float32)
    ik = _rope_tile(ik_sm[:, 0:LANES], cos_t, sin_lo, sin_hi)[:, 0:IDX_DIM]
    ik_ref[...] = ik
    ikb_ref[...] = ik.astype(ikb_ref.dtype)
    small_ref[...] = ik_sm[:, LANES:2 * LANES]

    u_ref[...] = jnp.dot(xn, w_ref[:, OFF_U:OFF_U + CONV_DIM], preferred_element_type=jnp.float32)
    z_ref[...] = jnp.dot(xn, w_ref[:, OFF_Z:OFF_Z + H_B * DV_B], preferred_element_type=jnp.float32)
    gl_ref[...] = jnp.dot(xn, w_ref[:, OFF_GL:OFF_GL + 2 * D_MODEL], preferred_element_type=jnp.float32)


def _pack_w_in(w_in):
    pts = np.cumsum(IN_SIZES)[:-1].tolist()
    q, k, v, iq, ik, iw, u, a, b, z, gl = jnp.split(w_in, pts, axis=-1)
    d = w_in.shape[0]
    seg_a = jnp.concatenate([q, k, v, iq, ik, jnp.zeros((d, SEG_A_PAD - SEG_A), w_in.dtype)], axis=1)
    small = jnp.concatenate([iw, a, b, jnp.zeros((d, SMALL_W - IDX_HEADS - 2 * H_B), w_in.dtype)], axis=1)
    return jnp.concatenate([seg_a, small, u, z, gl], axis=1).astype(MXU_DTYPE)


def _rope_tables(pos):
    rot = HEAD_DIM_A // 4
    half = rot // 2
    inv_freq = ROPE_THETA ** (-jnp.arange(half, dtype=jnp.float32) / half)
    ang = pos.astype(jnp.float32)[:, None] * inv_freq[None, :]
    cos, sin = jnp.cos(ang), jnp.sin(ang)
    rows = pos.shape[0]
    one = jnp.ones((rows, HEAD_DIM_A - rot), jnp.float32)
    zero = jnp.zeros((rows, HEAD_DIM_A - rot), jnp.float32)
    zh = jnp.zeros((rows, half), jnp.float32)
    cos_h = jnp.concatenate([cos, cos, one], axis=1)
    slo_h = jnp.concatenate([-sin, zh, zero], axis=1)
    shi_h = jnp.concatenate([zh, sin, zero], axis=1)
    rep = LANES // HEAD_DIM_A
    return jnp.tile(cos_h, (1, rep)), jnp.tile(slo_h, (1, rep)), jnp.tile(shi_h, (1, rep))


def _in_proj(x2d, pos_tables, n_table_blocks, tm, norm_mix, w_packed, q_norm, k_norm):
    n = x2d.shape[0]
    assert n % tm == 0
    cos_t, sin_lo, sin_hi = pos_tables
    bd = jnp.kron(jnp.eye(BRANCH_WIDTH // HEAD_DIM_A, dtype=jnp.float32),
                  jnp.full((HEAD_DIM_A, HEAD_DIM_A), 1.0 / HEAD_DIM_A, jnp.float32)).astype(MXU_DTYPE)
    qg = jnp.tile(q_norm.astype(jnp.float32), BRANCH_WIDTH // HEAD_DIM_A)[None, :]
    kg = jnp.tile(k_norm.astype(jnp.float32), N_KV_A)[None, :]
    kw = N_KV_A * HEAD_DIM_A
    row = lambda w: pl.BlockSpec((tm, w), lambda i: (i, 0))
    full = lambda a: pl.BlockSpec(a.shape, lambda i: (0,) * a.ndim)
    tab = pl.BlockSpec((tm, LANES), lambda i: (i % n_table_blocks, 0))
    out_shape = (
        jax.ShapeDtypeStruct((N_HEADS_A, n, LANES), MXU_DTYPE),
        jax.ShapeDtypeStruct((n, 2 * kw), jnp.float32),
        jax.ShapeDtypeStruct((n, kw), MXU_DTYPE),
        jax.ShapeDtypeStruct((n, 2 * kw), MXU_DTYPE),
        jax.ShapeDtypeStruct((IDX_HEADS, n, IDX_DIM), MXU_DTYPE),
        jax.ShapeDtypeStruct((n, IDX_DIM), jnp.float32),
        jax.ShapeDtypeStruct((n, IDX_DIM), MXU_DTYPE),
        jax.ShapeDtypeStruct((n, SMALL_W), jnp.float32),
        jax.ShapeDtypeStruct((n, CONV_DIM), jnp.float32),
        jax.ShapeDtypeStruct((n, H_B * DV_B), jnp.float32),
        jax.ShapeDtypeStruct((n, 2 * D_MODEL), jnp.float32),
    )
    out_specs = (
        pl.BlockSpec((N_HEADS_A, tm, LANES), lambda i: (0, i, 0)),
        row(2 * kw), row(kw), row(2 * kw),
        pl.BlockSpec((IDX_HEADS, tm, IDX_DIM), lambda i: (0, i, 0)),
        row(IDX_DIM), row(IDX_DIM), row(SMALL_W), row(CONV_DIM), row(H_B * DV_B), row(2 * D_MODEL),
    )
    return pl.pallas_call(
        _in_proj_kernel,
        grid=(n // tm,),
        in_specs=[row(D_MODEL), full(norm_mix[None, :]), full(w_packed), full(bd), full(qg), full(kg),
                  tab, tab, tab],
        out_specs=out_specs,
        out_shape=out_shape,
        compiler_params=pltpu.CompilerParams(dimension_semantics=("arbitrary",),
                                             vmem_limit_bytes=VMEM_LIMIT_BYTES),
        name="in_proj",
    )(x2d, norm_mix[None, :].astype(jnp.float32), w_packed, bd, qg, kg, cos_t, sin_lo, sin_hi)


_INT_MAG = 0x7FFFFFFF


def _f32_key(x):
    b = lax.bitcast_convert_type(x, jnp.int32)
    return b ^ (lax.shift_right_arithmetic(b, 31) & _INT_MAG)


def _key_f32(k):
    b = k ^ (lax.shift_right_arithmetic(k, 31) & _INT_MAG)
    return lax.bitcast_convert_type(b, jnp.float32)


def _wide(x, reps):
    return x if reps == 1 else jnp.concatenate([x] * reps, axis=1)


def _fold(x):
    acc = x[:, 0:LANES]
    for j in range(1, x.shape[1] // LANES):
        acc = acc + x[:, j * LANES:(j + 1) * LANES]
    return acc


def _topk_threshold(count, row_min, row_max, n_adm, topk, n_keys, linear_steps=14):
    kf = jnp.float32(topk)
    need = n_adm > topk
    lo_k = _f32_key(row_min)
    hi_k = _f32_key(row_max) + 1
    thr = jnp.where(need, row_min, -jnp.inf)
    zero = jnp.zeros_like(row_min)
    state = (jnp.int32(0), jnp.int32(1), lo_k, hi_k, thr, zero, jnp.where(need, 0, 1).astype(jnp.int32),
             jnp.zeros_like(lo_k))

    def cond(st):
        it, active = st[0], st[1]
        return jnp.logical_and(it < 80, active > 0)

    def body(st):
        it, _, lo_k, hi_k, thr, cnt_hi, done, tie = st
        adjacent = hi_k == lo_k + 1
        lo_f, hi_f = _key_f32(lo_k), _key_f32(hi_k)
        mid_lin = _f32_key(lo_f + 0.5 * (hi_f - lo_f))
        mid_lin = jnp.minimum(jnp.maximum(mid_lin, lo_k + 1), hi_k - 1)
        mid_int = (lo_k & hi_k) + lax.shift_right_arithmetic(lo_k ^ hi_k, 1)
        mid = jnp.where(it < linear_steps, mid_lin, mid_int)
        mid_f = _key_f32(mid)
        cnt = count(lambda s, kpos: s >= _wide(mid_f, s.shape[1] // LANES))
        live = jnp.logical_and(done == 0, jnp.logical_not(adjacent))
        hit = jnp.logical_and(live, cnt == kf)
        up = jnp.logical_and(live, cnt > kf)
        dn = jnp.logical_and(live, cnt < kf)
        new_tie = jnp.logical_and(done == 0, adjacent)
        thr = jnp.where(hit, mid_f, jnp.where(new_tie, lo_f, thr))
        tie = jnp.where(new_tie, 1, tie)
        done = jnp.where(jnp.logical_or(hit, new_tie), 1, done)
        lo_k = jnp.where(up, mid, lo_k)
        hi_k = jnp.where(dn, mid, hi_k)
        cnt_hi = jnp.where(dn, cnt, cnt_hi)
        active = jnp.max(1 - done)
        return (it + 1, active, lo_k, hi_k, thr, cnt_hi, done, tie)

    st = lax.while_loop(cond, body, state)
    thr, cnt_hi, tie = st[4], st[5], st[7]

    need_ties = kf - cnt_hi
    n_bits = max(1, int(math.ceil(math.log2(n_keys + 1))))
    any_tie = jnp.max(tie)

    def tie_body(_, lm):
        lo_m, hi_m = lm
        mid = lax.shift_right_arithmetic(lo_m + hi_m, 1)
        g = count(lambda s, kpos: jnp.logical_and(s == _wide(thr, s.shape[1] // LANES),
                                                  kpos < _wide(mid, s.shape[1] // LANES)))
        ge = g >= need_ties
        return jnp.where(ge, lo_m, mid), jnp.where(ge, mid, hi_m)

    lo_m0 = jnp.zeros_like(lo_k)
    hi_m0 = jnp.full_like(lo_k, n_keys)
    _, hi_m = lax.fori_loop(0, jnp.where(any_tie > 0, n_bits + 1, 0), tie_body, (lo_m0, hi_m0))
    cut = jnp.where(tie > 0, hi_m, n_keys + 1)
    return thr, cut


Q_TILE = 128
K_TILE = 256


def _dsa_prompt_kernel(topk, qexp_ref, iq_ref, small_ref, kb_ref, vext_ref, ikb_ref, o_ref,
                       sc_ref, wb_ref, acc_ref, m_ref):
    i = pl.program_id(1)
    tq, kc = Q_TILE, K_TILE
    reps = kc // LANES
    n_keys = sc_ref.shape[0] * kc
    nchunk = (i + 2) // 2
    qpos = i * tq + lax.broadcasted_iota(jnp.int32, (tq, LANES), 0)
    qpos_w = _wide(qpos, reps)
    lane_w = lax.broadcasted_iota(jnp.int32, (tq, kc), 1)

    sm = small_ref[...]
    s_scale = IDX_DIM ** -0.5 * IDX_HEADS ** -0.5
    for h in range(IDX_HEADS):
        wb_ref[h] = jnp.broadcast_to(sm[:, h:h + 1], (tq, kc))
    iq_all = iq_ref[...].reshape(IDX_HEADS * tq, IDX_DIM)

    def score_tile(c, carry):
        k0 = pl.multiple_of(c * kc, kc)
        d = _mm_nt(iq_all, ikb_ref[pl.ds(k0, kc), :])
        s = wb_ref[0] * jnp.maximum(d[0:tq], 0.0)
        for h in range(1, IDX_HEADS):
            s = s + wb_ref[h] * jnp.maximum(d[h * tq:(h + 1) * tq], 0.0)
        sc_ref[c] = jnp.where(k0 + lane_w <= qpos_w, s * s_scale, NEG_INF)
        return carry

    lax.fori_loop(0, nchunk, score_tile, 0)

    def count(pred):
        def body(c, acc):
            s = sc_ref[c]
            kpos = c * kc + lane_w
            ok = jnp.logical_and(pred(s, kpos), kpos <= qpos_w)
            return acc + _fold(jnp.where(ok, 1.0, 0.0))
        acc = lax.fori_loop(0, nchunk, body, jnp.zeros((tq, LANES), jnp.float32))
        return jnp.broadcast_to(jnp.sum(acc, axis=1, keepdims=True), (tq, LANES))

    def minmax(c, mm):
        s = sc_ref[c]
        adm = c * kc + lane_w <= qpos_w
        lo = jnp.where(adm, s, jnp.inf)
        hi = jnp.where(adm, s, -jnp.inf)
        for j in range(reps):
            mm = (jnp.minimum(mm[0], lo[:, j * LANES:(j + 1) * LANES]),
                  jnp.maximum(mm[1], hi[:, j * LANES:(j + 1) * LANES]))
        return mm

    mn, mx = lax.fori_loop(0, nchunk, minmax, (jnp.full((tq, LANES), jnp.inf, jnp.float32),
                                               jnp.full((tq, LANES), -jnp.inf, jnp.float32)))
    row_min = jnp.broadcast_to(jnp.min(mn, axis=1, keepdims=True), (tq, LANES))
    row_max = jnp.broadcast_to(jnp.max(mx, axis=1, keepdims=True), (tq, LANES))
    thr, cut = _topk_threshold(count, row_min, row_max, qpos + 1, topk, n_keys)
    thr_w, cut_w = _wide(thr, reps), _wide(cut, reps)

    m_ref[...] = jnp.full(m_ref.shape, 0.5 * NEG_INF, jnp.float32)
    acc_ref[...] = jnp.zeros(acc_ref.shape, jnp.float32)
    q_all = qexp_ref[...].reshape(N_HEADS_A * tq, LANES)

    def attend_tile(c, carry):
        k0 = pl.multiple_of(c * kc, kc)
        s = sc_ref[c]
        kpos = k0 + lane_w
        sel = jnp.logical_or(s > thr_w, jnp.logical_and(s == thr_w, kpos < cut_w))
        sel = jnp.logical_and(sel, kpos <= qpos_w)
        bias = jnp.where(sel, 0.0, NEG_INF)
        lg = _mm_nt(q_all, kb_ref[pl.ds(k0, kc), :])
        ps = []
        alphas = []
        for h in range(N_HEADS_A):
            rows = slice(h * tq, (h + 1) * tq)
            lgh = lg[rows] + bias
            m_old = m_ref[rows]
            m_new = jnp.maximum(m_old, jnp.max(lgh, axis=1, keepdims=True))
            alphas.append(jnp.exp(m_old - m_new))
            ps.append(jnp.exp(lgh - _wide(m_new, reps)).astype(MXU_DTYPE))
            m_ref[rows] = m_new
        pv = jnp.dot(jnp.concatenate(ps, axis=0), vext_ref[pl.ds(k0, kc), :],
                     preferred_element_type=jnp.float32)
        for h in range(N_HEADS_A):
            rows = slice(h * tq, (h + 1) * tq)
            acc_ref[rows] = acc_ref[rows] * _wide(alphas[h], 2) + pv[rows]
        return carry

    lax.fori_loop(0, nchunk, attend_tile, 0)

    lo_half = lax.broadcasted_iota(jnp.int32, (tq, LANES), 1) < HEAD_DIM_A
    outs = []
    for h in range(N_HEADS_A):
        a = acc_ref[h * tq:(h + 1) * tq]
        outs.append(a[:, 0:LANES] / a[:, LANES:2 * LANES])
    for p in range(N_HEADS_A // 2):
        n = (2 * p) // GROUP_A
        even, odd = outs[2 * p], outs[2 * p + 1]
        if n == 0:
            t = jnp.where(lo_half, even, pltpu.roll(odd, HEAD_DIM_A, 1))
        else:
            t = jnp.where(lo_half, pltpu.roll(even, HEAD_DIM_A, 1), odd)
        o_ref[:, p * LANES:(p + 1) * LANES] = t.astype(o_ref.dtype)


def _dsa_prompt(qexp, iqhm, small, kb, vext, ikb, batch, seq):
    tq, kc = Q_TILE, K_TILE
    assert seq % kc == 0
    nq = seq // tq
    n = batch * seq
    topk = min(TOPK_MAX, seq // 4)
    return pl.pallas_call(
        functools.partial(_dsa_prompt_kernel, topk),
        grid=(batch, nq),
        in_specs=[
            pl.BlockSpec((N_HEADS_A, tq, LANES), lambda b, i: (0, b * nq + i, 0)),
            pl.BlockSpec((IDX_HEADS, tq, IDX_DIM), lambda b, i: (0, b * nq + i, 0)),
            pl.BlockSpec((tq, SMALL_W), lambda b, i: (b * nq + i, 0)),
            pl.BlockSpec((seq, LANES), lambda b, i: (b, 0)),
            pl.BlockSpec((seq, 2 * LANES), lambda b, i: (b, 0)),
            pl.BlockSpec((seq, IDX_DIM), lambda b, i: (b, 0)),
        ],
        out_specs=pl.BlockSpec((tq, BRANCH_WIDTH), lambda b, i: (b * nq + i, 0)),
        out_shape=jax.ShapeDtypeStruct((n, BRANCH_WIDTH), MXU_DTYPE),
        scratch_shapes=[
            pltpu.VMEM((seq // kc, tq, kc), jnp.float32),
            pltpu.VMEM((IDX_HEADS, tq, kc), jnp.float32),
            pltpu.VMEM((N_HEADS_A * tq, 2 * LANES), jnp.float32),
            pltpu.VMEM((N_HEADS_A * tq, LANES), jnp.float32),
        ],
        compiler_params=pltpu.CompilerParams(dimension_semantics=("arbitrary", "arbitrary"),
                                             vmem_limit_bytes=VMEM_LIMIT_BYTES),
        name="dsa_prompt",
    )(qexp, iqhm, small, kb, vext, ikb)


def _page_copy(pt_ref, cache_ref, buf_ref, sem_ref, seq, page, slot):
    return pltpu.make_async_copy(cache_ref.at[pt_ref[seq, page]], buf_ref.at[slot, page], sem_ref.at[slot])


def _pages_start(pt_ref, cache_ref, buf_ref, sem_ref, seq, slot, n_pages):
    def body(p, c):
        _page_copy(pt_ref, cache_ref, buf_ref, sem_ref, seq, p, slot).start()
        return c
    lax.fori_loop(0, n_pages, body, 0)


def _pages_wait(pt_ref, cache_ref, buf_ref, sem_ref, seq, slot, n_pages):
    def body(p, c):
        _page_copy(pt_ref, cache_ref, buf_ref, sem_ref, seq, p, slot).wait()
        return c
    lax.fori_loop(0, n_pages, body, 0)


def _dsa_sample_score_kernel(topk, n_pages, pt_ref, iq_ref, iw_ref, iknew_ref, cache_ik_ref,
                             sc_ref, thr_ref, cut_ref, ikbuf_ref, sem_ref):
    s = pl.program_id(0)
    n_seq = pl.num_programs(0)
    past = n_pages * PAGE_SIZE
    n_tiles = sc_ref.shape[0]
    slot = s % 2

    @pl.when(s == 0)
    def _():
        _pages_start(pt_ref, cache_ik_ref, ikbuf_ref, sem_ref, 0, 0, n_pages)

    @pl.when(s + 1 < n_seq)
    def _():
        _pages_start(pt_ref, cache_ik_ref, ikbuf_ref, sem_ref, s + 1, 1 - slot, n_pages)

    _pages_wait(pt_ref, cache_ik_ref, ikbuf_ref, sem_ref, s, slot, n_pages)

    iq = iq_ref[0]
    w = iw_ref[0]
    s_scale = IDX_DIM ** -0.5 * IDX_HEADS ** -0.5

    def page_scores(p, c):
        d = _mm(iq, ikbuf_ref[slot, p])
        sc_ref[p, pl.ds(s, 1), :] = jnp.sum(w * jnp.maximum(d, 0.0), axis=0, keepdims=True) * s_scale
        return c

    lax.fori_loop(0, n_pages, page_scores, 0)
    ik_new = iknew_ref[0].astype(MXU_DTYPE).astype(jnp.float32)
    d_self = jnp.sum(iq.astype(jnp.float32) * ik_new, axis=1, keepdims=True)
    s_self = jnp.sum(w * jnp.maximum(d_self, 0.0), axis=0, keepdims=True) * s_scale
    lane1 = lax.broadcasted_iota(jnp.int32, (1, LANES), 1)
    sc_ref[n_tiles - 1, pl.ds(s, 1), :] = jnp.where(lane1 == 0, s_self, NEG_INF)

    @pl.when(s == n_seq - 1)
    def _():
        rows = sc_ref.shape[1]
        lane = lax.broadcasted_iota(jnp.int32, (rows, LANES), 1)

        def count(pred):
            def body(j, acc):
                kpos = j * LANES + lane
                ok = jnp.logical_and(pred(sc_ref[j], kpos), kpos <= past)
                return acc + jnp.where(ok, 1.0, 0.0)
            acc = lax.fori_loop(0, n_tiles, body, jnp.zeros((rows, LANES), jnp.float32))
            return jnp.broadcast_to(jnp.sum(acc, axis=1, keepdims=True), (rows, LANES))

        def minmax(j, mm):
            t = sc_ref[j]
            adm = j * LANES + lane <= past
            return (jnp.minimum(mm[0], jnp.where(adm, t, jnp.inf)),
                    jnp.maximum(mm[1], jnp.where(adm, t, -jnp.inf)))

        mn, mx = lax.fori_loop(0, n_tiles, minmax, (jnp.full((rows, LANES), jnp.inf, jnp.float32),
                                                    jnp.full((rows, LANES), -jnp.inf, jnp.float32)))
        row_min = jnp.broadcast_to(jnp.min(mn, axis=1, keepdims=True), (rows, LANES))
        row_max = jnp.broadcast_to(jnp.max(mx, axis=1, keepdims=True), (rows, LANES))
        n_adm = jnp.full((rows, LANES), past + 1, jnp.int32)
        thr, cut = _topk_threshold(count, row_min, row_max, n_adm, topk, n_tiles * LANES)
        thr_ref[...] = thr
        cut_ref[...] = cut


def _dsa_sample_attend_kernel(n_pages, pt_ref, q_ref, sc_ref, thr_ref, cut_ref, kvnew_ref,
                              cache_k_ref, cache_v_ref, o_ref, kbuf_ref, vbuf_ref, lg_ref, ksem_ref, vsem_ref):
    s = pl.program_id(0)
    n_seq = pl.num_programs(0)
    past = n_pages * PAGE_SIZE
    slot = s % 2

    def start(seq, sl):
        _pages_start(pt_ref, cache_k_ref, kbuf_ref, ksem_ref, seq, sl, n_pages)
        _pages_start(pt_ref, cache_v_ref, vbuf_ref, vsem_ref, seq, sl, n_pages)

    @pl.when(s == 0)
    def _():
        start(0, 0)

    @pl.when(s + 1 < n_seq)
    def _():
        start(s + 1, 1 - slot)

    _pages_wait(pt_ref, cache_k_ref, kbuf_ref, ksem_ref, s, slot, n_pages)
    _pages_wait(pt_ref, cache_v_ref, vbuf_ref, vsem_ref, s, slot, n_pages)

    q = q_ref[0]
    thr = thr_ref[0][:, 0:1]
    cut = cut_ref[0][:, 0:1]
    kw = N_KV_A * HEAD_DIM_A
    k_new = kvnew_ref[0][:, 0:kw].astype(MXU_DTYPE).astype(jnp.float32)
    v_new = kvnew_ref[0][:, kw:2 * kw].astype(MXU_DTYPE).astype(jnp.float32)
    lane1 = lax.broadcasted_iota(jnp.int32, (1, LANES), 1)

    def selected(srow, kpos):
        return jnp.logical_or(srow > thr, jnp.logical_and(srow == thr, kpos < cut))

    def page_logits(p, mx):
        srow = sc_ref[0, pl.ds(p, 1), :]
        bias = jnp.where(selected(srow, p * PAGE_SIZE + lane1), 0.0, NEG_INF)
        lg = _mm(q, kbuf_ref[slot, p]) + bias
        lg_ref[p] = lg
        return jnp.maximum(mx, lg)

    mx = lax.fori_loop(0, n_pages, page_logits, jnp.full((N_HEADS_A, LANES), NEG_INF, jnp.float32))
    s_self = sc_ref[0, pl.ds(n_pages, 1), :][:, 0:1]
    lg_self = jnp.sum(q.astype(jnp.float32) * k_new, axis=1, keepdims=True)
    lg_self = jnp.where(selected(s_self, past), lg_self, NEG_INF)
    m = jnp.maximum(jnp.max(mx, axis=1, keepdims=True), lg_self)

    def page_out(p, carry):
        lsum, o = carry
        pr = jnp.exp(lg_ref[p] - m)
        return lsum + pr, o + _mm_nt(pr, vbuf_ref[slot, p])

    zero = jnp.zeros((N_HEADS_A, LANES), jnp.float32)
    lsum, o = lax.fori_loop(0, n_pages, page_out, (zero, zero))
    p_self = jnp.exp(lg_self - m)
    denom = jnp.sum(lsum, axis=1, keepdims=True) + p_self
    o = (o + p_self * v_new) / denom
    parts = []
    for h in range(N_HEADS_A):
        n = h // GROUP_A
        parts.append(o[h:h + 1, n * HEAD_DIM_A:(n + 1) * HEAD_DIM_A])
    o_ref[0] = jnp.concatenate(parts, axis=1)


def _dsa_sample(qexp, iqhm, small, ik_new, kv_new, cache_k, cache_v, cache_ik, page_table):
    db, n_pages = page_table.shape
    past = n_pages * PAGE_SIZE
    n_pool = cache_ik.shape[0]
    topk = min(TOPK_MAX, (past + 1) // 4)
    n_tiles = n_pages + 1
    kw = N_KV_A * HEAD_DIM_A
    q_s = jnp.swapaxes(qexp, 0, 1)
    iq_s = jnp.swapaxes(iqhm, 0, 1)
    iw_s = small[:, 0:IDX_HEADS].reshape(db, IDX_HEADS, 1)
    ck_t = jnp.transpose(cache_k, (0, 2, 3, 1)).reshape(n_pool, kw, PAGE_SIZE)
    cv_t = jnp.transpose(cache_v, (0, 2, 3, 1)).reshape(n_pool, kw, PAGE_SIZE)
    cik_t = jnp.swapaxes(cache_ik, 1, 2)
    cparams = pltpu.CompilerParams(dimension_semantics=("arbitrary",), vmem_limit_bytes=VMEM_LIMIT_BYTES)
    per_seq = lambda *shape: pl.BlockSpec((1,) + shape, lambda s, pt: (s,) + (0,) * len(shape))
    whole = lambda *shape: pl.BlockSpec(shape, lambda s, pt: (0,) * len(shape))
    any_spec = pl.BlockSpec(memory_space=pl.ANY)

    sc, thr, cut = pl.pallas_call(
        functools.partial(_dsa_sample_score_kernel, topk, n_pages),
        grid_spec=pltpu.PrefetchScalarGridSpec(
            num_scalar_prefetch=1,
            grid=(db,),
            in_specs=[per_seq(IDX_HEADS, IDX_DIM), per_seq(IDX_HEADS, 1), per_seq(1, IDX_DIM), any_spec],
            out_specs=(whole(n_tiles, db, LANES), whole(db, LANES), whole(db, LANES)),
            scratch_shapes=[pltpu.VMEM((2, n_pages, IDX_DIM, PAGE_SIZE), jnp.float32),
                            pltpu.SemaphoreType.DMA((2,))],
        ),
        out_shape=(jax.ShapeDtypeStruct((n_tiles, db, LANES), jnp.float32),
                   jax.ShapeDtypeStruct((db, LANES), jnp.float32),
                   jax.ShapeDtypeStruct((db, LANES), jnp.int32)),
        compiler_params=cparams,
        name="dsa_sample_score",
    )(page_table, iq_s, iw_s, ik_new.reshape(db, 1, IDX_DIM), cik_t)

    o = pl.pallas_call(
        functools.partial(_dsa_sample_attend_kernel, n_pages),
        grid_spec=pltpu.PrefetchScalarGridSpec(
            num_scalar_prefetch=1,
            grid=(db,),
            in_specs=[per_seq(N_HEADS_A, LANES), per_seq(n_tiles, LANES), per_seq(1, LANES), per_seq(1, LANES),
                      per_seq(1, 2 * kw), any_spec, any_spec],
            out_specs=per_seq(1, BRANCH_WIDTH),
            scratch_shapes=[pltpu.VMEM((2, n_pages, kw, PAGE_SIZE), jnp.float32),
                            pltpu.VMEM((2, n_pages, kw, PAGE_SIZE), jnp.float32),
                            pltpu.VMEM((n_pages, N_HEADS_A, LANES), jnp.float32),
                            pltpu.SemaphoreType.DMA((2,)), pltpu.SemaphoreType.DMA((2,))],
        ),
        out_shape=jax.ShapeDtypeStruct((db, 1, BRANCH_WIDTH), jnp.float32),
        compiler_params=cparams,
        name="dsa_sample_attend",
    )(page_table, q_s, jnp.swapaxes(sc, 0, 1), thr.reshape(db, 1, LANES), cut.reshape(db, 1, LANES),
      kv_new.reshape(db, 1, 2 * kw), ck_t, cv_t)
    return o.reshape(db, BRANCH_WIDTH)


GDN_CHUNK = 128
A_LANE = IDX_HEADS
B_LANE = IDX_HEADS + H_B


def _split2(x):
    h = x.astype(MXU_DTYPE)
    return h, (x - h.astype(jnp.float32)).astype(MXU_DTYPE)


def _mm2(a, b):
    a1, a2 = _split2(a)
    b1, b2 = _split2(b)
    d = functools.partial(jnp.dot, preferred_element_type=jnp.float32)
    return d(a1, b1) + (d(a1, b2) + d(a2, b1))


def _unit_lower_inverse(a):
    n = a.shape[0]
    eye = (lax.broadcasted_iota(jnp.int32, (n, n), 0) == lax.broadcasted_iota(jnp.int32, (n, n), 1))
    s = jnp.where(eye, 1.0, 0.0) - a
    p = _mm2(a, a)
    k = 2
    while k < n:
        s = s + _mm2(s, p)
        k *= 2
        if k < n:
            p = _mm2(p, p)
    return s


def _l2norm(x):
    return x * lax.rsqrt(jnp.sum(x * x, axis=-1, keepdims=True) + NORM_EPS)


def _gdn_prompt_kernel(u_ref, small_ref, z_ref, conv0_ref, s0_ref, wconv_ref, alog_ref, dtb_ref, dnorm_ref,
                       o_ref, conv_out_ref, s_out_ref, ucat_ref, state_ref):
    n = pl.program_id(1)
    c = GDN_CHUNK
    head = SUBLANES
    tail = CONV_K - 1

    @pl.when(n == 0)
    def _():
        ucat_ref[head - tail:head, :] = conv0_ref[0]
        state_ref[...] = s0_ref[0]

    ucat_ref[head:head + c, :] = u_ref[...]
    y = wconv_ref[tail:tail + 1, :] * ucat_ref[head:head + c, :]
    for j in range(tail):
        y = y + wconv_ref[j:j + 1, :] * ucat_ref[head - tail + j:head - tail + j + c, :]
    cv = _silu(y)
    carry_rows = ucat_ref[head + c - tail:head + c, :]
    ucat_ref[head - tail:head, :] = carry_rows
    conv_out_ref[0] = carry_rows

    sm = small_ref[...]
    g_all = -jnp.exp(alog_ref[...]) * _softplus(sm + dtb_ref[...])
    beta_all = _sigmoid(sm)
    row = lax.broadcasted_iota(jnp.int32, (c, c), 0)
    col = lax.broadcasted_iota(jnp.int32, (c, c), 1)
    lower = row >= col
    strict = row > col
    a1, a2, a3 = _split3(g_all)
    ltri = jnp.where(lower, 1.0, 0.0).astype(MXU_DTYPE)
    d = functools.partial(jnp.dot, preferred_element_type=jnp.float32)
    gc_all = d(ltri, a1) + (d(ltri, a2) + d(ltri, a3))

    qw = H_B * DK_B
    for h in range(H_B):
        qh = _l2norm(cv[:, h * DK_B:(h + 1) * DK_B]) * (DK_B ** -0.5)
        kh = _l2norm(cv[:, qw + h * DK_B:qw + (h + 1) * DK_B])
        vh = cv[:, 2 * qw + h * DV_B:2 * qw + (h + 1) * DV_B]
        beta_b = jnp.broadcast_to(beta_all[:, B_LANE + h:B_LANE + h + 1], (c, LANES))
        gcol = jnp.broadcast_to(gc_all[:, A_LANE + h:A_LANE + h + 1], (c, c))
        grow = gcol.T
        diff = gcol - grow
        decay = jnp.where(lower, jnp.exp(jnp.where(lower, diff, 0.0)), 0.0)
        eg = jnp.exp(gcol)
        g_last = gcol[c - 1:c, :]
        kb = kh * beta_b
        vb = vh * beta_b
        a_mat = jnp.where(strict, _mm_nt(kb, kh) * decay, 0.0)
        t_inv = _unit_lower_inverse(a_mat)
        sol = _mm2(t_inv, jnp.concatenate([vb, kb * eg], axis=1))
        u_i, w_i = sol[:, 0:DV_B], sol[:, DV_B:DV_B + DK_B]
        qk = _mm_nt(qh, kh) * decay
        s_h = state_ref[h]
        v_new = u_i - _mm(w_i, s_h)
        o_h = _mm(qh * eg, s_h) + _mm(qk, v_new)
        k_dec = kh * jnp.exp(g_last - gcol)
        state_ref[h] = s_h * jnp.exp(g_last) + _mm_tn(k_dec, v_new)
        ms = jnp.mean(o_h * o_h, axis=-1, keepdims=True)
        o_n = o_h * lax.rsqrt(ms + NORM_EPS) * dnorm_ref[...]
        o_ref[:, h * DV_B:(h + 1) * DV_B] = (o_n * _silu(z_ref[:, h * DV_B:(h + 1) * DV_B])).astype(o_ref.dtype)

    @pl.when(n == pl.num_programs(1) - 1)
    def _():
        s_out_ref[0] = state_ref[...]


GDN_SEQ_TILE = 8


def _gdn_sample_kernel(u_ref, cb_ref, small_ref, z_ref, s0_ref, wconv_ref, alog_ref, dtb_ref, dnorm_ref,
                       o_ref, conv_out_ref, s_out_ref):
    ts = GDN_SEQ_TILE
    tail = CONV_K - 1
    u_new = u_ref[...]
    y = wconv_ref[tail:tail + 1, :] * u_new
    for j in range(tail):
        y = y + wconv_ref[j:j + 1, :] * cb_ref[j]
    cv = _silu(y)
    for j in range(tail - 1):
        conv_out_ref[j] = cb_ref[j + 1]
    conv_out_ref[tail - 1] = u_new

    sm = small_ref[...]
    eg_all = jnp.exp(-jnp.exp(alog_ref[...]) * _softplus(sm + dtb_ref[...]))
    beta_all = _sigmoid(sm)
    qw = H_B * DK_B
    for h in range(H_B):
        q = _l2norm(cv[:, h * DK_B:(h + 1) * DK_B]) * (DK_B ** -0.5)
        k = _l2norm(cv[:, qw + h * DK_B:qw + (h + 1) * DK_B])
        v = cv[:, 2 * qw + h * DV_B:2 * qw + (h + 1) * DV_B]
        eg = eg_all[:, A_LANE + h:A_LANE + h + 1]
        beta = beta_all[:, B_LANE + h:B_LANE + h + 1]
        qk = jnp.sum(q * k, axis=-1, keepdims=True)
        k_t, q_t = k.T, q.T
        rows = []
        for r in range(ts):
            s_old = s0_ref[r, h]
            kc = k_t[:, r:r + 1]
            ks = jnp.sum(s_old * kc, axis=0, keepdims=True)
            qs = jnp.sum(s_old * q_t[:, r:r + 1], axis=0, keepdims=True)
            eg_r = eg[r:r + 1, :]
            v_new = beta[r:r + 1, :] * (v[r:r + 1, :] - eg_r * ks)
            rows.append(eg_r * qs + qk[r:r + 1, :] * v_new)
            s_out_ref[r, h] = s_old * eg_r + kc * v_new
        o_h = jnp.concatenate(rows, axis=0)
        ms = jnp.mean(o_h * o_h, axis=-1, keepdims=True)
        o_n = o_h * lax.rsqrt(ms + NORM_EPS) * dnorm_ref[...]
        o_ref[:, h * DV_B:(h + 1) * DV_B] = (o_n * _silu(z_ref[:, h * DV_B:(h + 1) * DV_B])).astype(o_ref.dtype)


def _gdn_sample(u, small, z, conv_buf, s0, w_conv, a_log, dt_bias, delta_norm):
    db = u.shape[0]
    ts = GDN_SEQ_TILE
    assert db % ts == 0
    tail = CONV_K - 1
    alog_row, dtb_row = _gate_rows(a_log, dt_bias)
    row = lambda w: pl.BlockSpec((ts, w), lambda i: (i, 0))
    full = lambda *shape: pl.BlockSpec(shape, lambda i: (0,) * len(shape))
    cb_spec = pl.BlockSpec((tail, ts, CONV_DIM), lambda i: (0, i, 0))
    st_spec = pl.BlockSpec((ts, H_B, DK_B, DV_B), lambda i: (i, 0, 0, 0))
    o, conv_t, s_new = pl.pallas_call(
        _gdn_sample_kernel,
        grid=(db // ts,),
        in_specs=[row(CONV_DIM), cb_spec, row(SMALL_W), row(H_B * DV_B), st_spec,
                  full(CONV_K, CONV_DIM), full(1, SMALL_W), full(1, SMALL_W), full(1, DV_B)],
        out_specs=(row(H_B * DV_B), cb_spec, st_spec),
        out_shape=(jax.ShapeDtypeStruct((db, H_B * DV_B), jnp.float32),
                   jax.ShapeDtypeStruct((tail, db, CONV_DIM), jnp.float32),
                   jax.ShapeDtypeStruct((db, H_B, DK_B, DV_B), jnp.float32)),
        compiler_params=pltpu.CompilerParams(dimension_semantics=("arbitrary",),
                                             vmem_limit_bytes=VMEM_LIMIT_BYTES),
        name="gdn_sample",
    )(u, jnp.swapaxes(conv_buf, 0, 1), small, z, s0, w_conv.astype(jnp.float32), alog_row, dtb_row,
      delta_norm.astype(jnp.float32)[None, :])
    return o, jnp.swapaxes(conv_t, 0, 1), s_new


def _gate_rows(a_log, dt_bias):
    alog_row = jnp.zeros((1, SMALL_W), jnp.float32).at[0, A_LANE:A_LANE + H_B].set(a_log.astype(jnp.float32))
    dtb_row = jnp.zeros((1, SMALL_W), jnp.float32).at[0, A_LANE:A_LANE + H_B].set(dt_bias.astype(jnp.float32))
    return alog_row, dtb_row


def _gdn_prompt(u, small, z, conv0, s0, w_conv, a_log, dt_bias, delta_norm, batch, seq):
    c = GDN_CHUNK
    assert seq % c == 0
    nc = seq // c
    n = batch * seq
    alog_row, dtb_row = _gate_rows(a_log, dt_bias)
    row = lambda w: pl.BlockSpec((c, w), lambda b, i: (b * nc + i, 0))
    per_b = lambda *shape: pl.BlockSpec((1,) + shape, lambda b, i: (b,) + (0,) * len(shape))
    full = lambda *shape: pl.BlockSpec(shape, lambda b, i: (0,) * len(shape))
    return pl.pallas_call(
        _gdn_prompt_kernel,
        grid=(batch, nc),
        in_specs=[row(CONV_DIM), row(SMALL_W), row(H_B * DV_B), per_b(CONV_K - 1, CONV_DIM),
                  per_b(H_B, DK_B, DV_B), full(CONV_K, CONV_DIM), full(1, SMALL_W), full(1, SMALL_W),
                  full(1, DV_B)],
        out_specs=(row(H_B * DV_B), per_b(CONV_K - 1, CONV_DIM), per_b(H_B, DK_B, DV_B)),
        out_shape=(jax.ShapeDtypeStruct((n, H_B * DV_B), MXU_DTYPE),
                   jax.ShapeDtypeStruct((batch, CONV_K - 1, CONV_DIM), jnp.float32),
                   jax.ShapeDtypeStruct((batch, H_B, DK_B, DV_B), jnp.float32)),
        scratch_shapes=[pltpu.VMEM((SUBLANES + c, CONV_DIM), jnp.float32),
                        pltpu.VMEM((H_B, DK_B, DV_B), jnp.float32)],
        compiler_params=pltpu.CompilerParams(dimension_semantics=("arbitrary", "arbitrary"),
                                             vmem_limit_bytes=VMEM_LIMIT_BYTES),
        name="gdn_prompt",
    )(u, small, z, conv0, s0, w_conv.astype(jnp.float32), alog_row, dtb_row,
      delta_norm.astype(jnp.float32)[None, :])


def _merge_kernel(x_ref, oa_ref, ob_ref, gl_ref, wba_ref, wbb_ref, wout_ref, gain_ref, x1_ref, hn_ref):
    pa = _mm(oa_ref[...], wba_ref[...])
    pb = _mm(ob_ref[...], wbb_ref[...])
    mix = _sigmoid(gl_ref[:, 0:D_MODEL]) * pa + _sigmoid(gl_ref[:, D_MODEL:2 * D_MODEL]) * pb
    x1 = x_ref[...] + _mm(mix, wout_ref[...])
    x1_ref[...] = x1
    ms = jnp.mean(x1 * x1, axis=-1, keepdims=True)
    hn_ref[...] = (x1 * lax.rsqrt(ms + NORM_EPS) * gain_ref[...]).astype(hn_ref.dtype)


def _merge(x2d, o_a, o_b, gl, w_branch, w_out, norm_ffn, tm):
    n = x2d.shape[0]
    assert n % tm == 0
    row = lambda w: pl.BlockSpec((tm, w), lambda i: (i, 0))
    full = lambda *shape: pl.BlockSpec(shape, lambda i: (0,) * len(shape))
    return pl.pallas_call(
        _merge_kernel,
        grid=(n // tm,),
        in_specs=[row(D_MODEL), row(BRANCH_WIDTH), row(BRANCH_WIDTH), row(2 * D_MODEL),
                  full(BRANCH_WIDTH, D_MODEL), full(BRANCH_WIDTH, D_MODEL), full(D_MODEL, D_MODEL),
                  full(1, D_MODEL)],
        out_specs=(row(D_MODEL), row(D_MODEL)),
        out_shape=(jax.ShapeDtypeStruct((n, D_MODEL), jnp.float32),
                   jax.ShapeDtypeStruct((n, D_MODEL), MXU_DTYPE)),
        compiler_params=pltpu.CompilerParams(dimension_semantics=("arbitrary",),
                                             vmem_limit_bytes=VMEM_LIMIT_BYTES),
        name="merge",
    )(x2d, o_a, o_b, gl, w_branch[0].astype(MXU_DTYPE), w_branch[1].astype(MXU_DTYPE),
      w_out.astype(MXU_DTYPE), norm_ffn.astype(jnp.float32)[None, :])


FFN_TILE = D_FF // 2


def _ffn_kernel(hn_ref, x1_ref, wg_ref, wu_ref, wd_ref, y_ref, acc_ref):
    j = pl.program_id(1)

    @pl.when(j == 0)
    def _():
        acc_ref[...] = x1_ref[...]

    hn = hn_ref[...]
    g = jnp.dot(hn, wg_ref[...], preferred_element_type=jnp.float32)
    u = jnp.dot(hn, wu_ref[...], preferred_element_type=jnp.float32)
    acc_ref[...] += _mm(_silu(g) * u, wd_ref[...])

    @pl.when(j == pl.num_programs(1) - 1)
    def _():
        y_ref[...] = acc_ref[...]


def _ffn(hn, x1, w_gate_up, w_down, tm):
    n = hn.shape[0]
    tf = FFN_TILE
    assert n % tm == 0 and D_FF % tf == 0 and tf % LANES == 0
    nf = D_FF // tf
    wgu = w_gate_up.astype(MXU_DTYPE)
    return pl.pallas_call(
        _ffn_kernel,
        grid=(n // tm, nf),
        in_specs=[pl.BlockSpec((tm, D_MODEL), lambda i, j: (i, 0)),
                  pl.BlockSpec((tm, D_MODEL), lambda i, j: (i, 0)),
                  pl.BlockSpec((D_MODEL, tf), lambda i, j: (0, j)),
                  pl.BlockSpec((D_MODEL, tf), lambda i, j: (0, j + nf)),
                  pl.BlockSpec((tf, D_MODEL), lambda i, j: (j, 0))],
        out_specs=pl.BlockSpec((tm, D_MODEL), lambda i, j: (i, 0)),
        out_shape=jax.ShapeDtypeStruct((n, D_MODEL), jnp.float32),
        scratch_shapes=[pltpu.VMEM((tm, D_MODEL), jnp.float32)],
        compiler_params=pltpu.CompilerParams(dimension_semantics=("arbitrary", "arbitrary"),
                                             vmem_limit_bytes=VMEM_LIMIT_BYTES),
        name="ffn",
    )(hn, x1, wgu, wgu, w_down.astype(MXU_DTYPE))


IN_PROJ_TILE = 256
MERGE_TILE = 512
FFN_ROW_TILE = 512


def _layer(x_p, x_s, cache_k, cache_v, cache_ik, conv_s, delta_s, page_table, norm_mix, w_in, q_norm, k_norm,
           w_conv, a_log, dt_bias, delta_norm, w_branch, w_out, norm_ffn, w_gate_up, w_down):
    b, t, d = x_p.shape
    db = x_s.shape[0]
    past = page_table.shape[1] * PAGE_SIZE
    kw = N_KV_A * HEAD_DIM_A
    w_packed = _pack_w_in(w_in)

    xp2 = x_p.reshape(b * t, d)
    tm = min(IN_PROJ_TILE, t)
    assert t % tm == 0
    qexp, kv, kb, vext, iqhm, ik, ikb, small, u, z, gl = _in_proj(
        xp2, _rope_tables(jnp.arange(t)), t // tm, tm, norm_mix, w_packed, q_norm, k_norm)
    o_a = _dsa_prompt(qexp, iqhm, small, kb, vext, ikb, b, t)
    conv0 = jnp.zeros((b, CONV_K - 1, CONV_DIM), jnp.float32)
    delta0 = jnp.zeros((b, H_B, DK_B, DV_B), jnp.float32)
    o_b, conv_p, delta_p = _gdn_prompt(u, small, z, conv0, delta0, w_conv, a_log, dt_bias, delta_norm, b, t)
    x1, hn = _merge(xp2, o_a, o_b, gl, w_branch, w_out, norm_ffn, min(MERGE_TILE, b * t))
    y_p = _ffn(hn, x1, w_gate_up, w_down, min(FFN_ROW_TILE, b * t)).reshape(b, t, d)
    st_p = (kv[:, 0:kw].reshape(b, t, N_KV_A, HEAD_DIM_A), kv[:, kw:2 * kw].reshape(b, t, N_KV_A, HEAD_DIM_A),
            ik.reshape(b, t, IDX_DIM), conv_p, delta_p)

    xs2 = x_s.reshape(db, d)
    qexp, kv, kb, vext, iqhm, ik, ikb, small, u, z, gl = _in_proj(
        xs2, _rope_tables(jnp.full((db,), past, jnp.int32)), 1, db, norm_mix, w_packed, q_norm, k_norm)
    o_a = _dsa_sample(qexp, iqhm, small, ik, kv, cache_k, cache_v, cache_ik, page_table)
    o_b, conv_n, delta_n = _gdn_sample(u, small, z, conv_s, delta_s, w_conv, a_log, dt_bias, delta_norm)
    x1, hn = _merge(xs2, o_a, o_b, gl, w_branch, w_out, norm_ffn, db)
    y_s = _ffn(hn, x1, w_gate_up, w_down, db).reshape(db, 1, d)
    st_s = (kv[:, 0:kw].reshape(db, 1, N_KV_A, HEAD_DIM_A), kv[:, kw:2 * kw].reshape(db, 1, N_KV_A, HEAD_DIM_A),
            ik.reshape(db, 1, IDX_DIM), conv_n, delta_n)
    return y_p, y_s, st_p, st_s


def kernel(x_prompt, x_sample, cache_k, cache_v, cache_idx_k, state_conv, state_delta, page_table,
           norm_mix, w_in, q_norm, k_norm, w_conv, a_log, dt_bias, delta_norm, w_branch, w_out,
           norm_ffn, w_gate_up, w_down):
    assert x_sample.shape[1] == 1, "the sample group decodes one token per sequence"
    y_p, y_s = x_prompt, x_sample
    new_p, new_s = [], []
    for l in range(w_in.shape[0]):
        y_p, y_s, st_p, st_s = _layer(
            y_p, y_s, cache_k[l], cache_v[l], cache_idx_k[l], state_conv[l], state_delta[l], page_table,
            norm_mix[l], w_in[l], q_norm[l], k_norm[l], w_conv[l], a_log[l], dt_bias[l], delta_norm[l],
            w_branch[l], w_out[l], norm_ffn[l], w_gate_up[l], w_down[l])
        new_p.append(st_p)
        new_s.append(st_s)
    k_p, v_p, ik_p, conv_p, delta_p = [jnp.stack(a) for a in zip(*new_p)]
    k_s, v_s, ik_s, conv_s, delta_s = [jnp.stack(a) for a in zip(*new_s)]
    return (y_p, y_s, k_p, v_p, ik_p, conv_p, delta_p, k_s, v_s, ik_s, conv_s, delta_s)
```

```python
import functools
import math

import jax
import jax.numpy as jnp
import numpy as np
from jax import lax
from jax.experimental import pallas as pl
from jax.experimental.pallas import tpu as pltpu

D_MODEL = 1024
PAGE_SIZE = 128
N_HEADS_A = 8
N_KV_A = 2
HEAD_DIM_A = 64
GROUP_A = N_HEADS_A // N_KV_A
IDX_HEADS = 8
IDX_DIM = 64
TOPK_MAX = 256
ROPE_THETA = 500000.0
H_B = 4
DK_B = 128
DV_B = 128
CONV_K = 4
CONV_DIM = 2 * H_B * DK_B + H_B * DV_B
BRANCH_WIDTH = N_HEADS_A * HEAD_DIM_A
D_FF = -(-8 * D_MODEL // (3 * 256)) * 256
NORM_EPS = 1e-6
NEG_INF = -1e30
IN_SIZES = (N_HEADS_A * HEAD_DIM_A, N_KV_A * HEAD_DIM_A, N_KV_A * HEAD_DIM_A,
            IDX_HEADS * IDX_DIM, IDX_DIM, IDX_HEADS,
            CONV_DIM, H_B, H_B, H_B * DV_B, 2 * D_MODEL)

LANES = 128
SUBLANES = 8
VMEM_LIMIT_BYTES = 56 * 1024 * 1024

MXU_DTYPE = jnp.bfloat16

SMALL_W = LANES
SEG_A = BRANCH_WIDTH + 2 * N_KV_A * HEAD_DIM_A + IDX_HEADS * IDX_DIM + IDX_DIM
SEG_A_PAD = -(-SEG_A // LANES) * LANES
OFF_SMALL = SEG_A_PAD
OFF_U = OFF_SMALL + SMALL_W
OFF_Z = OFF_U + CONV_DIM
OFF_GL = OFF_Z + H_B * DV_B
D_IN_PACKED = OFF_GL + 2 * D_MODEL


def _mm(a, b):
    return jnp.dot(a.astype(MXU_DTYPE), b.astype(MXU_DTYPE), preferred_element_type=jnp.float32)


def _mm_nt(a, b):
    return lax.dot_general(a.astype(MXU_DTYPE), b.astype(MXU_DTYPE), (((1,), (1,)), ((), ())),
                           preferred_element_type=jnp.float32)


def _mm_tn(a, b):
    return lax.dot_general(a.astype(MXU_DTYPE), b.astype(MXU_DTYPE), (((0,), (0,)), ((), ())),
                           preferred_element_type=jnp.float32)


def _split3(x):
    x = x.astype(jnp.float32)
    h = x.astype(MXU_DTYPE)
    r = x - h.astype(jnp.float32)
    m = r.astype(MXU_DTYPE)
    l = (r - m.astype(jnp.float32)).astype(MXU_DTYPE)
    return h, m, l


def _sigmoid(x):
    return 1.0 / (1.0 + jnp.exp(-x))


def _silu(x):
    return x * _sigmoid(x)


def _softplus(x):
    return jnp.maximum(x, 0.0) + jnp.log(1.0 + jnp.exp(-jnp.abs(x)))


def _rope_tile(x, cos_t, sin_lo, sin_hi):
    half = HEAD_DIM_A // 8
    up = pltpu.roll(x, LANES - half, 1)
    dn = pltpu.roll(x, half, 1)
    return x * cos_t + up * sin_lo + dn * sin_hi


def _in_proj_kernel(key_major, x_ref, gain_ref, w_ref, bd_ref, qg_ref, kg_ref, cos_ref, slo_ref, shi_ref,
                    kv_ref, ik_ref, small_ref, u_ref, z_ref, gl_ref, *attn_refs):
    x = x_ref[...]
    ms = jnp.mean(x * x, axis=-1, keepdims=True)
    xn = (x * lax.rsqrt(ms + NORM_EPS) * gain_ref[...]).astype(MXU_DTYPE)

    cos_t, sin_lo, sin_hi = cos_ref[...], slo_ref[...], shi_ref[...]
    lane = lax.broadcasted_iota(jnp.int32, (x.shape[0], LANES), 1)
    lo_half = lane < HEAD_DIM_A

    def head_rms(t, gain):
        tt = t * t
        hi = tt.astype(MXU_DTYPE)
        lo = (tt - hi.astype(jnp.float32)).astype(MXU_DTYPE)
        bd = bd_ref[0:t.shape[1], 0:t.shape[1]]
        msq = (jnp.dot(hi, bd, preferred_element_type=jnp.float32)
               + jnp.dot(lo, bd, preferred_element_type=jnp.float32))
        return t * lax.rsqrt(msq + NORM_EPS) * gain

    q = jnp.dot(xn, w_ref[:, 0:BRANCH_WIDTH], preferred_element_type=jnp.float32)
    q = head_rms(q, qg_ref[...])
    if key_major:
        qt_ref, iqt_ref, wt_ref, kb_ref, ikb_ref, vt_ref = attn_refs
        n_qb = x.shape[0] // Q_TILE
        q_scale = HEAD_DIM_A ** -0.5 * math.log2(math.e)
    else:
        qexp_ref, iqhm_ref = attn_refs
        q_scale = HEAD_DIM_A ** -0.5
    for p in range(BRANCH_WIDTH // LANES):
        t = _rope_tile(q[:, p * LANES:(p + 1) * LANES], cos_t, sin_lo, sin_hi) * q_scale
        t_sw = pltpu.roll(t, HEAD_DIM_A, 1)
        for e in range(2):
            h = 2 * p + e
            n = h // GROUP_A
            src = t if e == n else t_sw
            keep = lo_half if n == 0 else jnp.logical_not(lo_half)
            qe = jnp.where(keep, src, 0.0)
            if key_major:
                qe_t = qe.T
                for j in range(n_qb):
                    qt_ref[j, :, h * Q_TILE:(h + 1) * Q_TILE] = qe_t[:, j * Q_TILE:(j + 1) * Q_TILE].astype(qt_ref.dtype)
            else:
                qexp_ref[h] = qe.astype(qexp_ref.dtype)

    c0 = BRANCH_WIDTH
    kw = N_KV_A * HEAD_DIM_A
    k = jnp.dot(xn, w_ref[:, c0:c0 + kw], preferred_element_type=jnp.float32)
    k = _rope_tile(head_rms(k, kg_ref[...]), cos_t, sin_lo, sin_hi)
    v = jnp.dot(xn, w_ref[:, c0 + kw:c0 + 2 * kw], preferred_element_type=jnp.float32)
    kv_ref[:, 0:kw] = k
    kv_ref[:, kw:2 * kw] = v
    if key_major:
        kb_ref[...] = k.astype(kb_ref.dtype)
        vt_ref[0, 0:kw, :] = v.T.astype(vt_ref.dtype)
        vt_ref[0, kw:kw + ONES_ROWS, :] = jnp.ones((ONES_ROWS, x.shape[0]), vt_ref.dtype)

    c1 = c0 + 2 * kw
    iqw = IDX_HEADS * IDX_DIM
    iq = jnp.dot(xn, w_ref[:, c1:c1 + iqw], preferred_element_type=jnp.float32)
    for p in range(iqw // LANES):
        t = _rope_tile(iq[:, p * LANES:(p + 1) * LANES], cos_t, sin_lo, sin_hi)
        if key_major:
            t_t = t.T
            for e in range(2):
                h = 2 * p + e
                for j in range(n_qb):
                    iqt_ref[j, :, h * Q_TILE:(h + 1) * Q_TILE] = (
                        t_t[e * IDX_DIM:(e + 1) * IDX_DIM, j * Q_TILE:(j + 1) * Q_TILE].astype(iqt_ref.dtype))
        else:
            t = t.astype(iqhm_ref.dtype)
            iqhm_ref[2 * p] = t[:, 0:IDX_DIM]
            iqhm_ref[2 * p + 1] = t[:, IDX_DIM:2 * IDX_DIM]

    c2 = c1 + iqw
    ik_sm = jnp.dot(xn, w_ref[:, c2:c2 + 2 * LANES], preferred_element_type=jnp.float32)
    ik = _rope_tile(ik_sm[:, 0:LANES], cos_t, sin_lo, sin_hi)[:, 0:IDX_DIM]
    ik_ref[...] = ik
    small = ik_sm[:, LANES:2 * LANES]
    small_ref[...] = small
    if key_major:
        ikb_ref[...] = ik.astype(ikb_ref.dtype)
        small_t = small.T
        for j in range(n_qb):
            wt_ref[j] = small_t[0:IDX_HEADS, j * Q_TILE:(j + 1) * Q_TILE]

    u_ref[...] = jnp.dot(xn, w_ref[:, OFF_U:OFF_U + CONV_DIM], preferred_element_type=jnp.float32)
    z_ref[...] = jnp.dot(xn, w_ref[:, OFF_Z:OFF_Z + H_B * DV_B], preferred_element_type=jnp.float32)
    gl_ref[...] = jnp.dot(xn, w_ref[:, OFF_GL:OFF_GL + 2 * D_MODEL], preferred_element_type=jnp.float32)


def _pack_w_in(w_in):
    pts = np.cumsum(IN_SIZES)[:-1].tolist()
    q, k, v, iq, ik, iw, u, a, b, z, gl = jnp.split(w_in, pts, axis=-1)
    d = w_in.shape[0]
    seg_a = jnp.concatenate([q, k, v, iq, ik, jnp.zeros((d, SEG_A_PAD - SEG_A), w_in.dtype)], axis=1)
    small = jnp.concatenate([iw, a, b, jnp.zeros((d, SMALL_W - IDX_HEADS - 2 * H_B), w_in.dtype)], axis=1)
    return jnp.concatenate([seg_a, small, u, z, gl], axis=1).astype(MXU_DTYPE)


def _rope_tables(pos):
    rot = HEAD_DIM_A // 4
    half = rot // 2
    inv_freq = ROPE_THETA ** (-jnp.arange(half, dtype=jnp.float32) / half)
    ang = pos.astype(jnp.float32)[:, None] * inv_freq[None, :]
    cos, sin = jnp.cos(ang), jnp.sin(ang)
    rows = pos.shape[0]
    one = jnp.ones((rows, HEAD_DIM_A - rot), jnp.float32)
    zero = jnp.zeros((rows, HEAD_DIM_A - rot), jnp.float32)
    zh = jnp.zeros((rows, half), jnp.float32)
    cos_h = jnp.concatenate([cos, cos, one], axis=1)
    slo_h = jnp.concatenate([-sin, zh, zero], axis=1)
    shi_h = jnp.concatenate([zh, sin, zero], axis=1)
    rep = LANES // HEAD_DIM_A
    return jnp.tile(cos_h, (1, rep)), jnp.tile(slo_h, (1, rep)), jnp.tile(shi_h, (1, rep))


def _in_proj(x2d, pos_tables, n_table_blocks, tm, norm_mix, w_packed, q_norm, k_norm, key_major):
    n = x2d.shape[0]
    assert n % tm == 0 and (not key_major or tm == K_TILE)
    cos_t, sin_lo, sin_hi = pos_tables
    bd = jnp.kron(jnp.eye(BRANCH_WIDTH // HEAD_DIM_A, dtype=jnp.float32),
                  jnp.full((HEAD_DIM_A, HEAD_DIM_A), 1.0 / HEAD_DIM_A, jnp.float32)).astype(MXU_DTYPE)
    qg = jnp.tile(q_norm.astype(jnp.float32), BRANCH_WIDTH // HEAD_DIM_A)[None, :]
    kg = jnp.tile(k_norm.astype(jnp.float32), N_KV_A)[None, :]
    kw = N_KV_A * HEAD_DIM_A
    row = lambda w: pl.BlockSpec((tm, w), lambda i: (i, 0))
    full = lambda a: pl.BlockSpec(a.shape, lambda i: (0,) * a.ndim)
    tab = pl.BlockSpec((tm, LANES), lambda i: (i % n_table_blocks, 0))
    out_shape = [
        jax.ShapeDtypeStruct((n, 2 * kw), jnp.float32),
        jax.ShapeDtypeStruct((n, IDX_DIM), jnp.float32),
        jax.ShapeDtypeStruct((n, SMALL_W), jnp.float32),
        jax.ShapeDtypeStruct((n, CONV_DIM), jnp.float32),
        jax.ShapeDtypeStruct((n, H_B * DV_B), jnp.float32),
        jax.ShapeDtypeStruct((n, 2 * D_MODEL), jnp.float32),
    ]
    out_specs = [row(2 * kw), row(IDX_DIM), row(SMALL_W), row(CONV_DIM), row(H_B * DV_B), row(2 * D_MODEL)]
    if key_major:
        n_qb = tm // Q_TILE
        blk = lambda r, w: pl.BlockSpec((n_qb, r, w), lambda i: (i, 0, 0))
        out_shape += [
            jax.ShapeDtypeStruct((n // Q_TILE, LANES, N_HEADS_A * Q_TILE), MXU_DTYPE),
            jax.ShapeDtypeStruct((n // Q_TILE, IDX_DIM, IDX_HEADS * Q_TILE), MXU_DTYPE),
            jax.ShapeDtypeStruct((n // Q_TILE, IDX_HEADS, Q_TILE), jnp.float32),
            jax.ShapeDtypeStruct((n, kw), MXU_DTYPE),
            jax.ShapeDtypeStruct((n, IDX_DIM), MXU_DTYPE),
            jax.ShapeDtypeStruct((n // K_TILE, kw + ONES_ROWS, K_TILE), MXU_DTYPE),
        ]
        out_specs += [blk(LANES, N_HEADS_A * Q_TILE), blk(IDX_DIM, IDX_HEADS * Q_TILE), blk(IDX_HEADS, Q_TILE),
                      row(kw), row(IDX_DIM), pl.BlockSpec((1, kw + ONES_ROWS, K_TILE), lambda i: (i, 0, 0))]
    else:
        out_shape += [jax.ShapeDtypeStruct((N_HEADS_A, n, LANES), MXU_DTYPE),
                      jax.ShapeDtypeStruct((IDX_HEADS, n, IDX_DIM), MXU_DTYPE)]
        out_specs += [pl.BlockSpec((N_HEADS_A, tm, LANES), lambda i: (0, i, 0)),
                      pl.BlockSpec((IDX_HEADS, tm, IDX_DIM), lambda i: (0, i, 0))]
    return pl.pallas_call(
        functools.partial(_in_proj_kernel, key_major),
        grid=(n // tm,),
        in_specs=[row(D_MODEL), full(norm_mix[None, :]), full(w_packed), full(bd), full(qg), full(kg),
                  tab, tab, tab],
        out_specs=tuple(out_specs),
        out_shape=tuple(out_shape),
        compiler_params=pltpu.CompilerParams(dimension_semantics=("arbitrary",),
                                             vmem_limit_bytes=VMEM_LIMIT_BYTES),
        name="in_proj",
    )(x2d, norm_mix[None, :].astype(jnp.float32), w_packed, bd, qg, kg, cos_t, sin_lo, sin_hi)


_INT_MAG = 0x7FFFFFFF


def _f32_key(x):
    b = lax.bitcast_convert_type(x, jnp.int32)
    return b ^ (lax.shift_right_arithmetic(b, 31) & _INT_MAG)


def _key_f32(k):
    b = k ^ (lax.shift_right_arithmetic(k, 31) & _INT_MAG)
    return lax.bitcast_convert_type(b, jnp.float32)


def _topk_threshold(count_ge, count_tie, row_min, row_max, n_adm, topk, n_keys, linear_steps=24):
    kf = jnp.float32(topk)
    need = n_adm > topk
    lo_k = _f32_key(row_min)
    hi_k = _f32_key(row_max) + 1
    thr = jnp.where(need, row_min, -jnp.inf)
    zero = jnp.zeros_like(row_min)
    state = (jnp.int32(0), jnp.int32(1), lo_k, hi_k, thr, zero, jnp.where(need, 0, 1).astype(jnp.int32),
             jnp.zeros_like(lo_k))

    def cond(st):
        it, active = st[0], st[1]
        return jnp.logical_and(it < 80, active > 0)

    def body(st):
        it, _, lo_k, hi_k, thr, cnt_hi, done, tie = st
        active = jnp.max(1 - done)
        adjacent = hi_k == lo_k + 1
        lo_f, hi_f = _key_f32(lo_k), _key_f32(hi_k)
        mid_lin = _f32_key(lo_f + 0.5 * (hi_f - lo_f))
        mid_lin = jnp.minimum(jnp.maximum(mid_lin, lo_k + 1), hi_k - 1)
        mid_int = (lo_k & hi_k) + lax.shift_right_arithmetic(lo_k ^ hi_k, 1)
        mid = jnp.where(it < linear_steps, mid_lin, mid_int)
        mid_f = _key_f32(mid)
        cnt = count_ge(mid_f)
        live = jnp.logical_and(done == 0, jnp.logical_not(adjacent))
        hit = jnp.logical_and(live, cnt == kf)
        up = jnp.logical_and(live, cnt > kf)
        dn = jnp.logical_and(live, cnt < kf)
        new_tie = jnp.logical_and(done == 0, adjacent)
        thr = jnp.where(hit, mid_f, jnp.where(new_tie, lo_f, thr))
        tie = jnp.where(new_tie, 1, tie)
        done = jnp.where(jnp.logical_or(hit, new_tie), 1, done)
        lo_k = jnp.where(up, mid, lo_k)
        hi_k = jnp.where(dn, mid, hi_k)
        cnt_hi = jnp.where(dn, cnt, cnt_hi)
        return (it + 1, active, lo_k, hi_k, thr, cnt_hi, done, tie)

    st = lax.while_loop(cond, body, state)
    thr, cnt_hi, tie = st[4], st[5], st[7]

    need_ties = kf - cnt_hi
    n_bits = max(1, int(math.ceil(math.log2(n_keys + 1))))
    any_tie = jnp.max(tie)

    def tie_body(_, lm):
        lo_m, hi_m = lm
        mid = lax.shift_right_arithmetic(lo_m + hi_m, 1)
        ge = count_tie(thr, mid) >= need_ties
        return jnp.where(ge, lo_m, mid), jnp.where(ge, mid, hi_m)

    lo_m0 = jnp.zeros_like(lo_k)
    hi_m0 = jnp.full_like(lo_k, n_keys)
    _, hi_m = lax.fori_loop(0, jnp.where(any_tie > 0, n_bits + 1, 0), tie_body, (lo_m0, hi_m0))
    cut = jnp.where(tie > 0, hi_m, n_keys + 1)
    return thr, cut


Q_TILE = 128
K_TILE = 256
K_UNROLL = 2
SCAN_UNROLL = 4
ONES_ROWS = 16


def _dsa_prompt_kernel(topk, qt_ref, iqt_ref, wt_ref, kb_ref, ikb_ref, vt_ref, o_ref, sc_ref, acc_ref, m_ref):
    i = pl.program_id(1)
    tq, kc = Q_TILE, K_TILE
    n_keys = sc_ref.shape[0] * kc
    nchunk = (i + 2) // 2
    qpos = i * tq + lax.broadcasted_iota(jnp.int32, (kc, tq), 1)
    krow = lax.broadcasted_iota(jnp.int32, (kc, tq), 0)
    qpos8 = qpos[0:SUBLANES]

    def col_reduce(x, op):
        return op(x.reshape(kc // SUBLANES, SUBLANES, tq), axis=0)

    def all_rows(x, op2):
        for shift in (4, 2, 1):
            x = op2(x, pltpu.roll(x, shift, 0))
        return x

    def tile_loop(first, rest, init, unroll):
        def trip(t, carry):
            heads = [first(t * unroll + sub) for sub in range(unroll)]
            for sub in range(unroll):
                carry = rest(t * unroll + sub, heads[sub], carry)
            return carry
        full = nchunk // unroll
        carry = lax.fori_loop(0, full, trip, init)
        return lax.fori_loop(full * unroll, nchunk, lambda c, carry: rest(c, first(c), carry), carry)

    def key_rows(c):
        return pl.ds(pl.multiple_of(c * kc, kc), kc)

    w = wt_ref[0]
    iqt = iqt_ref[0]
    s_scale = IDX_DIM ** -0.5 * IDX_HEADS ** -0.5

    def score_dots(c):
        return jnp.dot(ikb_ref[key_rows(c), :], iqt, preferred_element_type=jnp.float32)

    def score_tile(c, d, carry):
        s = w[0:1] * jnp.maximum(d[:, 0:tq], 0.0)
        for h in range(1, IDX_HEADS):
            s = s + w[h:h + 1] * jnp.maximum(d[:, h * tq:(h + 1) * tq], 0.0)
        sc_ref[c] = jnp.where(c * kc + krow <= qpos, s * s_scale, NEG_INF)
        return carry

    tile_loop(score_dots, score_tile, 0, K_UNROLL)

    def scan(body, init):
        return tile_loop(lambda c: sc_ref[c], body, init, SCAN_UNROLL)

    def count(pred):
        def body(c, s, acc):
            return acc + col_reduce(jnp.where(pred(s, c * kc + krow), 1.0, 0.0), jnp.sum)
        return all_rows(scan(body, jnp.zeros((SUBLANES, tq), jnp.float32)), jnp.add)

    def count_ge(c):
        return count(lambda s, kpos: s >= c[0:1])

    def count_tie(v, m):
        return count(lambda s, kpos: jnp.logical_and(s == v[0:1], kpos < m[0:1]))

    def minmax(c, s, mm):
        adm = c * kc + krow <= qpos
        return (jnp.minimum(mm[0], col_reduce(jnp.where(adm, s, jnp.inf), jnp.min)),
                jnp.maximum(mm[1], col_reduce(jnp.where(adm, s, -jnp.inf), jnp.max)))

    mn, mx = scan(minmax, (jnp.full((SUBLANES, tq), jnp.inf, jnp.float32),
                           jnp.full((SUBLANES, tq), -jnp.inf, jnp.float32)))
    thr, cut = _topk_threshold(count_ge, count_tie, all_rows(mn, jnp.minimum), all_rows(mx, jnp.maximum),
                               qpos8 + 1, topk, n_keys)
    thr_row, cut_row = thr[0:1], cut[0:1]

    m_ref[...] = jnp.full(m_ref.shape, 0.5 * NEG_INF, jnp.float32)
    acc_ref[...] = jnp.zeros(acc_ref.shape, jnp.float32)
    qt = qt_ref[0]
    kw = N_KV_A * HEAD_DIM_A

    def logits(c):
        return jnp.dot(kb_ref[key_rows(c), :], qt, preferred_element_type=jnp.float32)

    def attend_tile(c, lg, carry):
        s = sc_ref[c]
        kpos = c * kc + krow
        sel = jnp.logical_or(s > thr_row, jnp.logical_and(s == thr_row, kpos < cut_row))
        bias = jnp.where(jnp.logical_and(sel, kpos <= qpos), 0.0, NEG_INF)
        ps, alphas = [], []
        for h in range(N_HEADS_A):
            cols = slice(h * tq, (h + 1) * tq)
            lgh = lg[:, cols] + bias
            m_old = m_ref[:, cols]
            m_new = jnp.maximum(m_old, all_rows(col_reduce(lgh, jnp.max), jnp.maximum))
            alphas.append(jnp.exp2(m_old - m_new)[0:1])
            ps.append(jnp.exp2(lgh - m_new[0:1]).astype(MXU_DTYPE))
            m_ref[:, cols] = m_new
        pv = jnp.dot(vt_ref[c], jnp.concatenate(ps, axis=1), preferred_element_type=jnp.float32)
        acc_ref[...] = acc_ref[...] * jnp.concatenate(alphas, axis=1) + pv
        return carry

    tile_loop(logits, attend_tile, 0, K_UNROLL)

    acc = acc_ref[...]
    o_t = acc[0:kw] / acc[kw:kw + 1]
    for p in range(N_HEADS_A // 2):
        n = (2 * p) // GROUP_A
        pair = jnp.concatenate([o_t[n * HEAD_DIM_A:(n + 1) * HEAD_DIM_A, (2 * p + e) * tq:(2 * p + e + 1) * tq]
                                for e in range(2)], axis=0)
        o_ref[:, p * LANES:(p + 1) * LANES] = pair.T.astype(o_ref.dtype)


def _dsa_prompt(qt, iqt, wt, kb, ikb, vt, batch, seq):
    tq, kc = Q_TILE, K_TILE
    assert seq % kc == 0
    nq = seq // tq
    nk = seq // kc
    n = batch * seq
    kw = N_KV_A * HEAD_DIM_A
    topk = min(TOPK_MAX, seq // 4)
    return pl.pallas_call(
        functools.partial(_dsa_prompt_kernel, topk),
        grid=(batch, nq),
        in_specs=[
            pl.BlockSpec((1, LANES, N_HEADS_A * tq), lambda b, i: (b * nq + i, 0, 0)),
            pl.BlockSpec((1, IDX_DIM, IDX_HEADS * tq), lambda b, i: (b * nq + i, 0, 0)),
            pl.BlockSpec((1, IDX_HEADS, tq), lambda b, i: (b * nq + i, 0, 0)),
            pl.BlockSpec((seq, kw), lambda b, i: (b, 0)),
            pl.BlockSpec((seq, IDX_DIM), lambda b, i: (b, 0)),
            pl.BlockSpec((nk, kw + ONES_ROWS, kc), lambda b, i: (b, 0, 0)),
        ],
        out_specs=pl.BlockSpec((tq, BRANCH_WIDTH), lambda b, i: (b * nq + i, 0)),
        out_shape=jax.ShapeDtypeStruct((n, BRANCH_WIDTH), MXU_DTYPE),
        scratch_shapes=[
            pltpu.VMEM((nk, kc, tq), jnp.float32),
            pltpu.VMEM((kw + ONES_ROWS, N_HEADS_A * tq), jnp.float32),
            pltpu.VMEM((SUBLANES, N_HEADS_A * tq), jnp.float32),
        ],
        compiler_params=pltpu.CompilerParams(dimension_semantics=("arbitrary", "arbitrary"),
                                             vmem_limit_bytes=VMEM_LIMIT_BYTES),
        name="dsa_prompt",
    )(qt, iqt, wt, kb, ikb, vt)


def _page_copy(pt_ref, cache_ref, buf_ref, sem_ref, seq, page, slot):
    lanes = pl.ds(pl.multiple_of(page * PAGE_SIZE, PAGE_SIZE), PAGE_SIZE)
    return pltpu.make_async_copy(cache_ref.at[pt_ref[seq, page]], buf_ref.at[slot, :, lanes], sem_ref.at[slot])


def _pages_start(pt_ref, cache_ref, buf_ref, sem_ref, seq, slot, n_pages):
    def body(p, c):
        _page_copy(pt_ref, cache_ref, buf_ref, sem_ref, seq, p, slot).start()
        return c
    lax.fori_loop(0, n_pages, body, 0)


def _pages_wait(pt_ref, cache_ref, buf_ref, sem_ref, seq, slot, n_pages):
    def body(p, c):
        _page_copy(pt_ref, cache_ref, buf_ref, sem_ref, seq, p, slot).wait()
        return c
    lax.fori_loop(0, n_pages, body, 0)


def _dsa_sample_score_kernel(topk, n_pages, pt_ref, iq_ref, iw_ref, iknew_ref, cache_ik_ref,
                             sc_ref, thr_ref, cut_ref, ikbuf_ref, sem_ref):
    s = pl.program_id(0)
    n_seq = pl.num_programs(0)
    past = n_pages * PAGE_SIZE
    n_tiles = sc_ref.shape[0]
    slot = s % 2

    @pl.when(s == 0)
    def _():
        _pages_start(pt_ref, cache_ik_ref, ikbuf_ref, sem_ref, 0, 0, n_pages)

    @pl.when(s + 1 < n_seq)
    def _():
        _pages_start(pt_ref, cache_ik_ref, ikbuf_ref, sem_ref, s + 1, 1 - slot, n_pages)

    _pages_wait(pt_ref, cache_ik_ref, ikbuf_ref, sem_ref, s, slot, n_pages)

    iq = iq_ref[0]
    w = iw_ref[0]
    s_scale = IDX_DIM ** -0.5 * IDX_HEADS ** -0.5
    d = _mm(iq, ikbuf_ref[slot])
    srow = jnp.sum(w * jnp.maximum(d, 0.0), axis=0, keepdims=True) * s_scale
    for j in range(n_pages):
        sc_ref[j, pl.ds(s, 1), :] = srow[:, j * LANES:(j + 1) * LANES]
    ik_new = iknew_ref[0].astype(MXU_DTYPE).astype(jnp.float32)
    d_self = jnp.sum(iq.astype(jnp.float32) * ik_new, axis=1, keepdims=True)
    s_self = jnp.sum(w * jnp.maximum(d_self, 0.0), axis=0, keepdims=True) * s_scale
    lane1 = lax.broadcasted_iota(jnp.int32, (1, LANES), 1)
    sc_ref[n_tiles - 1, pl.ds(s, 1), :] = jnp.where(lane1 == 0, s_self, NEG_INF)

    @pl.when(s == n_seq - 1)
    def _():
        rows = sc_ref.shape[1]
        lane = lax.broadcasted_iota(jnp.int32, (rows, LANES), 1)

        def count(pred):
            def body(j, acc):
                return acc + jnp.where(pred(sc_ref[j], j * LANES + lane), 1.0, 0.0)
            acc = lax.fori_loop(0, n_tiles, body, jnp.zeros((rows, LANES), jnp.float32))
            return jnp.broadcast_to(jnp.sum(acc, axis=1, keepdims=True), (rows, LANES))

        def count_ge(c):
            return count(lambda t, kpos: t >= c)

        def count_tie(v, m):
            return count(lambda t, kpos: jnp.logical_and(t == v, kpos < m))

        def minmax(j, mm):
            t = sc_ref[j]
            adm = j * LANES + lane <= past
            return (jnp.minimum(mm[0], jnp.where(adm, t, jnp.inf)),
                    jnp.maximum(mm[1], jnp.where(adm, t, -jnp.inf)))

        mn, mx = lax.fori_loop(0, n_tiles, minmax, (jnp.full((rows, LANES), jnp.inf, jnp.float32),
                                                    jnp.full((rows, LANES), -jnp.inf, jnp.float32)))
        row_min = jnp.broadcast_to(jnp.min(mn, axis=1, keepdims=True), (rows, LANES))
        row_max = jnp.broadcast_to(jnp.max(mx, axis=1, keepdims=True), (rows, LANES))
        n_adm = jnp.full((rows, LANES), past + 1, jnp.int32)
        thr, cut = _topk_threshold(count_ge, count_tie, row_min, row_max, n_adm, topk, n_tiles * LANES)
        thr_ref[...] = thr
        cut_ref[...] = cut


def _dsa_sample_attend_kernel(n_pages, pt_ref, q_ref, sc_ref, thr_ref, cut_ref, kvnew_ref,
                              cache_k_ref, cache_v_ref, o_ref, kbuf_ref, vbuf_ref, ksem_ref, vsem_ref):
    s = pl.program_id(0)
    n_seq = pl.num_programs(0)
    past = n_pages * PAGE_SIZE
    slot = s % 2

    def start(seq, sl):
        _pages_start(pt_ref, cache_k_ref, kbuf_ref, ksem_ref, seq, sl, n_pages)
        _pages_start(pt_ref, cache_v_ref, vbuf_ref, vsem_ref, seq, sl, n_pages)

    @pl.when(s == 0)
    def _():
        start(0, 0)

    @pl.when(s + 1 < n_seq)
    def _():
        start(s + 1, 1 - slot)

    _pages_wait(pt_ref, cache_k_ref, kbuf_ref, ksem_ref, s, slot, n_pages)
    _pages_wait(pt_ref, cache_v_ref, vbuf_ref, vsem_ref, s, slot, n_pages)

    q = q_ref[0]
    thr = thr_ref[0][:, 0:1]
    cut = cut_ref[0][:, 0:1]
    kw = N_KV_A * HEAD_DIM_A
    k_new = kvnew_ref[0][:, 0:kw].astype(MXU_DTYPE).astype(jnp.float32)
    v_new = kvnew_ref[0][:, kw:2 * kw].astype(MXU_DTYPE).astype(jnp.float32)
    sc = sc_ref[0]

    def selected(srow, kpos):
        return jnp.logical_or(srow > thr, jnp.logical_and(srow == thr, kpos < cut))

    kpos = lax.broadcasted_iota(jnp.int32, (1, past), 1)
    bias = jnp.where(selected(sc[:, 0:past], kpos), 0.0, NEG_INF)
    lg = _mm(q, kbuf_ref[slot]) + bias
    lg_self = jnp.sum(q.astype(jnp.float32) * k_new, axis=1, keepdims=True)
    lg_self = jnp.where(selected(sc[:, past:past + 1], past), lg_self, NEG_INF)
    m = jnp.maximum(jnp.max(lg, axis=1, keepdims=True), lg_self)
    p = jnp.exp(lg - m)
    p_self = jnp.exp(lg_self - m)
    denom = jnp.sum(p, axis=1, keepdims=True) + p_self
    o = (_mm_nt(p, vbuf_ref[slot]) + p_self * v_new) / denom
    parts = []
    for h in range(N_HEADS_A):
        n = h // GROUP_A
        parts.append(o[h:h + 1, n * HEAD_DIM_A:(n + 1) * HEAD_DIM_A])
    o_ref[0] = jnp.concatenate(parts, axis=1)


def _dsa_sample(qexp, iqhm, small, ik_new, kv_new, cache_k, cache_v, cache_ik, page_table):
    db, n_pages = page_table.shape
    past = n_pages * PAGE_SIZE
    n_pool = cache_ik.shape[0]
    topk = min(TOPK_MAX, (past + 1) // 4)
    n_tiles = n_pages + 1
    kw = N_KV_A * HEAD_DIM_A
    q_s = jnp.swapaxes(qexp, 0, 1)
    iq_s = jnp.swapaxes(iqhm, 0, 1)
    iw_s = small[:, 0:IDX_HEADS].reshape(db, IDX_HEADS, 1)
    ck_t = jnp.transpose(cache_k, (0, 2, 3, 1)).reshape(n_pool, kw, PAGE_SIZE)
    cv_t = jnp.transpose(cache_v, (0, 2, 3, 1)).reshape(n_pool, kw, PAGE_SIZE)
    cik_t = jnp.swapaxes(cache_ik, 1, 2)
    cparams = pltpu.CompilerParams(dimension_semantics=("arbitrary",), vmem_limit_bytes=VMEM_LIMIT_BYTES)
    per_seq = lambda *shape: pl.BlockSpec((1,) + shape, lambda s, pt: (s,) + (0,) * len(shape))
    whole = lambda *shape: pl.BlockSpec(shape, lambda s, pt: (0,) * len(shape))
    any_spec = pl.BlockSpec(memory_space=pl.ANY)

    sc, thr, cut = pl.pallas_call(
        functools.partial(_dsa_sample_score_kernel, topk, n_pages),
        grid_spec=pltpu.PrefetchScalarGridSpec(
            num_scalar_prefetch=1,
            grid=(db,),
            in_specs=[per_seq(IDX_HEADS, IDX_DIM), per_seq(IDX_HEADS, 1), per_seq(1, IDX_DIM), any_spec],
            out_specs=(whole(n_tiles, db, LANES), whole(db, LANES), whole(db, LANES)),
            scratch_shapes=[pltpu.VMEM((2, IDX_DIM, past), jnp.float32), pltpu.SemaphoreType.DMA((2,))],
        ),
        out_shape=(jax.ShapeDtypeStruct((n_tiles, db, LANES), jnp.float32),
                   jax.ShapeDtypeStruct((db, LANES), jnp.float32),
                   jax.ShapeDtypeStruct((db, LANES), jnp.int32)),
        compiler_params=cparams,
        name="dsa_sample_score",
    )(page_table, iq_s, iw_s, ik_new.reshape(db, 1, IDX_DIM), cik_t)

    o = pl.pallas_call(
        functools.partial(_dsa_sample_attend_kernel, n_pages),
        grid_spec=pltpu.PrefetchScalarGridSpec(
            num_scalar_prefetch=1,
            grid=(db,),
            in_specs=[per_seq(N_HEADS_A, LANES), per_seq(1, n_tiles * LANES), per_seq(1, LANES), per_seq(1, LANES),
                      per_seq(1, 2 * kw), any_spec, any_spec],
            out_specs=per_seq(1, BRANCH_WIDTH),
            scratch_shapes=[pltpu.VMEM((2, kw, past), jnp.float32), pltpu.VMEM((2, kw, past), jnp.float32),
                            pltpu.SemaphoreType.DMA((2,)), pltpu.SemaphoreType.DMA((2,))],
        ),
        out_shape=jax.ShapeDtypeStruct((db, 1, BRANCH_WIDTH), jnp.float32),
        compiler_params=cparams,
        name="dsa_sample_attend",
    )(page_table, q_s, jnp.swapaxes(sc, 0, 1).reshape(db, 1, n_tiles * LANES),
      thr.reshape(db, 1, LANES), cut.reshape(db, 1, LANES),
      kv_new.reshape(db, 1, 2 * kw), ck_t, cv_t)
    return o.reshape(db, BRANCH_WIDTH)


GDN_CHUNK = 128
A_LANE = IDX_HEADS
B_LANE = IDX_HEADS + H_B


def _split2(x):
    h = x.astype(MXU_DTYPE)
    return h, (x - h.astype(jnp.float32)).astype(MXU_DTYPE)


def _mm2(a, b):
    a1, a2 = _split2(a)
    b1, b2 = _split2(b)
    d = functools.partial(jnp.dot, preferred_element_type=jnp.float32)
    return d(a1, b1) + (d(a1, b2) + d(a2, b1))


def _unit_lower_inverses(mats):
    n = mats[0].shape[0]
    eye = (lax.broadcasted_iota(jnp.int32, (n, n), 0) == lax.broadcasted_iota(jnp.int32, (n, n), 1))
    ss = [jnp.where(eye, 1.0, 0.0) - a for a in mats]
    ps = [_mm2(a, a) for a in mats]
    k = 2
    while k < n:
        ss = [s + _mm2(s, p) for s, p in zip(ss, ps)]
        k *= 2
        if k < n:
            ps = [_mm2(p, p) for p in ps]
    return ss


def _l2norm(x):
    return x * lax.rsqrt(jnp.sum(x * x, axis=-1, keepdims=True) + NORM_EPS)


def _gdn_prompt_kernel(u_ref, small_ref, z_ref, conv0_ref, s0_ref, wconv_ref, alog_ref, dtb_ref, dnorm_ref,
                       o_ref, conv_out_ref, s_out_ref, ucat_ref, state_ref):
    n = pl.program_id(1)
    c = GDN_CHUNK
    head = SUBLANES
    tail = CONV_K - 1

    @pl.when(n == 0)
    def _():
        ucat_ref[head - tail:head, :] = conv0_ref[0]
        state_ref[...] = s0_ref[0]

    ucat_ref[head:head + c, :] = u_ref[...]
    y = wconv_ref[tail:tail + 1, :] * ucat_ref[head:head + c, :]
    for j in range(tail):
        y = y + wconv_ref[j:j + 1, :] * ucat_ref[head - tail + j:head - tail + j + c, :]
    cv = _silu(y)
    carry_rows = ucat_ref[head + c - tail:head + c, :]
    ucat_ref[head - tail:head, :] = carry_rows
    conv_out_ref[0] = carry_rows

    sm = small_ref[...]
    g_all = -jnp.exp(alog_ref[...]) * _softplus(sm + dtb_ref[...])
    beta_all = _sigmoid(sm)
    row = lax.broadcasted_iota(jnp.int32, (c, c), 0)
    col = lax.broadcasted_iota(jnp.int32, (c, c), 1)
    lower = row >= col
    strict = row > col
    a1, a2, a3 = _split3(g_all)
    ltri = jnp.where(lower, 1.0, 0.0).astype(MXU_DTYPE)
    d = functools.partial(jnp.dot, preferred_element_type=jnp.float32)
    gc_all = d(ltri, a1) + (d(ltri, a2) + d(ltri, a3))

    qw = H_B * DK_B
    heads = range(H_B)
    qs = [_l2norm(cv[:, h * DK_B:(h + 1) * DK_B]) * (DK_B ** -0.5) for h in heads]
    ks = [_l2norm(cv[:, qw + h * DK_B:qw + (h + 1) * DK_B]) for h in heads]
    vs = [cv[:, 2 * qw + h * DV_B:2 * qw + (h + 1) * DV_B] for h in heads]
    betas = [jnp.broadcast_to(beta_all[:, B_LANE + h:B_LANE + h + 1], (c, LANES)) for h in heads]
    gcols = [jnp.broadcast_to(gc_all[:, A_LANE + h:A_LANE + h + 1], (c, c)) for h in heads]
    decays = [jnp.where(lower, jnp.exp(jnp.where(lower, g - g.T, 0.0)), 0.0) for g in gcols]
    egs = [jnp.exp(g) for g in gcols]
    g_lasts = [g[c - 1:c, :] for g in gcols]
    kbs = [k * b for k, b in zip(ks, betas)]
    vbs = [v * b for v, b in zip(vs, betas)]
    kks = [_mm_nt(kb, k) for kb, k in zip(kbs, ks)]
    qks = [_mm_nt(q, k) for q, k in zip(qs, ks)]
    t_invs = _unit_lower_inverses([jnp.where(strict, kk * d, 0.0) for kk, d in zip(kks, decays)])
    sols = [_mm2(t, jnp.concatenate([vb, kb * eg], axis=1)) for t, vb, kb, eg in zip(t_invs, vbs, kbs, egs)]
    s_olds = [state_ref[h] for h in heads]
    v_news = [sol[:, 0:DV_B] - _mm(sol[:, DV_B:DV_B + DK_B], s) for sol, s in zip(sols, s_olds)]
    o_hs = [_mm(q * eg, s) + _mm(qk * d, v_new)
            for q, eg, s, qk, d, v_new in zip(qs, egs, s_olds, qks, decays, v_news)]
    for h in heads:
        k_dec = ks[h] * jnp.exp(g_lasts[h] - gcols[h])
        state_ref[h] = s_olds[h] * jnp.exp(g_lasts[h]) + _mm_tn(k_dec, v_news[h])
    for h in heads:
        o_h = o_hs[h]
        ms = jnp.mean(o_h * o_h, axis=-1, keepdims=True)
        o_n = o_h * lax.rsqrt(ms + NORM_EPS) * dnorm_ref[...]
        o_ref[:, h * DV_B:(h + 1) * DV_B] = (o_n * _silu(z_ref[:, h * DV_B:(h + 1) * DV_B])).astype(o_ref.dtype)

    @pl.when(n == pl.num_programs(1) - 1)
    def _():
        s_out_ref[0] = state_ref[...]


GDN_SEQ_TILE = 8


def _gdn_sample_kernel(u_ref, cb_ref, small_ref, z_ref, s0_ref, wconv_ref, alog_ref, dtb_ref, dnorm_ref,
                       o_ref, conv_out_ref, s_out_ref):
    ts = GDN_SEQ_TILE
    tail = CONV_K - 1
    u_new = u_ref[...]
    y = wconv_ref[tail:tail + 1, :] * u_new
    for j in range(tail):
        y = y + wconv_ref[j:j + 1, :] * cb_ref[j]
    cv = _silu(y)
    for j in range(tail - 1):
        conv_out_ref[j] = cb_ref[j + 1]
    conv_out_ref[tail - 1] = u_new

    sm = small_ref[...]
    eg_all = jnp.exp(-jnp.exp(alog_ref[...]) * _softplus(sm + dtb_ref[...]))
    beta_all = _sigmoid(sm)
    qw = H_B * DK_B
    for h in range(H_B):
        q = _l2norm(cv[:, h * DK_B:(h + 1) * DK_B]) * (DK_B ** -0.5)
        k = _l2norm(cv[:, qw + h * DK_B:qw + (h + 1) * DK_B])
        v = cv[:, 2 * qw + h * DV_B:2 * qw + (h + 1) * DV_B]
        eg = eg_all[:, A_LANE + h:A_LANE + h + 1]
        beta = beta_all[:, B_LANE + h:B_LANE + h + 1]
        qk = jnp.sum(q * k, axis=-1, keepdims=True)
        k_t, q_t = k.T, q.T
        rows = []
        for r in range(ts):
            s_old = s0_ref[r, h]
            kc = k_t[:, r:r + 1]
            ks = jnp.sum(s_old * kc, axis=0, keepdims=True)
            qs = jnp.sum(s_old * q_t[:, r:r + 1], axis=0, keepdims=True)
            eg_r = eg[r:r + 1, :]
            v_new = beta[r:r + 1, :] * (v[r:r + 1, :] - eg_r * ks)
            rows.append(eg_r * qs + qk[r:r + 1, :] * v_new)
            s_out_ref[r, h] = s_old * eg_r + kc * v_new
        o_h = jnp.concatenate(rows, axis=0)
        ms = jnp.mean(o_h * o_h, axis=-1, keepdims=True)
        o_n = o_h * lax.rsqrt(ms + NORM_EPS) * dnorm_ref[...]
        o_ref[:, h * DV_B:(h + 1) * DV_B] = (o_n * _silu(z_ref[:, h * DV_B:(h + 1) * DV_B])).astype(o_ref.dtype)


def _gdn_sample(u, small, z, conv_buf, s0, w_conv, a_log, dt_bias, delta_norm):
    db = u.shape[0]
    ts = GDN_SEQ_TILE
    assert db % ts == 0
    tail = CONV_K - 1
    alog_row, dtb_row = _gate_rows(a_log, dt_bias)
    row = lambda w: pl.BlockSpec((ts, w), lambda i: (i, 0))
    full = lambda *shape: pl.BlockSpec(shape, lambda i: (0,) * len(shape))
    cb_spec = pl.BlockSpec((tail, ts, CONV_DIM), lambda i: (0, i, 0))
    st_spec = pl.BlockSpec((ts, H_B, DK_B, DV_B), lambda i: (i, 0, 0, 0))
    o, conv_t, s_new = pl.pallas_call(
        _gdn_sample_kernel,
        grid=(db // ts,),
        in_specs=[row(CONV_DIM), cb_spec, row(SMALL_W), row(H_B * DV_B), st_spec,
                  full(CONV_K, CONV_DIM), full(1, SMALL_W), full(1, SMALL_W), full(1, DV_B)],
        out_specs=(row(H_B * DV_B), cb_spec, st_spec),
        out_shape=(jax.ShapeDtypeStruct((db, H_B * DV_B), jnp.float32),
                   jax.ShapeDtypeStruct((tail, db, CONV_DIM), jnp.float32),
                   jax.ShapeDtypeStruct((db, H_B, DK_B, DV_B), jnp.float32)),
        compiler_params=pltpu.CompilerParams(dimension_semantics=("arbitrary",),
                                             vmem_limit_bytes=VMEM_LIMIT_BYTES),
        name="gdn_sample",
    )(u, jnp.swapaxes(conv_buf, 0, 1), small, z, s0, w_conv.astype(jnp.float32), alog_row, dtb_row,
      delta_norm.astype(jnp.float32)[None, :])
    return o, jnp.swapaxes(conv_t, 0, 1), s_new


def _gate_rows(a_log, dt_bias):
    alog_row = jnp.zeros((1, SMALL_W), jnp.float32).at[0, A_LANE:A_LANE + H_B].set(a_log.astype(jnp.float32))
    dtb_row = jnp.zeros((1, SMALL_W), jnp.float32).at[0, A_LANE:A_LANE + H_B].set(dt_bias.astype(jnp.float32))
    return alog_row, dtb_row


def _gdn_prompt(u, small, z, conv0, s0, w_conv, a_log, dt_bias, delta_norm, batch, seq):
    c = GDN_CHUNK
    assert seq % c == 0
    nc = seq // c
    n = batch * seq
    alog_row, dtb_row = _gate_rows(a_log, dt_bias)
    row = lambda w: pl.BlockSpec((c, w), lambda b, i: (b * nc + i, 0))
    per_b = lambda *shape: pl.BlockSpec((1,) + shape, lambda b, i: (b,) + (0,) * len(shape))
    full = lambda *shape: pl.BlockSpec(shape, lambda b, i: (0,) * len(shape))
    return pl.pallas_call(
        _gdn_prompt_kernel,
        grid=(batch, nc),
        in_specs=[row(CONV_DIM), row(SMALL_W), row(H_B * DV_B), per_b(CONV_K - 1, CONV_DIM),
                  per_b(H_B, DK_B, DV_B), full(CONV_K, CONV_DIM), full(1, SMALL_W), full(1, SMALL_W),
                  full(1, DV_B)],
        out_specs=(row(H_B * DV_B), per_b(CONV_K - 1, CONV_DIM), per_b(H_B, DK_B, DV_B)),
        out_shape=(jax.ShapeDtypeStruct((n, H_B * DV_B), MXU_DTYPE),
                   jax.ShapeDtypeStruct((batch, CONV_K - 1, CONV_DIM), jnp.float32),
                   jax.ShapeDtypeStruct((batch, H_B, DK_B, DV_B), jnp.float32)),
        scratch_shapes=[pltpu.VMEM((SUBLANES + c, CONV_DIM), jnp.float32),
                        pltpu.VMEM((H_B, DK_B, DV_B), jnp.float32)],
        compiler_params=pltpu.CompilerParams(dimension_semantics=("arbitrary", "arbitrary"),
                                             vmem_limit_bytes=VMEM_LIMIT_BYTES),
        name="gdn_prompt",
    )(u, small, z, conv0, s0, w_conv.astype(jnp.float32), alog_row, dtb_row,
      delta_norm.astype(jnp.float32)[None, :])


def _merge_kernel(x_ref, oa_ref, ob_ref, gl_ref, wba_ref, wbb_ref, wout_ref, gain_ref, x1_ref, hn_ref):
    pa = _mm(oa_ref[...], wba_ref[...])
    pb = _mm(ob_ref[...], wbb_ref[...])
    mix = _sigmoid(gl_ref[:, 0:D_MODEL]) * pa + _sigmoid(gl_ref[:, D_MODEL:2 * D_MODEL]) * pb
    x1 = x_ref[...] + _mm(mix, wout_ref[...])
    x1_ref[...] = x1
    ms = jnp.mean(x1 * x1, axis=-1, keepdims=True)
    hn_ref[...] = (x1 * lax.rsqrt(ms + NORM_EPS) * gain_ref[...]).astype(hn_ref.dtype)


def _merge(x2d, o_a, o_b, gl, w_branch, w_out, norm_ffn, tm):
    n = x2d.shape[0]
    assert n % tm == 0
    row = lambda w: pl.BlockSpec((tm, w), lambda i: (i, 0))
    full = lambda *shape: pl.BlockSpec(shape, lambda i: (0,) * len(shape))
    return pl.pallas_call(
        _merge_kernel,
        grid=(n // tm,),
        in_specs=[row(D_MODEL), row(BRANCH_WIDTH), row(BRANCH_WIDTH), row(2 * D_MODEL),
                  full(BRANCH_WIDTH, D_MODEL), full(BRANCH_WIDTH, D_MODEL), full(D_MODEL, D_MODEL),
                  full(1, D_MODEL)],
        out_specs=(row(D_MODEL), row(D_MODEL)),
        out_shape=(jax.ShapeDtypeStruct((n, D_MODEL), jnp.float32),
                   jax.ShapeDtypeStruct((n, D_MODEL), MXU_DTYPE)),
        compiler_params=pltpu.CompilerParams(dimension_semantics=("arbitrary",),
                                             vmem_limit_bytes=VMEM_LIMIT_BYTES),
        name="merge",
    )(x2d, o_a, o_b, gl, w_branch[0].astype(MXU_DTYPE), w_branch[1].astype(MXU_DTYPE),
      w_out.astype(MXU_DTYPE), norm_ffn.astype(jnp.float32)[None, :])


FFN_TILE = D_FF // 2


def _ffn_kernel(hn_ref, x1_ref, wg_ref, wu_ref, wd_ref, y_ref, acc_ref):
    j = pl.program_id(1)

    @pl.when(j == 0)
    def _():
        acc_ref[...] = x1_ref[...]

    hn = hn_ref[...]
    g = jnp.dot(hn, wg_ref[...], preferred_element_type=jnp.float32)
    u = jnp.dot(hn, wu_ref[...], preferred_element_type=jnp.float32)
    acc_ref[...] += _mm(_silu(g) * u, wd_ref[...])

    @pl.when(j == pl.num_programs(1) - 1)
    def _():
        y_ref[...] = acc_ref[...]


def _ffn(hn, x1, w_gate_up, w_down, tm):
    n = hn.shape[0]
    tf = FFN_TILE
    assert n % tm == 0 and D_FF % tf == 0 and tf % LANES == 0
    nf = D_FF // tf
    wgu = w_gate_up.astype(MXU_DTYPE)
    return pl.pallas_call(
        _ffn_kernel,
        grid=(n // tm, nf),
        in_specs=[pl.BlockSpec((tm, D_MODEL), lambda i, j: (i, 0)),
                  pl.BlockSpec((tm, D_MODEL), lambda i, j: (i, 0)),
                  pl.BlockSpec((D_MODEL, tf), lambda i, j: (0, j)),
                  pl.BlockSpec((D_MODEL, tf), lambda i, j: (0, j + nf)),
                  pl.BlockSpec((tf, D_MODEL), lambda i, j: (j, 0))],
        out_specs=pl.BlockSpec((tm, D_MODEL), lambda i, j: (i, 0)),
        out_shape=jax.ShapeDtypeStruct((n, D_MODEL), jnp.float32),
        scratch_shapes=[pltpu.VMEM((tm, D_MODEL), jnp.float32)],
        compiler_params=pltpu.CompilerParams(dimension_semantics=("arbitrary", "arbitrary"),
                                             vmem_limit_bytes=VMEM_LIMIT_BYTES),
        name="ffn",
    )(hn, x1, wgu, wgu, w_down.astype(MXU_DTYPE))


IN_PROJ_TILE = K_TILE
MERGE_TILE = 512
FFN_ROW_TILE = 512


def _layer(x_p, x_s, cache_k, cache_v, cache_ik, conv_s, delta_s, page_table, norm_mix, w_in, q_norm, k_norm,
           w_conv, a_log, dt_bias, delta_norm, w_branch, w_out, norm_ffn, w_gate_up, w_down):
    b, t, d = x_p.shape
    db = x_s.shape[0]
    past = page_table.shape[1] * PAGE_SIZE
    kw = N_KV_A * HEAD_DIM_A
    w_packed = _pack_w_in(w_in)

    xp2 = x_p.reshape(b * t, d)
    tm = IN_PROJ_TILE
    assert t % tm == 0
    kv, ik, small, u, z, gl, qt, iqt, wt, kb, ikb, vt = _in_proj(
        xp2, _rope_tables(jnp.arange(t)), t // tm, tm, norm_mix, w_packed, q_norm, k_norm, key_major=True)
    o_a = _dsa_prompt(qt, iqt, wt, kb, ikb, vt, b, t)
    conv0 = jnp.zeros((b, CONV_K - 1, CONV_DIM), jnp.float32)
    delta0 = jnp.zeros((b, H_B, DK_B, DV_B), jnp.float32)
    o_b, conv_p, delta_p = _gdn_prompt(u, small, z, conv0, delta0, w_conv, a_log, dt_bias, delta_norm, b, t)
    x1, hn = _merge(xp2, o_a, o_b, gl, w_branch, w_out, norm_ffn, min(MERGE_TILE, b * t))
    y_p = _ffn(hn, x1, w_gate_up, w_down, min(FFN_ROW_TILE, b * t)).reshape(b, t, d)
    st_p = (kv[:, 0:kw].reshape(b, t, N_KV_A, HEAD_DIM_A), kv[:, kw:2 * kw].reshape(b, t, N_KV_A, HEAD_DIM_A),
            ik.reshape(b, t, IDX_DIM), conv_p, delta_p)

    xs2 = x_s.reshape(db, d)
    kv, ik, small, u, z, gl, qexp, iqhm = _in_proj(
        xs2, _rope_tables(jnp.full((db,), past, jnp.int32)), 1, db, norm_mix, w_packed, q_norm, k_norm,
        key_major=False)
    o_a = _dsa_sample(qexp, iqhm, small, ik, kv, cache_k, cache_v, cache_ik, page_table)
    o_b, conv_n, delta_n = _gdn_sample(u, small, z, conv_s, delta_s, w_conv, a_log, dt_bias, delta_norm)
    x1, hn = _merge(xs2, o_a, o_b, gl, w_branch, w_out, norm_ffn, db)
    y_s = _ffn(hn, x1, w_gate_up, w_down, db).reshape(db, 1, d)
    st_s = (kv[:, 0:kw].reshape(db, 1, N_KV_A, HEAD_DIM_A), kv[:, kw:2 * kw].reshape(db, 1, N_KV_A, HEAD_DIM_A),
            ik.reshape(db, 1, IDX_DIM), conv_n, delta_n)
    return y_p, y_s, st_p, st_s


def kernel(x_prompt, x_sample, cache_k, cache_v, cache_idx_k, state_conv, state_delta, page_table,
           norm_mix, w_in, q_norm, k_norm, w_conv, a_log, dt_bias, delta_norm, w_branch, w_out,
           norm_ffn, w_gate_up, w_down):
    assert x_sample.shape[1] == 1, "the sample group decodes one token per sequence"
    y_p, y_s = x_prompt, x_sample
    new_p, new_s = [], []
    for l in range(w_in.shape[0]):
        y_p, y_s, st_p, st_s = _layer(
            y_p, y_s, cache_k[l], cache_v[l], cache_idx_k[l], state_conv[l], state_delta[l], page_table,
            norm_mix[l], w_in[l], q_norm[l], k_norm[l], w_conv[l], a_log[l], dt_bias[l], delta_norm[l],
            w_branch[l], w_out[l], norm_ffn[l], w_gate_up[l], w_down[l])
        new_p.append(st_p)
        new_s.append(st_s)
    k_p, v_p, ik_p, conv_p, delta_p = [jnp.stack(a) for a in zip(*new_p)]
    k_s, v_s, ik_s, conv_s, delta_s = [jnp.stack(a) for a in zip(*new_s)]
    return (y_p, y_s, k_p, v_p, ik_p, conv_p, delta_p, k_s, v_s, ik_s, conv_s, delta_s)
```

```python
import functools
import math

import jax
import jax.numpy as jnp
import numpy as np
from jax import lax
from jax.experimental import pallas as pl
from jax.experimental.pallas import tpu as pltpu

D_MODEL = 1024
PAGE_SIZE = 128
N_HEADS_A = 8
N_KV_A = 2
HEAD_DIM_A = 64
GROUP_A = N_HEADS_A // N_KV_A
IDX_HEADS = 8
IDX_DIM = 64
TOPK_MAX = 256
ROPE_THETA = 500000.0
H_B = 4
DK_B = 128
DV_B = 128
CONV_K = 4
CONV_DIM = 2 * H_B * DK_B + H_B * DV_B
BRANCH_WIDTH = N_HEADS_A * HEAD_DIM_A
D_FF = -(-8 * D_MODEL // (3 * 256)) * 256
NORM_EPS = 1e-6
NEG_INF = -1e30
IN_SIZES = (N_HEADS_A * HEAD_DIM_A, N_KV_A * HEAD_DIM_A, N_KV_A * HEAD_DIM_A,
            IDX_HEADS * IDX_DIM, IDX_DIM, IDX_HEADS,
            CONV_DIM, H_B, H_B, H_B * DV_B, 2 * D_MODEL)

LANES = 128
SUBLANES = 8
VMEM_LIMIT_BYTES = 56 * 1024 * 1024

MXU_DTYPE = jnp.bfloat16

SMALL_W = LANES
SEG_A = BRANCH_WIDTH + 2 * N_KV_A * HEAD_DIM_A + IDX_HEADS * IDX_DIM + IDX_DIM
SEG_A_PAD = -(-SEG_A // LANES) * LANES
OFF_SMALL = SEG_A_PAD
OFF_U = OFF_SMALL + SMALL_W
OFF_Z = OFF_U + CONV_DIM
OFF_GL = OFF_Z + H_B * DV_B
D_IN_PACKED = OFF_GL + 2 * D_MODEL


def _mm(a, b):
    return jnp.dot(a.astype(MXU_DTYPE), b.astype(MXU_DTYPE), preferred_element_type=jnp.float32)


def _mm_nt(a, b):
    return lax.dot_general(a.astype(MXU_DTYPE), b.astype(MXU_DTYPE), (((1,), (1,)), ((), ())),
                           preferred_element_type=jnp.float32)


def _mm_tn(a, b):
    return lax.dot_general(a.astype(MXU_DTYPE), b.astype(MXU_DTYPE), (((0,), (0,)), ((), ())),
                           preferred_element_type=jnp.float32)


def _split3(x):
    x = x.astype(jnp.float32)
    h = x.astype(MXU_DTYPE)
    r = x - h.astype(jnp.float32)
    m = r.astype(MXU_DTYPE)
    l = (r - m.astype(jnp.float32)).astype(MXU_DTYPE)
    return h, m, l


def _sigmoid(x):
    return 1.0 / (1.0 + jnp.exp(-x))


def _silu(x):
    return x * _sigmoid(x)


def _softplus(x):
    return jnp.maximum(x, 0.0) + jnp.log(1.0 + jnp.exp(-jnp.abs(x)))


def _rope_tile(x, cos_t, sin_lo, sin_hi):
    half = HEAD_DIM_A // 8
    up = pltpu.roll(x, LANES - half, 1)
    dn = pltpu.roll(x, half, 1)
    return x * cos_t + up * sin_lo + dn * sin_hi


def _in_proj_kernel(key_major, x_ref, gain_ref, w_ref, bd_ref, qg_ref, kg_ref, cos_ref, slo_ref, shi_ref,
                    kv_ref, ik_ref, small_ref, u_ref, z_ref, gl_ref, *attn_refs):
    x = x_ref[...]
    ms = jnp.mean(x * x, axis=-1, keepdims=True)
    xn = (x * lax.rsqrt(ms + NORM_EPS) * gain_ref[...]).astype(MXU_DTYPE)

    cos_t, sin_lo, sin_hi = cos_ref[...], slo_ref[...], shi_ref[...]
    lane = lax.broadcasted_iota(jnp.int32, (x.shape[0], LANES), 1)
    lo_half = lane < HEAD_DIM_A

    def head_rms(t, gain):
        tt = t * t
        hi = tt.astype(MXU_DTYPE)
        lo = (tt - hi.astype(jnp.float32)).astype(MXU_DTYPE)
        bd = bd_ref[0:t.shape[1], 0:t.shape[1]]
        msq = (jnp.dot(hi, bd, preferred_element_type=jnp.float32)
               + jnp.dot(lo, bd, preferred_element_type=jnp.float32))
        return t * lax.rsqrt(msq + NORM_EPS) * gain

    q = jnp.dot(xn, w_ref[:, 0:BRANCH_WIDTH], preferred_element_type=jnp.float32)
    q = head_rms(q, qg_ref[...])
    if key_major:
        qt_ref, iqt_ref, wt_ref, kb_ref, ikb_ref, vt_ref = attn_refs
        n_qb = x.shape[0] // Q_TILE
        q_scale = HEAD_DIM_A ** -0.5 * math.log2(math.e)
    else:
        qexp_ref, iqhm_ref = attn_refs
        q_scale = HEAD_DIM_A ** -0.5
    for p in range(BRANCH_WIDTH // LANES):
        t = _rope_tile(q[:, p * LANES:(p + 1) * LANES], cos_t, sin_lo, sin_hi) * q_scale
        t_sw = pltpu.roll(t, HEAD_DIM_A, 1)
        for e in range(2):
            h = 2 * p + e
            n = h // GROUP_A
            src = t if e == n else t_sw
            keep = lo_half if n == 0 else jnp.logical_not(lo_half)
            qe = jnp.where(keep, src, 0.0)
            if key_major:
                qe_t = qe.T
                for j in range(n_qb):
                    qt_ref[j, :, h * Q_TILE:(h + 1) * Q_TILE] = qe_t[:, j * Q_TILE:(j + 1) * Q_TILE].astype(qt_ref.dtype)
            else:
                qexp_ref[h] = qe.astype(qexp_ref.dtype)

    c0 = BRANCH_WIDTH
    kw = N_KV_A * HEAD_DIM_A
    k = jnp.dot(xn, w_ref[:, c0:c0 + kw], preferred_element_type=jnp.float32)
    k = _rope_tile(head_rms(k, kg_ref[...]), cos_t, sin_lo, sin_hi)
    v = jnp.dot(xn, w_ref[:, c0 + kw:c0 + 2 * kw], preferred_element_type=jnp.float32)
    kv_ref[:, 0:kw] = k
    kv_ref[:, kw:2 * kw] = v
    if key_major:
        kb_ref[...] = k.astype(kb_ref.dtype)
        vt_ref[0, 0:kw, :] = v.T.astype(vt_ref.dtype)
        vt_ref[0, kw:kw + ONES_ROWS, :] = jnp.ones((ONES_ROWS, x.shape[0]), vt_ref.dtype)

    c1 = c0 + 2 * kw
    iqw = IDX_HEADS * IDX_DIM
    iq = jnp.dot(xn, w_ref[:, c1:c1 + iqw], preferred_element_type=jnp.float32)
    for p in range(iqw // LANES):
        t = _rope_tile(iq[:, p * LANES:(p + 1) * LANES], cos_t, sin_lo, sin_hi)
        if key_major:
            t_t = t.T
            for e in range(2):
                h = 2 * p + e
                for j in range(n_qb):
                    iqt_ref[j, :, h * Q_TILE:(h + 1) * Q_TILE] = (
                        t_t[e * IDX_DIM:(e + 1) * IDX_DIM, j * Q_TILE:(j + 1) * Q_TILE].astype(iqt_ref.dtype))
        else:
            t = t.astype(iqhm_ref.dtype)
            iqhm_ref[2 * p] = t[:, 0:IDX_DIM]
            iqhm_ref[2 * p + 1] = t[:, IDX_DIM:2 * IDX_DIM]

    c2 = c1 + iqw
    ik_sm = jnp.dot(xn, w_ref[:, c2:c2 + 2 * LANES], preferred_element_type=jnp.float32)
    ik = _rope_tile(ik_sm[:, 0:LANES], cos_t, sin_lo, sin_hi)[:, 0:IDX_DIM]
    ik_ref[...] = ik
    small = ik_sm[:, LANES:2 * LANES]
    small_ref[...] = small
    if key_major:
        ikb_ref[...] = ik.astype(ikb_ref.dtype)
        small_t = small.T
        for j in range(n_qb):
            wt_ref[j] = small_t[0:IDX_HEADS, j * Q_TILE:(j + 1) * Q_TILE]

    u_ref[...] = jnp.dot(xn, w_ref[:, OFF_U:OFF_U + CONV_DIM], preferred_element_type=jnp.float32)
    z_ref[...] = jnp.dot(xn, w_ref[:, OFF_Z:OFF_Z + H_B * DV_B], preferred_element_type=jnp.float32)
    gl_ref[...] = jnp.dot(xn, w_ref[:, OFF_GL:OFF_GL + 2 * D_MODEL], preferred_element_type=jnp.float32)


def _pack_w_in(w_in):
    pts = np.cumsum(IN_SIZES)[:-1].tolist()
    q, k, v, iq, ik, iw, u, a, b, z, gl = jnp.split(w_in, pts, axis=-1)
    d = w_in.shape[0]
    seg_a = jnp.concatenate([q, k, v, iq, ik, jnp.zeros((d, SEG_A_PAD - SEG_A), w_in.dtype)], axis=1)
    small = jnp.concatenate([iw, a, b, jnp.zeros((d, SMALL_W - IDX_HEADS - 2 * H_B), w_in.dtype)], axis=1)
    return jnp.concatenate([seg_a, small, u, z, gl], axis=1).astype(MXU_DTYPE)


def _rope_tables(pos):
    rot = HEAD_DIM_A // 4
    half = rot // 2
    inv_freq = ROPE_THETA ** (-jnp.arange(half, dtype=jnp.float32) / half)
    ang = pos.astype(jnp.float32)[:, None] * inv_freq[None, :]
    cos, sin = jnp.cos(ang), jnp.sin(ang)
    rows = pos.shape[0]
    one = jnp.ones((rows, HEAD_DIM_A - rot), jnp.float32)
    zero = jnp.zeros((rows, HEAD_DIM_A - rot), jnp.float32)
    zh = jnp.zeros((rows, half), jnp.float32)
    cos_h = jnp.concatenate([cos, cos, one], axis=1)
    slo_h = jnp.concatenate([-sin, zh, zero], axis=1)
    shi_h = jnp.concatenate([zh, sin, zero], axis=1)
    rep = LANES // HEAD_DIM_A
    return jnp.tile(cos_h, (1, rep)), jnp.tile(slo_h, (1, rep)), jnp.tile(shi_h, (1, rep))


def _in_proj(x2d, pos_tables, n_table_blocks, tm, norm_mix, w_packed, q_norm, k_norm, key_major):
    n = x2d.shape[0]
    assert n % tm == 0 and (not key_major or tm == K_TILE)
    cos_t, sin_lo, sin_hi = pos_tables
    bd = jnp.kron(jnp.eye(BRANCH_WIDTH // HEAD_DIM_A, dtype=jnp.float32),
                  jnp.full((HEAD_DIM_A, HEAD_DIM_A), 1.0 / HEAD_DIM_A, jnp.float32)).astype(MXU_DTYPE)
    qg = jnp.tile(q_norm.astype(jnp.float32), BRANCH_WIDTH // HEAD_DIM_A)[None, :]
    kg = jnp.tile(k_norm.astype(jnp.float32), N_KV_A)[None, :]
    kw = N_KV_A * HEAD_DIM_A
    row = lambda w: pl.BlockSpec((tm, w), lambda i: (i, 0))
    full = lambda a: pl.BlockSpec(a.shape, lambda i: (0,) * a.ndim)
    tab = pl.BlockSpec((tm, LANES), lambda i: (i % n_table_blocks, 0))
    out_shape = [
        jax.ShapeDtypeStruct((n, 2 * kw), jnp.float32),
        jax.ShapeDtypeStruct((n, IDX_DIM), jnp.float32),
        jax.ShapeDtypeStruct((n, SMALL_W), jnp.float32),
        jax.ShapeDtypeStruct((n, CONV_DIM), jnp.float32),
        jax.ShapeDtypeStruct((n, H_B * DV_B), jnp.float32),
        jax.ShapeDtypeStruct((n, 2 * D_MODEL), jnp.float32),
    ]
    out_specs = [row(2 * kw), row(IDX_DIM), row(SMALL_W), row(CONV_DIM), row(H_B * DV_B), row(2 * D_MODEL)]
    if key_major:
        n_qb = tm // Q_TILE
        blk = lambda r, w: pl.BlockSpec((n_qb, r, w), lambda i: (i, 0, 0))
        out_shape += [
            jax.ShapeDtypeStruct((n // Q_TILE, LANES, N_HEADS_A * Q_TILE), MXU_DTYPE),
            jax.ShapeDtypeStruct((n // Q_TILE, IDX_DIM, IDX_HEADS * Q_TILE), MXU_DTYPE),
            jax.ShapeDtypeStruct((n // Q_TILE, IDX_HEADS, Q_TILE), jnp.float32),
            jax.ShapeDtypeStruct((n, kw), MXU_DTYPE),
            jax.ShapeDtypeStruct((n, IDX_DIM), MXU_DTYPE),
            jax.ShapeDtypeStruct((n // K_TILE, kw + ONES_ROWS, K_TILE), MXU_DTYPE),
        ]
        out_specs += [blk(LANES, N_HEADS_A * Q_TILE), blk(IDX_DIM, IDX_HEADS * Q_TILE), blk(IDX_HEADS, Q_TILE),
                      row(kw), row(IDX_DIM), pl.BlockSpec((1, kw + ONES_ROWS, K_TILE), lambda i: (i, 0, 0))]
    else:
        out_shape += [jax.ShapeDtypeStruct((N_HEADS_A, n, LANES), MXU_DTYPE),
                      jax.ShapeDtypeStruct((IDX_HEADS, n, IDX_DIM), MXU_DTYPE)]
        out_specs += [pl.BlockSpec((N_HEADS_A, tm, LANES), lambda i: (0, i, 0)),
                      pl.BlockSpec((IDX_HEADS, tm, IDX_DIM), lambda i: (0, i, 0))]
    return pl.pallas_call(
        functools.partial(_in_proj_kernel, key_major),
        grid=(n // tm,),
        in_specs=[row(D_MODEL), full(norm_mix[None, :]), full(w_packed), full(bd), full(qg), full(kg),
                  tab, tab, tab],
        out_specs=tuple(out_specs),
        out_shape=tuple(out_shape),
        compiler_params=pltpu.CompilerParams(dimension_semantics=("arbitrary",),
                                             vmem_limit_bytes=VMEM_LIMIT_BYTES),
        name="in_proj",
    )(x2d, norm_mix[None, :].astype(jnp.float32), w_packed, bd, qg, kg, cos_t, sin_lo, sin_hi)


_INT_MAG = 0x7FFFFFFF


def _f32_key(x):
    b = lax.bitcast_convert_type(x, jnp.int32)
    return b ^ (lax.shift_right_arithmetic(b, 31) & _INT_MAG)


def _key_f32(k):
    b = k ^ (lax.shift_right_arithmetic(k, 31) & _INT_MAG)
    return lax.bitcast_convert_type(b, jnp.float32)


def _topk_threshold(count_ge, count_tie, row_min, row_max, n_adm, topk, n_keys, fixed_steps=18, linear_steps=24):
    kf = jnp.float32(topk)
    need = n_adm > topk
    lo_k = _f32_key(row_min)
    hi_k = _f32_key(row_max) + 1
    thr = jnp.where(need, row_min, -jnp.inf)
    state = (lo_k, hi_k, thr, jnp.zeros_like(row_min), jnp.where(need, 0, 1).astype(jnp.int32),
             jnp.zeros_like(lo_k))

    def step(linear, st):
        lo_k, hi_k, thr, cnt_hi, done, tie = st
        adjacent = hi_k == lo_k + 1
        lo_f, hi_f = _key_f32(lo_k), _key_f32(hi_k)
        mid_lin = _f32_key(lo_f + 0.5 * (hi_f - lo_f))
        mid_lin = jnp.minimum(jnp.maximum(mid_lin, lo_k + 1), hi_k - 1)
        mid_int = (lo_k & hi_k) + lax.shift_right_arithmetic(lo_k ^ hi_k, 1)
        mid = mid_lin if linear is True else jnp.where(linear, mid_lin, mid_int)
        mid_f = _key_f32(mid)
        cnt = count_ge(mid_f)
        live = jnp.logical_and(done == 0, jnp.logical_not(adjacent))
        hit = jnp.logical_and(live, cnt == kf)
        up = jnp.logical_and(live, cnt > kf)
        dn = jnp.logical_and(live, cnt < kf)
        new_tie = jnp.logical_and(done == 0, adjacent)
        thr = jnp.where(hit, mid_f, jnp.where(new_tie, lo_f, thr))
        tie = jnp.where(new_tie, 1, tie)
        done = jnp.where(jnp.logical_or(hit, new_tie), 1, done)
        lo_k = jnp.where(up, mid, lo_k)
        hi_k = jnp.where(dn, mid, hi_k)
        cnt_hi = jnp.where(dn, cnt, cnt_hi)
        return (lo_k, hi_k, thr, cnt_hi, done, tie)

    state = lax.fori_loop(0, fixed_steps, lambda _, st: step(True, st), state)

    def cond(st):
        it, active = st[0], st[1]
        return jnp.logical_and(it < 80, active > 0)

    def body(st):
        it = st[0]
        new = step(it < linear_steps, st[2:])
        return (it + 1, jnp.max(1 - new[4])) + new

    st = lax.while_loop(cond, body, (jnp.int32(fixed_steps), jnp.max(1 - state[4])) + state)
    thr, cnt_hi, tie = st[4], st[5], st[7]

    need_ties = kf - cnt_hi
    n_bits = max(1, int(math.ceil(math.log2(n_keys + 1))))
    any_tie = jnp.max(tie)

    def tie_body(_, lm):
        lo_m, hi_m = lm
        mid = lax.shift_right_arithmetic(lo_m + hi_m, 1)
        ge = count_tie(thr, mid) >= need_ties
        return jnp.where(ge, lo_m, mid), jnp.where(ge, mid, hi_m)

    lo_m0 = jnp.zeros_like(lo_k)
    hi_m0 = jnp.full_like(lo_k, n_keys)
    _, hi_m = lax.fori_loop(0, jnp.where(any_tie > 0, n_bits + 1, 0), tie_body, (lo_m0, hi_m0))
    cut = jnp.where(tie > 0, hi_m, n_keys + 1)
    return thr, cut


Q_TILE = 128
K_TILE = 256
K_UNROLL = 4
SCAN_UNROLL = 4
ONES_ROWS = 16


def _dsa_prompt_kernel(topk, qt_ref, iqt_ref, wt_ref, kb_ref, ikb_ref, vt_ref, o_ref, sc_ref, acc_ref, m_ref):
    i = pl.program_id(1)
    tq, kc = Q_TILE, K_TILE
    n_keys = sc_ref.shape[0] * kc
    nchunk = (i + 2) // 2
    qpos = i * tq + lax.broadcasted_iota(jnp.int32, (kc, tq), 1)
    krow = lax.broadcasted_iota(jnp.int32, (kc, tq), 0)
    qpos8 = qpos[0:SUBLANES]

    def col_reduce(x, op):
        return op(x.reshape(kc // SUBLANES, SUBLANES, tq), axis=0)

    def all_rows(x, op2):
        for shift in (4, 2, 1):
            x = op2(x, pltpu.roll(x, shift, 0))
        return x

    def tile_loop(first, rest, init, unroll):
        def trip(t, carry):
            heads = [first(t * unroll + sub) for sub in range(unroll)]
            for sub in range(unroll):
                carry = rest(t * unroll + sub, heads[sub], carry)
            return carry
        full = nchunk // unroll
        carry = lax.fori_loop(0, full, trip, init)
        return lax.fori_loop(full * unroll, nchunk, lambda c, carry: rest(c, first(c), carry), carry)

    def key_rows(c):
        return pl.ds(pl.multiple_of(c * kc, kc), kc)

    w = wt_ref[0]
    iqt = iqt_ref[0]
    s_scale = IDX_DIM ** -0.5 * IDX_HEADS ** -0.5

    def score_dots(c):
        return jnp.dot(ikb_ref[key_rows(c), :], iqt, preferred_element_type=jnp.float32)

    def score_tile(c, d, carry):
        s = w[0:1] * jnp.maximum(d[:, 0:tq], 0.0)
        for h in range(1, IDX_HEADS):
            s = s + w[h:h + 1] * jnp.maximum(d[:, h * tq:(h + 1) * tq], 0.0)
        sc_ref[c] = jnp.where(c * kc + krow <= qpos, s * s_scale, NEG_INF)
        return carry

    tile_loop(score_dots, score_tile, 0, K_UNROLL)

    def scan(body, init):
        return tile_loop(lambda c: sc_ref[c], body, init, SCAN_UNROLL)

    def count(pred):
        def body(c, s, acc):
            return acc + col_reduce(jnp.where(pred(s, c * kc + krow), 1.0, 0.0), jnp.sum)
        return all_rows(scan(body, jnp.zeros((SUBLANES, tq), jnp.float32)), jnp.add)

    def count_ge(c):
        return count(lambda s, kpos: s >= c[0:1])

    def count_tie(v, m):
        return count(lambda s, kpos: jnp.logical_and(s == v[0:1], kpos < m[0:1]))

    def minmax(c, s, mm):
        adm = c * kc + krow <= qpos
        return (jnp.minimum(mm[0], col_reduce(jnp.where(adm, s, jnp.inf), jnp.min)),
                jnp.maximum(mm[1], col_reduce(jnp.where(adm, s, -jnp.inf), jnp.max)))

    mn, mx = scan(minmax, (jnp.full((SUBLANES, tq), jnp.inf, jnp.float32),
                           jnp.full((SUBLANES, tq), -jnp.inf, jnp.float32)))
    thr, cut = _topk_threshold(count_ge, count_tie, all_rows(mn, jnp.minimum), all_rows(mx, jnp.maximum),
                               qpos8 + 1, topk, n_keys)
    thr_row, cut_row = thr[0:1], cut[0:1]

    m_ref[...] = jnp.full(m_ref.shape, 0.5 * NEG_INF, jnp.float32)
    acc_ref[...] = jnp.zeros(acc_ref.shape, jnp.float32)
    qt = qt_ref[0]
    kw = N_KV_A * HEAD_DIM_A

    def logits(c):
        return jnp.dot(kb_ref[key_rows(c), :], qt, preferred_element_type=jnp.float32)

    def attend_tile(c, lg, carry):
        s = sc_ref[c]
        kpos = c * kc + krow
        sel = jnp.logical_or(s > thr_row, jnp.logical_and(s == thr_row, kpos < cut_row))
        bias = jnp.where(jnp.logical_and(sel, kpos <= qpos), 0.0, NEG_INF)
        ps, alphas = [], []
        for h in range(N_HEADS_A):
            cols = slice(h * tq, (h + 1) * tq)
            lgh = lg[:, cols] + bias
            m_old = m_ref[:, cols]
            m_new = jnp.maximum(m_old, all_rows(col_reduce(lgh, jnp.max), jnp.maximum))
            alphas.append(jnp.exp2(m_old - m_new)[0:1])
            ps.append(jnp.exp2(lgh - m_new[0:1]).astype(MXU_DTYPE))
            m_ref[:, cols] = m_new
        pv = jnp.dot(vt_ref[c], jnp.concatenate(ps, axis=1), preferred_element_type=jnp.float32)
        acc_ref[...] = acc_ref[...] * jnp.concatenate(alphas, axis=1) + pv
        return carry

    tile_loop(logits, attend_tile, 0, K_UNROLL)

    acc = acc_ref[...]
    o_t = acc[0:kw] / acc[kw:kw + 1]
    for p in range(N_HEADS_A // 2):
        n = (2 * p) // GROUP_A
        pair = jnp.concatenate([o_t[n * HEAD_DIM_A:(n + 1) * HEAD_DIM_A, (2 * p + e) * tq:(2 * p + e + 1) * tq]
                                for e in range(2)], axis=0)
        o_ref[:, p * LANES:(p + 1) * LANES] = pair.T.astype(o_ref.dtype)


def _dsa_prompt(qt, iqt, wt, kb, ikb, vt, batch, seq):
    tq, kc = Q_TILE, K_TILE
    assert seq % kc == 0
    nq = seq // tq
    nk = seq // kc
    n = batch * seq
    kw = N_KV_A * HEAD_DIM_A
    topk = min(TOPK_MAX, seq // 4)
    return pl.pallas_call(
        functools.partial(_dsa_prompt_kernel, topk),
        grid=(batch, nq),
        in_specs=[
            pl.BlockSpec((1, LANES, N_HEADS_A * tq), lambda b, i: (b * nq + i, 0, 0)),
            pl.BlockSpec((1, IDX_DIM, IDX_HEADS * tq), lambda b, i: (b * nq + i, 0, 0)),
            pl.BlockSpec((1, IDX_HEADS, tq), lambda b, i: (b * nq + i, 0, 0)),
            pl.BlockSpec((seq, kw), lambda b, i: (b, 0)),
            pl.BlockSpec((seq, IDX_DIM), lambda b, i: (b, 0)),
            pl.BlockSpec((nk, kw + ONES_ROWS, kc), lambda b, i: (b, 0, 0)),
        ],
        out_specs=pl.BlockSpec((tq, BRANCH_WIDTH), lambda b, i: (b * nq + i, 0)),
        out_shape=jax.ShapeDtypeStruct((n, BRANCH_WIDTH), MXU_DTYPE),
        scratch_shapes=[
            pltpu.VMEM((nk, kc, tq), jnp.float32),
            pltpu.VMEM((kw + ONES_ROWS, N_HEADS_A * tq), jnp.float32),
            pltpu.VMEM((SUBLANES, N_HEADS_A * tq), jnp.float32),
        ],
        compiler_params=pltpu.CompilerParams(dimension_semantics=("arbitrary", "arbitrary"),
                                             vmem_limit_bytes=VMEM_LIMIT_BYTES),
        name="dsa_prompt",
    )(qt, iqt, wt, kb, ikb, vt)


def _page_copy(pt_ref, cache_ref, buf_ref, sem_ref, seq, page, slot):
    lanes = pl.ds(pl.multiple_of(page * PAGE_SIZE, PAGE_SIZE), PAGE_SIZE)
    return pltpu.make_async_copy(cache_ref.at[pt_ref[seq, page]], buf_ref.at[slot, :, lanes], sem_ref.at[slot])


def _pages_start(pt_ref, cache_ref, buf_ref, sem_ref, seq, slot, n_pages):
    def body(p, c):
        _page_copy(pt_ref, cache_ref, buf_ref, sem_ref, seq, p, slot).start()
        return c
    lax.fori_loop(0, n_pages, body, 0)


def _pages_wait(pt_ref, cache_ref, buf_ref, sem_ref, seq, slot, n_pages):
    def body(p, c):
        _page_copy(pt_ref, cache_ref, buf_ref, sem_ref, seq, p, slot).wait()
        return c
    lax.fori_loop(0, n_pages, body, 0)


def _dsa_sample_score_kernel(topk, n_pages, pt_ref, iq_ref, iw_ref, iknew_ref, cache_ik_ref,
                             sc_ref, thr_ref, cut_ref, ikbuf_ref, sem_ref):
    s = pl.program_id(0)
    n_seq = pl.num_programs(0)
    past = n_pages * PAGE_SIZE
    n_tiles = sc_ref.shape[0]
    slot = s % 2

    @pl.when(s == 0)
    def _():
        _pages_start(pt_ref, cache_ik_ref, ikbuf_ref, sem_ref, 0, 0, n_pages)

    @pl.when(s + 1 < n_seq)
    def _():
        _pages_start(pt_ref, cache_ik_ref, ikbuf_ref, sem_ref, s + 1, 1 - slot, n_pages)

    _pages_wait(pt_ref, cache_ik_ref, ikbuf_ref, sem_ref, s, slot, n_pages)

    iq = iq_ref[0]
    w = iw_ref[0]
    s_scale = IDX_DIM ** -0.5 * IDX_HEADS ** -0.5
    d = _mm(iq, ikbuf_ref[slot])
    srow = jnp.sum(w * jnp.maximum(d, 0.0), axis=0, keepdims=True) * s_scale
    for j in range(n_pages):
        sc_ref[j, pl.ds(s, 1), :] = srow[:, j * LANES:(j + 1) * LANES]
    ik_new = iknew_ref[0].astype(MXU_DTYPE).astype(jnp.float32)
    d_self = jnp.sum(iq.astype(jnp.float32) * ik_new, axis=1, keepdims=True)
    s_self = jnp.sum(w * jnp.maximum(d_self, 0.0), axis=0, keepdims=True) * s_scale
    lane1 = lax.broadcasted_iota(jnp.int32, (1, LANES), 1)
    sc_ref[n_tiles - 1, pl.ds(s, 1), :] = jnp.where(lane1 == 0, s_self, NEG_INF)

    @pl.when(s == n_seq - 1)
    def _():
        rows = sc_ref.shape[1]
        lane = lax.broadcasted_iota(jnp.int32, (rows, LANES), 1)

        def count(pred):
            def body(j, acc):
                return acc + jnp.where(pred(sc_ref[j], j * LANES + lane), 1.0, 0.0)
            acc = lax.fori_loop(0, n_tiles, body, jnp.zeros((rows, LANES), jnp.float32))
            return jnp.broadcast_to(jnp.sum(acc, axis=1, keepdims=True), (rows, LANES))

        def count_ge(c):
            return count(lambda t, kpos: t >= c)

        def count_tie(v, m):
            return count(lambda t, kpos: jnp.logical_and(t == v, kpos < m))

        def minmax(j, mm):
            t = sc_ref[j]
            adm = j * LANES + lane <= past
            return (jnp.minimum(mm[0], jnp.where(adm, t, jnp.inf)),
                    jnp.maximum(mm[1], jnp.where(adm, t, -jnp.inf)))

        mn, mx = lax.fori_loop(0, n_tiles, minmax, (jnp.full((rows, LANES), jnp.inf, jnp.float32),
                                                    jnp.full((rows, LANES), -jnp.inf, jnp.float32)))
        row_min = jnp.broadcast_to(jnp.min(mn, axis=1, keepdims=True), (rows, LANES))
        row_max = jnp.broadcast_to(jnp.max(mx, axis=1, keepdims=True), (rows, LANES))
        n_adm = jnp.full((rows, LANES), past + 1, jnp.int32)
        thr, cut = _topk_threshold(count_ge, count_tie, row_min, row_max, n_adm, topk, n_tiles * LANES)
        thr_ref[...] = thr
        cut_ref[...] = cut


def _dsa_sample_attend_kernel(n_pages, pt_ref, q_ref, sc_ref, thr_ref, cut_ref, kvnew_ref,
                              cache_k_ref, cache_v_ref, o_ref, kbuf_ref, vbuf_ref, ksem_ref, vsem_ref):
    s = pl.program_id(0)
    n_seq = pl.num_programs(0)
    past = n_pages * PAGE_SIZE
    slot = s % 2

    def start(seq, sl):
        _pages_start(pt_ref, cache_k_ref, kbuf_ref, ksem_ref, seq, sl, n_pages)
        _pages_start(pt_ref, cache_v_ref, vbuf_ref, vsem_ref, seq, sl, n_pages)

    @pl.when(s == 0)
    def _():
        start(0, 0)

    @pl.when(s + 1 < n_seq)
    def _():
        start(s + 1, 1 - slot)

    _pages_wait(pt_ref, cache_k_ref, kbuf_ref, ksem_ref, s, slot, n_pages)
    _pages_wait(pt_ref, cache_v_ref, vbuf_ref, vsem_ref, s, slot, n_pages)

    q = q_ref[0]
    thr = thr_ref[0][:, 0:1]
    cut = cut_ref[0][:, 0:1]
    kw = N_KV_A * HEAD_DIM_A
    k_new = kvnew_ref[0][:, 0:kw].astype(MXU_DTYPE).astype(jnp.float32)
    v_new = kvnew_ref[0][:, kw:2 * kw].astype(MXU_DTYPE).astype(jnp.float32)
    sc = sc_ref[0]

    def selected(srow, kpos):
        return jnp.logical_or(srow > thr, jnp.logical_and(srow == thr, kpos < cut))

    kpos = lax.broadcasted_iota(jnp.int32, (1, past), 1)
    bias = jnp.where(selected(sc[:, 0:past], kpos), 0.0, NEG_INF)
    lg = _mm(q, kbuf_ref[slot]) + bias
    lg_self = jnp.sum(q.astype(jnp.float32) * k_new, axis=1, keepdims=True)
    lg_self = jnp.where(selected(sc[:, past:past + 1], past), lg_self, NEG_INF)
    m = jnp.maximum(jnp.max(lg, axis=1, keepdims=True), lg_self)
    p = jnp.exp(lg - m)
    p_self = jnp.exp(lg_self - m)
    denom = jnp.sum(p, axis=1, keepdims=True) + p_self
    o = (_mm_nt(p, vbuf_ref[slot]) + p_self * v_new) / denom
    parts = []
    for h in range(N_HEADS_A):
        n = h // GROUP_A
        parts.append(o[h:h + 1, n * HEAD_DIM_A:(n + 1) * HEAD_DIM_A])
    o_ref[0] = jnp.concatenate(parts, axis=1)


def _dsa_sample(qexp, iqhm, small, ik_new, kv_new, cache_k, cache_v, cache_ik, page_table):
    db, n_pages = page_table.shape
    past = n_pages * PAGE_SIZE
    n_pool = cache_ik.shape[0]
    topk = min(TOPK_MAX, (past + 1) // 4)
    n_tiles = n_pages + 1
    kw = N_KV_A * HEAD_DIM_A
    q_s = jnp.swapaxes(qexp, 0, 1)
    iq_s = jnp.swapaxes(iqhm, 0, 1)
    iw_s = small[:, 0:IDX_HEADS].reshape(db, IDX_HEADS, 1)
    ck_t = jnp.transpose(cache_k, (0, 2, 3, 1)).reshape(n_pool, kw, PAGE_SIZE)
    cv_t = jnp.transpose(cache_v, (0, 2, 3, 1)).reshape(n_pool, kw, PAGE_SIZE)
    cik_t = jnp.swapaxes(cache_ik, 1, 2)
    cparams = pltpu.CompilerParams(dimension_semantics=("arbitrary",), vmem_limit_bytes=VMEM_LIMIT_BYTES)
    per_seq = lambda *shape: pl.BlockSpec((1,) + shape, lambda s, pt: (s,) + (0,) * len(shape))
    whole = lambda *shape: pl.BlockSpec(shape, lambda s, pt: (0,) * len(shape))
    any_spec = pl.BlockSpec(memory_space=pl.ANY)

    sc, thr, cut = pl.pallas_call(
        functools.partial(_dsa_sample_score_kernel, topk, n_pages),
        grid_spec=pltpu.PrefetchScalarGridSpec(
            num_scalar_prefetch=1,
            grid=(db,),
            in_specs=[per_seq(IDX_HEADS, IDX_DIM), per_seq(IDX_HEADS, 1), per_seq(1, IDX_DIM), any_spec],
            out_specs=(whole(n_tiles, db, LANES), whole(db, LANES), whole(db, LANES)),
            scratch_shapes=[pltpu.VMEM((2, IDX_DIM, past), jnp.float32), pltpu.SemaphoreType.DMA((2,))],
        ),
        out_shape=(jax.ShapeDtypeStruct((n_tiles, db, LANES), jnp.float32),
                   jax.ShapeDtypeStruct((db, LANES), jnp.float32),
                   jax.ShapeDtypeStruct((db, LANES), jnp.int32)),
        compiler_params=cparams,
        name="dsa_sample_score",
    )(page_table, iq_s, iw_s, ik_new.reshape(db, 1, IDX_DIM), cik_t)

    o = pl.pallas_call(
        functools.partial(_dsa_sample_attend_kernel, n_pages),
        grid_spec=pltpu.PrefetchScalarGridSpec(
            num_scalar_prefetch=1,
            grid=(db,),
            in_specs=[per_seq(N_HEADS_A, LANES), per_seq(1, n_tiles * LANES), per_seq(1, LANES), per_seq(1, LANES),
                      per_seq(1, 2 * kw), any_spec, any_spec],
            out_specs=per_seq(1, BRANCH_WIDTH),
            scratch_shapes=[pltpu.VMEM((2, kw, past), jnp.float32), pltpu.VMEM((2, kw, past), jnp.float32),
                            pltpu.SemaphoreType.DMA((2,)), pltpu.SemaphoreType.DMA((2,))],
        ),
        out_shape=jax.ShapeDtypeStruct((db, 1, BRANCH_WIDTH), jnp.float32),
        compiler_params=cparams,
        name="dsa_sample_attend",
    )(page_table, q_s, jnp.swapaxes(sc, 0, 1).reshape(db, 1, n_tiles * LANES),
      thr.reshape(db, 1, LANES), cut.reshape(db, 1, LANES),
      kv_new.reshape(db, 1, 2 * kw), ck_t, cv_t)
    return o.reshape(db, BRANCH_WIDTH)


GDN_CHUNK = 128
A_LANE = IDX_HEADS
B_LANE = IDX_HEADS + H_B


def _split2(x):
    h = x.astype(MXU_DTYPE)
    return h, (x - h.astype(jnp.float32)).astype(MXU_DTYPE)


def _mm2(a, b):
    a1, a2 = _split2(a)
    b1, b2 = _split2(b)
    d = functools.partial(jnp.dot, preferred_element_type=jnp.float32)
    return d(a1, b1) + (d(a1, b2) + d(a2, b1))


def _unit_lower_inverses(mats):
    n = mats[0].shape[0]
    eye = (lax.broadcasted_iota(jnp.int32, (n, n), 0) == lax.broadcasted_iota(jnp.int32, (n, n), 1))
    ss = [jnp.where(eye, 1.0, 0.0) - a for a in mats]
    ps = [_mm2(a, a) for a in mats]
    k = 2
    while k < n:
        ss = [s + _mm2(s, p) for s, p in zip(ss, ps)]
        k *= 2
        if k < n:
            ps = [_mm2(p, p) for p in ps]
    return ss


def _l2norm(x):
    return x * lax.rsqrt(jnp.sum(x * x, axis=-1, keepdims=True) + NORM_EPS)


def _gdn_prompt_kernel(u_ref, small_ref, z_ref, conv0_ref, s0_ref, wconv_ref, alog_ref, dtb_ref, dnorm_ref,
                       o_ref, conv_out_ref, s_out_ref, ucat_ref, state_ref):
    n = pl.program_id(0)
    nb = u_ref.shape[0]
    c = GDN_CHUNK
    head = SUBLANES
    tail = CONV_K - 1

    @pl.when(n == 0)
    def _():
        ucat_ref[:, head - tail:head, :] = conv0_ref[...]
        state_ref[...] = s0_ref[...]

    row = lax.broadcasted_iota(jnp.int32, (c, c), 0)
    col = lax.broadcasted_iota(jnp.int32, (c, c), 1)
    lower = row >= col
    strict = row > col
    ltri = jnp.where(lower, 1.0, 0.0).astype(MXU_DTYPE)
    d = functools.partial(jnp.dot, preferred_element_type=jnp.float32)

    chains = [(b, h) for b in range(nb) for h in range(H_B)]
    qw = H_B * DK_B
    qs, ks, vs, betas, gcols = [], [], [], [], []
    for b in range(nb):
        ucat_ref[b, head:head + c, :] = u_ref[b]
        y = wconv_ref[tail:tail + 1, :] * ucat_ref[b, head:head + c, :]
        for j in range(tail):
            y = y + wconv_ref[j:j + 1, :] * ucat_ref[b, head - tail + j:head - tail + j + c, :]
        cv = _silu(y)
        carry_rows = ucat_ref[b, head + c - tail:head + c, :]
        ucat_ref[b, head - tail:head, :] = carry_rows
        conv_out_ref[b] = carry_rows
        sm = small_ref[b]
        g_all = -jnp.exp(alog_ref[...]) * _softplus(sm + dtb_ref[...])
        beta_all = _sigmoid(sm)
        a1, a2, a3 = _split3(g_all)
        gc_all = d(ltri, a1) + (d(ltri, a2) + d(ltri, a3))
        for h in range(H_B):
            qs.append(_l2norm(cv[:, h * DK_B:(h + 1) * DK_B]) * (DK_B ** -0.5))
            ks.append(_l2norm(cv[:, qw + h * DK_B:qw + (h + 1) * DK_B]))
            vs.append(cv[:, 2 * qw + h * DV_B:2 * qw + (h + 1) * DV_B])
            betas.append(jnp.broadcast_to(beta_all[:, B_LANE + h:B_LANE + h + 1], (c, LANES)))
            gcols.append(jnp.broadcast_to(gc_all[:, A_LANE + h:A_LANE + h + 1], (c, c)))
    decays = [jnp.where(lower, jnp.exp(jnp.where(lower, g - g.T, 0.0)), 0.0) for g in gcols]
    egs = [jnp.exp(g) for g in gcols]
    g_lasts = [g[c - 1:c, :] for g in gcols]
    kbs = [k * b for k, b in zip(ks, betas)]
    vbs = [v * b for v, b in zip(vs, betas)]
    kks = [_mm_nt(kb, k) for kb, k in zip(kbs, ks)]
    qks = [_mm_nt(q, k) for q, k in zip(qs, ks)]
    t_invs = _unit_lower_inverses([jnp.where(strict, kk * dc, 0.0) for kk, dc in zip(kks, decays)])
    sols = [_mm2(t, jnp.concatenate([vb, kb * eg], axis=1)) for t, vb, kb, eg in zip(t_invs, vbs, kbs, egs)]
    s_olds = [state_ref[b, h] for b, h in chains]
    v_news = [sol[:, 0:DV_B] - _mm(sol[:, DV_B:DV_B + DK_B], s) for sol, s in zip(sols, s_olds)]
    o_hs = [_mm(q * eg, s) + _mm(qk * dc, v_new)
            for q, eg, s, qk, dc, v_new in zip(qs, egs, s_olds, qks, decays, v_news)]
    for i, (b, h) in enumerate(chains):
        k_dec = ks[i] * jnp.exp(g_lasts[i] - gcols[i])
        state_ref[b, h] = s_olds[i] * jnp.exp(g_lasts[i]) + _mm_tn(k_dec, v_news[i])
    for i, (b, h) in enumerate(chains):
        o_h = o_hs[i]
        ms = jnp.mean(o_h * o_h, axis=-1, keepdims=True)
        o_n = o_h * lax.rsqrt(ms + NORM_EPS) * dnorm_ref[...]
        gate = _silu(z_ref[b, :, h * DV_B:(h + 1) * DV_B])
        o_ref[b, :, h * DV_B:(h + 1) * DV_B] = (o_n * gate).astype(o_ref.dtype)

    @pl.when(n == pl.num_programs(0) - 1)
    def _():
        s_out_ref[...] = state_ref[...]


GDN_SEQ_TILE = 8


def _gdn_sample_kernel(u_ref, cb_ref, small_ref, z_ref, s0_ref, wconv_ref, alog_ref, dtb_ref, dnorm_ref,
                       o_ref, conv_out_ref, s_out_ref):
    ts = GDN_SEQ_TILE
    tail = CONV_K - 1
    u_new = u_ref[...]
    y = wconv_ref[tail:tail + 1, :] * u_new
    for j in range(tail):
        y = y + wconv_ref[j:j + 1, :] * cb_ref[j]
    cv = _silu(y)
    for j in range(tail - 1):
        conv_out_ref[j] = cb_ref[j + 1]
    conv_out_ref[tail - 1] = u_new

    sm = small_ref[...]
    eg_all = jnp.exp(-jnp.exp(alog_ref[...]) * _softplus(sm + dtb_ref[...]))
    beta_all = _sigmoid(sm)
    qw = H_B * DK_B
    for h in range(H_B):
        q = _l2norm(cv[:, h * DK_B:(h + 1) * DK_B]) * (DK_B ** -0.5)
        k = _l2norm(cv[:, qw + h * DK_B:qw + (h + 1) * DK_B])
        v = cv[:, 2 * qw + h * DV_B:2 * qw + (h + 1) * DV_B]
        eg = eg_all[:, A_LANE + h:A_LANE + h + 1]
        beta = beta_all[:, B_LANE + h:B_LANE + h + 1]
        qk = jnp.sum(q * k, axis=-1, keepdims=True)
        k_t, q_t = k.T, q.T
        rows = []
        for r in range(ts):
            s_old = s0_ref[r, h]
            kc = k_t[:, r:r + 1]
            ks = jnp.sum(s_old * kc, axis=0, keepdims=True)
            qs = jnp.sum(s_old * q_t[:, r:r + 1], axis=0, keepdims=True)
            eg_r = eg[r:r + 1, :]
            v_new = beta[r:r + 1, :] * (v[r:r + 1, :] - eg_r * ks)
            rows.append(eg_r * qs + qk[r:r + 1, :] * v_new)
            s_out_ref[r, h] = s_old * eg_r + kc * v_new
        o_h = jnp.concatenate(rows, axis=0)
        ms = jnp.mean(o_h * o_h, axis=-1, keepdims=True)
        o_n = o_h * lax.rsqrt(ms + NORM_EPS) * dnorm_ref[...]
        o_ref[:, h * DV_B:(h + 1) * DV_B] = (o_n * _silu(z_ref[:, h * DV_B:(h + 1) * DV_B])).astype(o_ref.dtype)


def _gdn_sample(u, small, z, conv_buf, s0, w_conv, a_log, dt_bias, delta_norm):
    db = u.shape[0]
    ts = GDN_SEQ_TILE
    assert db % ts == 0
    tail = CONV_K - 1
    alog_row, dtb_row = _gate_rows(a_log, dt_bias)
    row = lambda w: pl.BlockSpec((ts, w), lambda i: (i, 0))
    full = lambda *shape: pl.BlockSpec(shape, lambda i: (0,) * len(shape))
    cb_spec = pl.BlockSpec((tail, ts, CONV_DIM), lambda i: (0, i, 0))
    st_spec = pl.BlockSpec((ts, H_B, DK_B, DV_B), lambda i: (i, 0, 0, 0))
    o, conv_t, s_new = pl.pallas_call(
        _gdn_sample_kernel,
        grid=(db // ts,),
        in_specs=[row(CONV_DIM), cb_spec, row(SMALL_W), row(H_B * DV_B), st_spec,
                  full(CONV_K, CONV_DIM), full(1, SMALL_W), full(1, SMALL_W), full(1, DV_B)],
        out_specs=(row(H_B * DV_B), cb_spec, st_spec),
        out_shape=(jax.ShapeDtypeStruct((db, H_B * DV_B), jnp.float32),
                   jax.ShapeDtypeStruct((tail, db, CONV_DIM), jnp.float32),
                   jax.ShapeDtypeStruct((db, H_B, DK_B, DV_B), jnp.float32)),
        compiler_params=pltpu.CompilerParams(dimension_semantics=("arbitrary",),
                                             vmem_limit_bytes=VMEM_LIMIT_BYTES),
        name="gdn_sample",
    )(u, jnp.swapaxes(conv_buf, 0, 1), small, z, s0, w_conv.astype(jnp.float32), alog_row, dtb_row,
      delta_norm.astype(jnp.float32)[None, :])
    return o, jnp.swapaxes(conv_t, 0, 1), s_new


def _gate_rows(a_log, dt_bias):
    alog_row = jnp.zeros((1, SMALL_W), jnp.float32).at[0, A_LANE:A_LANE + H_B].set(a_log.astype(jnp.float32))
    dtb_row = jnp.zeros((1, SMALL_W), jnp.float32).at[0, A_LANE:A_LANE + H_B].set(dt_bias.astype(jnp.float32))
    return alog_row, dtb_row


def _gdn_prompt(u, small, z, conv0, s0, w_conv, a_log, dt_bias, delta_norm, batch, seq):
    c = GDN_CHUNK
    assert seq % c == 0
    alog_row, dtb_row = _gate_rows(a_log, dt_bias)
    row = lambda w: pl.BlockSpec((batch, c, w), lambda i: (0, i, 0))
    full = lambda *shape: pl.BlockSpec(shape, lambda i: (0,) * len(shape))
    o, conv_new, s_new = pl.pallas_call(
        _gdn_prompt_kernel,
        grid=(seq // c,),
        in_specs=[row(CONV_DIM), row(SMALL_W), row(H_B * DV_B), full(batch, CONV_K - 1, CONV_DIM),
                  full(batch, H_B, DK_B, DV_B), full(CONV_K, CONV_DIM), full(1, SMALL_W), full(1, SMALL_W),
                  full(1, DV_B)],
        out_specs=(row(H_B * DV_B), full(batch, CONV_K - 1, CONV_DIM), full(batch, H_B, DK_B, DV_B)),
        out_shape=(jax.ShapeDtypeStruct((batch, seq, H_B * DV_B), MXU_DTYPE),
                   jax.ShapeDtypeStruct((batch, CONV_K - 1, CONV_DIM), jnp.float32),
                   jax.ShapeDtypeStruct((batch, H_B, DK_B, DV_B), jnp.float32)),
        scratch_shapes=[pltpu.VMEM((batch, SUBLANES + c, CONV_DIM), jnp.float32),
                        pltpu.VMEM((batch, H_B, DK_B, DV_B), jnp.float32)],
        compiler_params=pltpu.CompilerParams(dimension_semantics=("arbitrary",),
                                             vmem_limit_bytes=VMEM_LIMIT_BYTES),
        name="gdn_prompt",
    )(u.reshape(batch, seq, CONV_DIM), small.reshape(batch, seq, SMALL_W), z.reshape(batch, seq, H_B * DV_B),
      conv0, s0, w_conv.astype(jnp.float32), alog_row, dtb_row, delta_norm.astype(jnp.float32)[None, :])
    return o.reshape(batch * seq, H_B * DV_B), conv_new, s_new


def _merge_kernel(x_ref, oa_ref, ob_ref, gl_ref, wba_ref, wbb_ref, wout_ref, gain_ref, x1_ref, hn_ref):
    pa = _mm(oa_ref[...], wba_ref[...])
    pb = _mm(ob_ref[...], wbb_ref[...])
    mix = _sigmoid(gl_ref[:, 0:D_MODEL]) * pa + _sigmoid(gl_ref[:, D_MODEL:2 * D_MODEL]) * pb
    x1 = x_ref[...] + _mm(mix, wout_ref[...])
    x1_ref[...] = x1
    ms = jnp.mean(x1 * x1, axis=-1, keepdims=True)
    hn_ref[...] = (x1 * lax.rsqrt(ms + NORM_EPS) * gain_ref[...]).astype(hn_ref.dtype)


def _merge(x2d, o_a, o_b, gl, w_branch, w_out, norm_ffn, tm):
    n = x2d.shape[0]
    assert n % tm == 0
    row = lambda w: pl.BlockSpec((tm, w), lambda i: (i, 0))
    full = lambda *shape: pl.BlockSpec(shape, lambda i: (0,) * len(shape))
    return pl.pallas_call(
        _merge_kernel,
        grid=(n // tm,),
        in_specs=[row(D_MODEL), row(BRANCH_WIDTH), row(BRANCH_WIDTH), row(2 * D_MODEL),
                  full(BRANCH_WIDTH, D_MODEL), full(BRANCH_WIDTH, D_MODEL), full(D_MODEL, D_MODEL),
                  full(1, D_MODEL)],
        out_specs=(row(D_MODEL), row(D_MODEL)),
        out_shape=(jax.ShapeDtypeStruct((n, D_MODEL), jnp.float32),
                   jax.ShapeDtypeStruct((n, D_MODEL), MXU_DTYPE)),
        compiler_params=pltpu.CompilerParams(dimension_semantics=("arbitrary",),
                                             vmem_limit_bytes=VMEM_LIMIT_BYTES),
        name="merge",
    )(x2d, o_a, o_b, gl, w_branch[0].astype(MXU_DTYPE), w_branch[1].astype(MXU_DTYPE),
      w_out.astype(MXU_DTYPE), norm_ffn.astype(jnp.float32)[None, :])


FFN_TILE = D_FF // 2


def _ffn_kernel(hn_ref, x1_ref, wg_ref, wu_ref, wd_ref, y_ref, acc_ref):
    j = pl.program_id(1)

    @pl.when(j == 0)
    def _():
        acc_ref[...] = x1_ref[...]

    hn = hn_ref[...]
    g = jnp.dot(hn, wg_ref[...], preferred_element_type=jnp.float32)
    u = jnp.dot(hn, wu_ref[...], preferred_element_type=jnp.float32)
    acc_ref[...] += _mm(_silu(g) * u, wd_ref[...])

    @pl.when(j == pl.num_programs(1) - 1)
    def _():
        y_ref[...] = acc_ref[...]


def _ffn(hn, x1, w_gate_up, w_down, tm):
    n = hn.shape[0]
    tf = FFN_TILE
    assert n % tm == 0 and D_FF % tf == 0 and tf % LANES == 0
    nf = D_FF // tf
    wgu = w_gate_up.astype(MXU_DTYPE)
    return pl.pallas_call(
        _ffn_kernel,
        grid=(n // tm, nf),
        in_specs=[pl.BlockSpec((tm, D_MODEL), lambda i, j: (i, 0)),
                  pl.BlockSpec((tm, D_MODEL), lambda i, j: (i, 0)),
                  pl.BlockSpec((D_MODEL, tf), lambda i, j: (0, j)),
                  pl.BlockSpec((D_MODEL, tf), lambda i, j: (0, j + nf)),
                  pl.BlockSpec((tf, D_MODEL), lambda i, j: (j, 0))],
        out_specs=pl.BlockSpec((tm, D_MODEL), lambda i, j: (i, 0)),
        out_shape=jax.ShapeDtypeStruct((n, D_MODEL), jnp.float32),
        scratch_shapes=[pltpu.VMEM((tm, D_MODEL), jnp.float32)],
        compiler_params=pltpu.CompilerParams(dimension_semantics=("arbitrary", "arbitrary"),
                                             vmem_limit_bytes=VMEM_LIMIT_BYTES),
        name="ffn",
    )(hn, x1, wgu, wgu, w_down.astype(MXU_DTYPE))


IN_PROJ_TILE = K_TILE
MERGE_TILE = 512
FFN_ROW_TILE = 512


def _layer(x_p, x_s, cache_k, cache_v, cache_ik, conv_s, delta_s, page_table, norm_mix, w_in, q_norm, k_norm,
           w_conv, a_log, dt_bias, delta_norm, w_branch, w_out, norm_ffn, w_gate_up, w_down):
    b, t, d = x_p.shape
    db = x_s.shape[0]
    past = page_table.shape[1] * PAGE_SIZE
    kw = N_KV_A * HEAD_DIM_A
    w_packed = _pack_w_in(w_in)

    xp2 = x_p.reshape(b * t, d)
    tm = IN_PROJ_TILE
    assert t % tm == 0
    kv, ik, small, u, z, gl, qt, iqt, wt, kb, ikb, vt = _in_proj(
        xp2, _rope_tables(jnp.arange(t)), t // tm, tm, norm_mix, w_packed, q_norm, k_norm, key_major=True)
    o_a = _dsa_prompt(qt, iqt, wt, kb, ikb, vt, b, t)
    conv0 = jnp.zeros((b, CONV_K - 1, CONV_DIM), jnp.float32)
    delta0 = jnp.zeros((b, H_B, DK_B, DV_B), jnp.float32)
    o_b, conv_p, delta_p = _gdn_prompt(u, small, z, conv0, delta0, w_conv, a_log, dt_bias, delta_norm, b, t)
    x1, hn = _merge(xp2, o_a, o_b, gl, w_branch, w_out, norm_ffn, min(MERGE_TILE, b * t))
    y_p = _ffn(hn, x1, w_gate_up, w_down, min(FFN_ROW_TILE, b * t)).reshape(b, t, d)
    st_p = (kv[:, 0:kw].reshape(b, t, N_KV_A, HEAD_DIM_A), kv[:, kw:2 * kw].reshape(b, t, N_KV_A, HEAD_DIM_A),
            ik.reshape(b, t, IDX_DIM), conv_p, delta_p)

    xs2 = x_s.reshape(db, d)
    kv, ik, small, u, z, gl, qexp, iqhm = _in_proj(
        xs2, _rope_tables(jnp.full((db,), past, jnp.int32)), 1, db, norm_mix, w_packed, q_norm, k_norm,
        key_major=False)
    o_a = _dsa_sample(qexp, iqhm, small, ik, kv, cache_k, cache_v, cache_ik, page_table)
    o_b, conv_n, delta_n = _gdn_sample(u, small, z, conv_s, delta_s, w_conv, a_log, dt_bias, delta_norm)
    x1, hn = _merge(xs2, o_a, o_b, gl, w_branch, w_out, norm_ffn, db)
    y_s = _ffn(hn, x1, w_gate_up, w_down, db).reshape(db, 1, d)
    st_s = (kv[:, 0:kw].reshape(db, 1, N_KV_A, HEAD_DIM_A), kv[:, kw:2 * kw].reshape(db, 1, N_KV_A, HEAD_DIM_A),
            ik.reshape(db, 1, IDX_DIM), conv_n, delta_n)
    return y_p, y_s, st_p, st_s


def kernel(x_prompt, x_sample, cache_k, cache_v, cache_idx_k, state_conv, state_delta, page_table,
           norm_mix, w_in, q_norm, k_norm, w_conv, a_log, dt_bias, delta_norm, w_branch, w_out,
           norm_ffn, w_gate_up, w_down):
    assert x_sample.shape[1] == 1, "the sample group decodes one token per sequence"
    y_p, y_s = x_prompt, x_sample
    new_p, new_s = [], []
    for l in range(w_in.shape[0]):
        y_p, y_s, st_p, st_s = _layer(
            y_p, y_s, cache_k[l], cache_v[l], cache_idx_k[l], state_conv[l], state_delta[l], page_table,
            norm_mix[l], w_in[l], q_norm[l], k_norm[l], w_conv[l], a_log[l], dt_bias[l], delta_norm[l],
            w_branch[l], w_out[l], norm_ffn[l], w_gate_up[l], w_down[l])
        new_p.append(st_p)
        new_s.append(st_s)
    k_p, v_p, ik_p, conv_p, delta_p = [jnp.stack(a) for a in zip(*new_p)]
    k_s, v_s, ik_s, conv_s, delta_s = [jnp.stack(a) for a in zip(*new_s)]
    return (y_p, y_s, k_p, v_p, ik_p, conv_p, delta_p, k_s, v_s, ik_s, conv_s, delta_s)
```

```python
import functools
import math

import jax
import jax.numpy as jnp
import numpy as np
from jax import lax
from jax.experimental import pallas as pl
from jax.experimental.pallas import tpu as pltpu

D_MODEL = 1024
PAGE_SIZE = 128
N_HEADS_A = 8
N_KV_A = 2
HEAD_DIM_A = 64
GROUP_A = N_HEADS_A // N_KV_A
IDX_HEADS = 8
IDX_DIM = 64
TOPK_MAX = 256
ROPE_THETA = 500000.0
H_B = 4
DK_B = 128
DV_B = 128
CONV_K = 4
CONV_DIM = 2 * H_B * DK_B + H_B * DV_B
BRANCH_WIDTH = N_HEADS_A * HEAD_DIM_A
D_FF = -(-8 * D_MODEL // (3 * 256)) * 256
NORM_EPS = 1e-6
NEG_INF = -1e30
IN_SIZES = (N_HEADS_A * HEAD_DIM_A, N_KV_A * HEAD_DIM_A, N_KV_A * HEAD_DIM_A,
            IDX_HEADS * IDX_DIM, IDX_DIM, IDX_HEADS,
            CONV_DIM, H_B, H_B, H_B * DV_B, 2 * D_MODEL)

LANES = 128
SUBLANES = 8
VMEM_LIMIT_BYTES = 56 * 1024 * 1024

MXU_DTYPE = jnp.bfloat16

SMALL_W = LANES
SEG_A = BRANCH_WIDTH + 2 * N_KV_A * HEAD_DIM_A + IDX_HEADS * IDX_DIM + IDX_DIM
SEG_A_PAD = -(-SEG_A // LANES) * LANES
OFF_SMALL = SEG_A_PAD
OFF_U = OFF_SMALL + SMALL_W
OFF_Z = OFF_U + CONV_DIM
OFF_GL = OFF_Z + H_B * DV_B
D_IN_PACKED = OFF_GL + 2 * D_MODEL


def _mm(a, b):
    return jnp.dot(a.astype(MXU_DTYPE), b.astype(MXU_DTYPE), preferred_element_type=jnp.float32)


def _mm_nt(a, b):
    return lax.dot_general(a.astype(MXU_DTYPE), b.astype(MXU_DTYPE), (((1,), (1,)), ((), ())),
                           preferred_element_type=jnp.float32)


def _mm_tn(a, b):
    return lax.dot_general(a.astype(MXU_DTYPE), b.astype(MXU_DTYPE), (((0,), (0,)), ((), ())),
                           preferred_element_type=jnp.float32)


def _split3(x):
    x = x.astype(jnp.float32)
    h = x.astype(MXU_DTYPE)
    r = x - h.astype(jnp.float32)
    m = r.astype(MXU_DTYPE)
    l = (r - m.astype(jnp.float32)).astype(MXU_DTYPE)
    return h, m, l


def _sigmoid(x):
    return 1.0 / (1.0 + jnp.exp(-x))


def _silu(x):
    return x * _sigmoid(x)


def _softplus(x):
    return jnp.maximum(x, 0.0) + jnp.log(1.0 + jnp.exp(-jnp.abs(x)))


def _rope_tile(x, cos_t, sin_lo, sin_hi):
    half = HEAD_DIM_A // 8
    up = pltpu.roll(x, LANES - half, 1)
    dn = pltpu.roll(x, half, 1)
    return x * cos_t + up * sin_lo + dn * sin_hi


def _in_proj_kernel(key_major, x_ref, gain_ref, w_ref, bd_ref, qg_ref, kg_ref, cos_ref, slo_ref, shi_ref,
                    kv_ref, ik_ref, small_ref, u_ref, z_ref, gl_ref, *attn_refs):
    x = x_ref[...]
    ms = jnp.mean(x * x, axis=-1, keepdims=True)
    xn = (x * lax.rsqrt(ms + NORM_EPS) * gain_ref[...]).astype(MXU_DTYPE)

    cos_t, sin_lo, sin_hi = cos_ref[...], slo_ref[...], shi_ref[...]
    lane = lax.broadcasted_iota(jnp.int32, (x.shape[0], LANES), 1)
    lo_half = lane < HEAD_DIM_A

    def head_rms(t, gain):
        tt = t * t
        hi = tt.astype(MXU_DTYPE)
        lo = (tt - hi.astype(jnp.float32)).astype(MXU_DTYPE)
        bd = bd_ref[0:t.shape[1], 0:t.shape[1]]
        msq = (jnp.dot(hi, bd, preferred_element_type=jnp.float32)
               + jnp.dot(lo, bd, preferred_element_type=jnp.float32))
        return t * lax.rsqrt(msq + NORM_EPS) * gain

    q = jnp.dot(xn, w_ref[:, 0:BRANCH_WIDTH], preferred_element_type=jnp.float32)
    q = head_rms(q, qg_ref[...])
    if key_major:
        qt_ref, iqt_ref, wt_ref, kb_ref, ikb_ref, vt_ref = attn_refs
        n_qb = x.shape[0] // Q_TILE
        q_scale = HEAD_DIM_A ** -0.5 * math.log2(math.e)
    else:
        qexp_ref, iqhm_ref = attn_refs
        q_scale = HEAD_DIM_A ** -0.5
    for p in range(BRANCH_WIDTH // LANES):
        t = _rope_tile(q[:, p * LANES:(p + 1) * LANES], cos_t, sin_lo, sin_hi) * q_scale
        t_sw = pltpu.roll(t, HEAD_DIM_A, 1)
        for e in range(2):
            h = 2 * p + e
            n = h // GROUP_A
            src = t if e == n else t_sw
            keep = lo_half if n == 0 else jnp.logical_not(lo_half)
            qe = jnp.where(keep, src, 0.0)
            if key_major:
                qe_t = qe.T
                for j in range(n_qb):
                    qt_ref[j, :, h * Q_TILE:(h + 1) * Q_TILE] = qe_t[:, j * Q_TILE:(j + 1) * Q_TILE].astype(qt_ref.dtype)
            else:
                qexp_ref[h] = qe.astype(qexp_ref.dtype)

    c0 = BRANCH_WIDTH
    kw = N_KV_A * HEAD_DIM_A
    k = jnp.dot(xn, w_ref[:, c0:c0 + kw], preferred_element_type=jnp.float32)
    k = _rope_tile(head_rms(k, kg_ref[...]), cos_t, sin_lo, sin_hi)
    v = jnp.dot(xn, w_ref[:, c0 + kw:c0 + 2 * kw], preferred_element_type=jnp.float32)
    if key_major:
        v_t = v.T
        kv_ref[0, 0] = k.T
        kv_ref[0, 1] = v_t
        kb_ref[...] = k.astype(kb_ref.dtype)
        vt_ref[0, 0:kw, :] = v_t.astype(vt_ref.dtype)
        vt_ref[0, kw:kw + ONES_ROWS, :] = jnp.ones((ONES_ROWS, x.shape[0]), vt_ref.dtype)
    else:
        kv_ref[:, 0:kw] = k
        kv_ref[:, kw:2 * kw] = v

    c1 = c0 + 2 * kw
    iqw = IDX_HEADS * IDX_DIM
    iq = jnp.dot(xn, w_ref[:, c1:c1 + iqw], preferred_element_type=jnp.float32)
    for p in range(iqw // LANES):
        t = _rope_tile(iq[:, p * LANES:(p + 1) * LANES], cos_t, sin_lo, sin_hi)
        if key_major:
            t_t = t.T
            for e in range(2):
                h = 2 * p + e
                for j in range(n_qb):
                    iqt_ref[j, :, h * Q_TILE:(h + 1) * Q_TILE] = (
                        t_t[e * IDX_DIM:(e + 1) * IDX_DIM, j * Q_TILE:(j + 1) * Q_TILE].astype(iqt_ref.dtype))
        else:
            t = t.astype(iqhm_ref.dtype)
            iqhm_ref[2 * p] = t[:, 0:IDX_DIM]
            iqhm_ref[2 * p + 1] = t[:, IDX_DIM:2 * IDX_DIM]

    c2 = c1 + iqw
    ik_sm = jnp.dot(xn, w_ref[:, c2:c2 + 2 * LANES], preferred_element_type=jnp.float32)
    ik_tile = _rope_tile(ik_sm[:, 0:LANES], cos_t, sin_lo, sin_hi)
    ik = ik_tile[:, 0:IDX_DIM]
    small = ik_sm[:, LANES:2 * LANES]
    small_ref[...] = small
    if not key_major:
        ik_ref[...] = ik
    else:
        ik_ref[0] = ik_tile.T[0:IDX_DIM]
        ikb_ref[...] = ik.astype(ikb_ref.dtype)
        small_t = small.T
        for j in range(n_qb):
            wt_ref[j] = small_t[0:IDX_HEADS, j * Q_TILE:(j + 1) * Q_TILE]

    u_ref[...] = jnp.dot(xn, w_ref[:, OFF_U:OFF_U + CONV_DIM], preferred_element_type=jnp.float32)
    z_ref[...] = jnp.dot(xn, w_ref[:, OFF_Z:OFF_Z + H_B * DV_B], preferred_element_type=jnp.float32)
    gl_ref[...] = jnp.dot(xn, w_ref[:, OFF_GL:OFF_GL + 2 * D_MODEL], preferred_element_type=jnp.float32)


def _pack_w_in(w_in):
    pts = np.cumsum(IN_SIZES)[:-1].tolist()
    q, k, v, iq, ik, iw, u, a, b, z, gl = jnp.split(w_in, pts, axis=-1)
    d = w_in.shape[0]
    seg_a = jnp.concatenate([q, k, v, iq, ik, jnp.zeros((d, SEG_A_PAD - SEG_A), w_in.dtype)], axis=1)
    small = jnp.concatenate([iw, a, b, jnp.zeros((d, SMALL_W - IDX_HEADS - 2 * H_B), w_in.dtype)], axis=1)
    return jnp.concatenate([seg_a, small, u, z, gl], axis=1).astype(MXU_DTYPE)


def _rope_tables(pos):
    rot = HEAD_DIM_A // 4
    half = rot // 2
    inv_freq = ROPE_THETA ** (-jnp.arange(half, dtype=jnp.float32) / half)
    ang = pos.astype(jnp.float32)[:, None] * inv_freq[None, :]
    cos, sin = jnp.cos(ang), jnp.sin(ang)
    rows = pos.shape[0]
    one = jnp.ones((rows, HEAD_DIM_A - rot), jnp.float32)
    zero = jnp.zeros((rows, HEAD_DIM_A - rot), jnp.float32)
    zh = jnp.zeros((rows, half), jnp.float32)
    cos_h = jnp.concatenate([cos, cos, one], axis=1)
    slo_h = jnp.concatenate([-sin, zh, zero], axis=1)
    shi_h = jnp.concatenate([zh, sin, zero], axis=1)
    rep = LANES // HEAD_DIM_A
    return jnp.tile(cos_h, (1, rep)), jnp.tile(slo_h, (1, rep)), jnp.tile(shi_h, (1, rep))


def _in_proj(x2d, pos_tables, n_table_blocks, tm, norm_mix, w_packed, q_norm, k_norm, key_major):
    n = x2d.shape[0]
    assert n % tm == 0 and (not key_major or tm == K_TILE)
    cos_t, sin_lo, sin_hi = pos_tables
    bd = jnp.kron(jnp.eye(BRANCH_WIDTH // HEAD_DIM_A, dtype=jnp.float32),
                  jnp.full((HEAD_DIM_A, HEAD_DIM_A), 1.0 / HEAD_DIM_A, jnp.float32)).astype(MXU_DTYPE)
    qg = jnp.tile(q_norm.astype(jnp.float32), BRANCH_WIDTH // HEAD_DIM_A)[None, :]
    kg = jnp.tile(k_norm.astype(jnp.float32), N_KV_A)[None, :]
    kw = N_KV_A * HEAD_DIM_A
    row = lambda w: pl.BlockSpec((tm, w), lambda i: (i, 0))
    full = lambda a: pl.BlockSpec(a.shape, lambda i: (0,) * a.ndim)
    tab = pl.BlockSpec((tm, LANES), lambda i: (i % n_table_blocks, 0))
    if key_major:
        seq = n_table_blocks * tm
        nt = n_table_blocks
        kv_shape, ik_shape = (n // seq, 2, kw, seq), (n // seq, IDX_DIM, seq)
        kv_spec = pl.BlockSpec((1, 2, kw, tm), lambda i: (i // nt, 0, 0, i % nt))
        ik_spec = pl.BlockSpec((1, IDX_DIM, tm), lambda i: (i // nt, 0, i % nt))
    else:
        kv_shape, ik_shape = (n, 2 * kw), (n, IDX_DIM)
        kv_spec, ik_spec = row(2 * kw), row(IDX_DIM)
    out_shape = [
        jax.ShapeDtypeStruct(kv_shape, jnp.float32),
        jax.ShapeDtypeStruct(ik_shape, jnp.float32),
        jax.ShapeDtypeStruct((n, SMALL_W), jnp.float32),
        jax.ShapeDtypeStruct((n, CONV_DIM), jnp.float32),
        jax.ShapeDtypeStruct((n, H_B * DV_B), jnp.float32),
        jax.ShapeDtypeStruct((n, 2 * D_MODEL), jnp.float32),
    ]
    out_specs = [kv_spec, ik_spec, row(SMALL_W), row(CONV_DIM), row(H_B * DV_B), row(2 * D_MODEL)]
    if key_major:
        n_qb = tm // Q_TILE
        blk = lambda r, w: pl.BlockSpec((n_qb, r, w), lambda i: (i, 0, 0))
        out_shape += [
            jax.ShapeDtypeStruct((n // Q_TILE, LANES, N_HEADS_A * Q_TILE), MXU_DTYPE),
            jax.ShapeDtypeStruct((n // Q_TILE, IDX_DIM, IDX_HEADS * Q_TILE), MXU_DTYPE),
            jax.ShapeDtypeStruct((n // Q_TILE, IDX_HEADS, Q_TILE), jnp.float32),
            jax.ShapeDtypeStruct((n, kw), MXU_DTYPE),
            jax.ShapeDtypeStruct((n, IDX_DIM), MXU_DTYPE),
            jax.ShapeDtypeStruct((n // K_TILE, kw + ONES_ROWS, K_TILE), MXU_DTYPE),
        ]
        out_specs += [blk(LANES, N_HEADS_A * Q_TILE), blk(IDX_DIM, IDX_HEADS * Q_TILE), blk(IDX_HEADS, Q_TILE),
                      row(kw), row(IDX_DIM), pl.BlockSpec((1, kw + ONES_ROWS, K_TILE), lambda i: (i, 0, 0))]
    else:
        out_shape += [jax.ShapeDtypeStruct((N_HEADS_A, n, LANES), MXU_DTYPE),
                      jax.ShapeDtypeStruct((IDX_HEADS, n, IDX_DIM), MXU_DTYPE)]
        out_specs += [pl.BlockSpec((N_HEADS_A, tm, LANES), lambda i: (0, i, 0)),
                      pl.BlockSpec((IDX_HEADS, tm, IDX_DIM), lambda i: (0, i, 0))]
    return pl.pallas_call(
        functools.partial(_in_proj_kernel, key_major),
        grid=(n // tm,),
        in_specs=[row(D_MODEL), full(norm_mix[None, :]), full(w_packed), full(bd), full(qg), full(kg),
                  tab, tab, tab],
        out_specs=tuple(out_specs),
        out_shape=tuple(out_shape),
        compiler_params=pltpu.CompilerParams(dimension_semantics=("arbitrary",),
                                             vmem_limit_bytes=VMEM_LIMIT_BYTES),
        name="in_proj",
    )(x2d, norm_mix[None, :].astype(jnp.float32), w_packed, bd, qg, kg, cos_t, sin_lo, sin_hi)


_INT_MAG = 0x7FFFFFFF


def _f32_key(x):
    b = lax.bitcast_convert_type(x, jnp.int32)
    return b ^ (lax.shift_right_arithmetic(b, 31) & _INT_MAG)


def _key_f32(k):
    b = k ^ (lax.shift_right_arithmetic(k, 31) & _INT_MAG)
    return lax.bitcast_convert_type(b, jnp.float32)


def _topk_threshold(count_ge, count_tie, row_min, row_max, n_adm, topk, n_keys, fixed_steps=18, linear_steps=24):
    kf = jnp.float32(topk)
    need = n_adm > topk
    lo_k = _f32_key(row_min)
    hi_k = _f32_key(row_max) + 1
    thr = jnp.where(need, row_min, -jnp.inf)
    state = (lo_k, hi_k, thr, jnp.zeros_like(row_min), jnp.where(need, 0, 1).astype(jnp.int32),
             jnp.zeros_like(lo_k))

    def step(linear, st):
        lo_k, hi_k, thr, cnt_hi, done, tie = st
        adjacent = hi_k == lo_k + 1
        lo_f, hi_f = _key_f32(lo_k), _key_f32(hi_k)
        mid_lin = _f32_key(lo_f + 0.5 * (hi_f - lo_f))
        mid_lin = jnp.minimum(jnp.maximum(mid_lin, lo_k + 1), hi_k - 1)
        mid_int = (lo_k & hi_k) + lax.shift_right_arithmetic(lo_k ^ hi_k, 1)
        mid = mid_lin if linear is True else jnp.where(linear, mid_lin, mid_int)
        mid_f = _key_f32(mid)
        cnt = count_ge(mid_f)
        live = jnp.logical_and(done == 0, jnp.logical_not(adjacent))
        hit = jnp.logical_and(live, cnt == kf)
        up = jnp.logical_and(live, cnt > kf)
        dn = jnp.logical_and(live, cnt < kf)
        new_tie = jnp.logical_and(done == 0, adjacent)
        thr = jnp.where(hit, mid_f, jnp.where(new_tie, lo_f, thr))
        tie = jnp.where(new_tie, 1, tie)
        done = jnp.where(jnp.logical_or(hit, new_tie), 1, done)
        lo_k = jnp.where(up, mid, lo_k)
        hi_k = jnp.where(dn, mid, hi_k)
        cnt_hi = jnp.where(dn, cnt, cnt_hi)
        return (lo_k, hi_k, thr, cnt_hi, done, tie)

    state = lax.fori_loop(0, fixed_steps, lambda _, st: step(True, st), state)

    def cond(st):
        it, active = st[0], st[1]
        return jnp.logical_and(it < 80, active > 0)

    def body(st):
        it = st[0]
        new = step(it < linear_steps, st[2:])
        return (it + 1, jnp.max(1 - new[4])) + new

    st = lax.while_loop(cond, body, (jnp.int32(fixed_steps), jnp.max(1 - state[4])) + state)
    thr, cnt_hi, tie = st[4], st[5], st[7]

    need_ties = kf - cnt_hi
    n_bits = max(1, int(math.ceil(math.log2(n_keys + 1))))
    any_tie = jnp.max(tie)

    def tie_body(_, lm):
        lo_m, hi_m = lm
        mid = lax.shift_right_arithmetic(lo_m + hi_m, 1)
        ge = count_tie(thr, mid) >= need_ties
        return jnp.where(ge, lo_m, mid), jnp.where(ge, mid, hi_m)

    lo_m0 = jnp.zeros_like(lo_k)
    hi_m0 = jnp.full_like(lo_k, n_keys)
    _, hi_m = lax.fori_loop(0, jnp.where(any_tie > 0, n_bits + 1, 0), tie_body, (lo_m0, hi_m0))
    cut = jnp.where(tie > 0, hi_m, n_keys + 1)
    return thr, cut


Q_TILE = 128
K_TILE = 256
K_UNROLL = 4
SCAN_UNROLL = 4
ONES_ROWS = 16


def _dsa_prompt_kernel(topk, qt_ref, iqt_ref, wt_ref, kb_ref, ikb_ref, vt_ref, o_ref, sc_ref, acc_ref, m_ref):
    i = pl.program_id(1)
    tq, kc = Q_TILE, K_TILE
    n_keys = sc_ref.shape[0] * kc
    nchunk = (i + 2) // 2
    qpos = i * tq + lax.broadcasted_iota(jnp.int32, (kc, tq), 1)
    krow = lax.broadcasted_iota(jnp.int32, (kc, tq), 0)
    qpos8 = qpos[0:SUBLANES]

    def col_reduce(x, op):
        return op(x.reshape(kc // SUBLANES, SUBLANES, tq), axis=0)

    def all_rows(x, op2):
        for shift in (4, 2, 1):
            x = op2(x, pltpu.roll(x, shift, 0))
        return x

    def tile_loop(first, rest, init, unroll):
        def trip(t, carry):
            heads = [first(t * unroll + sub) for sub in range(unroll)]
            for sub in range(unroll):
                carry = rest(t * unroll + sub, heads[sub], carry)
            return carry
        full = nchunk // unroll
        carry = lax.fori_loop(0, full, trip, init)
        return lax.fori_loop(full * unroll, nchunk, lambda c, carry: rest(c, first(c), carry), carry)

    def key_rows(c):
        return pl.ds(pl.multiple_of(c * kc, kc), kc)

    w = wt_ref[0]
    iqt = iqt_ref[0]
    s_scale = IDX_DIM ** -0.5 * IDX_HEADS ** -0.5

    def score_dots(c):
        return jnp.dot(ikb_ref[key_rows(c), :], iqt, preferred_element_type=jnp.float32)

    def score_tile(c, d, carry):
        s = w[0:1] * jnp.maximum(d[:, 0:tq], 0.0)
        for h in range(1, IDX_HEADS):
            s = s + w[h:h + 1] * jnp.maximum(d[:, h * tq:(h + 1) * tq], 0.0)
        sc_ref[c] = jnp.where(c * kc + krow <= qpos, s * s_scale, NEG_INF)
        return carry

    tile_loop(score_dots, score_tile, 0, K_UNROLL)

    def scan(body, init):
        return tile_loop(lambda c: sc_ref[c], body, init, SCAN_UNROLL)

    def count(pred):
        def body(c, s, acc):
            return acc + col_reduce(jnp.where(pred(s, c * kc + krow), 1.0, 0.0), jnp.sum)
        return all_rows(scan(body, jnp.zeros((SUBLANES, tq), jnp.float32)), jnp.add)

    def count_ge(c):
        return count(lambda s, kpos: s >= c[0:1])

    def count_tie(v, m):
        return count(lambda s, kpos: jnp.logical_and(s == v[0:1], kpos < m[0:1]))

    def minmax(c, s, mm):
        adm = c * kc + krow <= qpos
        return (jnp.minimum(mm[0], col_reduce(jnp.where(adm, s, jnp.inf), jnp.min)),
                jnp.maximum(mm[1], col_reduce(jnp.where(adm, s, -jnp.inf), jnp.max)))

    mn, mx = scan(minmax, (jnp.full((SUBLANES, tq), jnp.inf, jnp.float32),
                           jnp.full((SUBLANES, tq), -jnp.inf, jnp.float32)))
    thr, cut = _topk_threshold(count_ge, count_tie, all_rows(mn, jnp.minimum), all_rows(mx, jnp.maximum),
                               qpos8 + 1, topk, n_keys)
    thr_row, cut_row = thr[0:1], cut[0:1]

    m_ref[...] = jnp.full(m_ref.shape, 0.5 * NEG_INF, jnp.float32)
    acc_ref[...] = jnp.zeros(acc_ref.shape, jnp.float32)
    qt = qt_ref[0]
    kw = N_KV_A * HEAD_DIM_A

    def logits(c):
        return jnp.dot(kb_ref[key_rows(c), :], qt, preferred_element_type=jnp.float32)

    def attend_tile(c, lg, carry):
        s = sc_ref[c]
        kpos = c * kc + krow
        sel = jnp.logical_or(s > thr_row, jnp.logical_and(s == thr_row, kpos < cut_row))
        bias = jnp.where(jnp.logical_and(sel, kpos <= qpos), 0.0, NEG_INF)
        ps, alphas = [], []
        for h in range(N_HEADS_A):
            cols = slice(h * tq, (h + 1) * tq)
            lgh = lg[:, cols] + bias
            m_old = m_ref[:, cols]
            m_new = jnp.maximum(m_old, all_rows(col_reduce(lgh, jnp.max), jnp.maximum))
            alphas.append(jnp.exp2(m_old - m_new)[0:1])
            ps.append(jnp.exp2(lgh - m_new[0:1]).astype(MXU_DTYPE))
            m_ref[:, cols] = m_new
        pv = jnp.dot(vt_ref[c], jnp.concatenate(ps, axis=1), preferred_element_type=jnp.float32)
        acc_ref[...] = acc_ref[...] * jnp.concatenate(alphas, axis=1) + pv
        return carry

    tile_loop(logits, attend_tile, 0, K_UNROLL)

    acc = acc_ref[...]
    o_t = acc[0:kw] / acc[kw:kw + 1]
    for p in range(N_HEADS_A // 2):
        n = (2 * p) // GROUP_A
        pair = jnp.concatenate([o_t[n * HEAD_DIM_A:(n + 1) * HEAD_DIM_A, (2 * p + e) * tq:(2 * p + e + 1) * tq]
                                for e in range(2)], axis=0)
        o_ref[:, p * LANES:(p + 1) * LANES] = pair.T.astype(o_ref.dtype)


def _dsa_prompt(qt, iqt, wt, kb, ikb, vt, batch, seq):
    tq, kc = Q_TILE, K_TILE
    assert seq % kc == 0
    nq = seq // tq
    nk = seq // kc
    n = batch * seq
    kw = N_KV_A * HEAD_DIM_A
    topk = min(TOPK_MAX, seq // 4)
    return pl.pallas_call(
        functools.partial(_dsa_prompt_kernel, topk),
        grid=(batch, nq),
        in_specs=[
            pl.BlockSpec((1, LANES, N_HEADS_A * tq), lambda b, i: (b * nq + i, 0, 0)),
            pl.BlockSpec((1, IDX_DIM, IDX_HEADS * tq), lambda b, i: (b * nq + i, 0, 0)),
            pl.BlockSpec((1, IDX_HEADS, tq), lambda b, i: (b * nq + i, 0, 0)),
            pl.BlockSpec((seq, kw), lambda b, i: (b, 0)),
            pl.BlockSpec((seq, IDX_DIM), lambda b, i: (b, 0)),
            pl.BlockSpec((nk, kw + ONES_ROWS, kc), lambda b, i: (b, 0, 0)),
        ],
        out_specs=pl.BlockSpec((tq, BRANCH_WIDTH), lambda b, i: (b * nq + i, 0)),
        out_shape=jax.ShapeDtypeStruct((n, BRANCH_WIDTH), MXU_DTYPE),
        scratch_shapes=[
            pltpu.VMEM((nk, kc, tq), jnp.float32),
            pltpu.VMEM((kw + ONES_ROWS, N_HEADS_A * tq), jnp.float32),
            pltpu.VMEM((SUBLANES, N_HEADS_A * tq), jnp.float32),
        ],
        compiler_params=pltpu.CompilerParams(dimension_semantics=("arbitrary", "arbitrary"),
                                             vmem_limit_bytes=VMEM_LIMIT_BYTES),
        name="dsa_prompt",
    )(qt, iqt, wt, kb, ikb, vt)


def _page_copy(pt_ref, cache_ref, buf_ref, sem_ref, seq, page, slot):
    lanes = pl.ds(page * PAGE_SIZE, PAGE_SIZE)
    return pltpu.make_async_copy(cache_ref.at[pt_ref[seq, page]], buf_ref.at[slot, :, lanes], sem_ref.at[slot])


def _pages_start(pt_ref, cache_ref, buf_ref, sem_ref, seq, slot, n_pages):
    for p in range(n_pages):
        _page_copy(pt_ref, cache_ref, buf_ref, sem_ref, seq, p, slot).start()


def _pages_wait(pt_ref, cache_ref, buf_ref, sem_ref, seq, slot, n_pages):
    for p in range(n_pages):
        _page_copy(pt_ref, cache_ref, buf_ref, sem_ref, seq, p, slot).wait()


def _dsa_sample_score_kernel(topk, n_pages, pt_ref, iq_ref, iw_ref, iknew_ref, cache_ik_ref,
                             sc_ref, thr_ref, cut_ref, ikbuf_ref, sem_ref):
    s = pl.program_id(0)
    n_seq = pl.num_programs(0)
    past = n_pages * PAGE_SIZE
    n_tiles = sc_ref.shape[0]
    slot = s % 2

    @pl.when(s == 0)
    def _():
        _pages_start(pt_ref, cache_ik_ref, ikbuf_ref, sem_ref, 0, 0, n_pages)

    @pl.when(s + 1 < n_seq)
    def _():
        _pages_start(pt_ref, cache_ik_ref, ikbuf_ref, sem_ref, s + 1, 1 - slot, n_pages)

    _pages_wait(pt_ref, cache_ik_ref, ikbuf_ref, sem_ref, s, slot, n_pages)

    iq = iq_ref[0]
    w = iw_ref[0]
    s_scale = IDX_DIM ** -0.5 * IDX_HEADS ** -0.5
    d = _mm(iq, ikbuf_ref[slot])
    srow = jnp.sum(w * jnp.maximum(d, 0.0), axis=0, keepdims=True) * s_scale
    for j in range(n_pages):
        sc_ref[j, pl.ds(s, 1), :] = srow[:, j * LANES:(j + 1) * LANES]
    ik_new = iknew_ref[0].astype(MXU_DTYPE).astype(jnp.float32)
    d_self = jnp.sum(iq.astype(jnp.float32) * ik_new, axis=1, keepdims=True)
    s_self = jnp.sum(w * jnp.maximum(d_self, 0.0), axis=0, keepdims=True) * s_scale
    lane1 = lax.broadcasted_iota(jnp.int32, (1, LANES), 1)
    sc_ref[n_tiles - 1, pl.ds(s, 1), :] = jnp.where(lane1 == 0, s_self, NEG_INF)

    @pl.when(s == n_seq - 1)
    def _():
        rows = sc_ref.shape[1]
        lane = lax.broadcasted_iota(jnp.int32, (rows, LANES), 1)

        def count(pred):
            def body(j, acc):
                return acc + jnp.where(pred(sc_ref[j], j * LANES + lane), 1.0, 0.0)
            acc = lax.fori_loop(0, n_tiles, body, jnp.zeros((rows, LANES), jnp.float32))
            return jnp.broadcast_to(jnp.sum(acc, axis=1, keepdims=True), (rows, LANES))

        def count_ge(c):
            return count(lambda t, kpos: t >= c)

        def count_tie(v, m):
            return count(lambda t, kpos: jnp.logical_and(t == v, kpos < m))

        def minmax(j, mm):
            t = sc_ref[j]
            adm = j * LANES + lane <= past
            return (jnp.minimum(mm[0], jnp.where(adm, t, jnp.inf)),
                    jnp.maximum(mm[1], jnp.where(adm, t, -jnp.inf)))

        mn, mx = lax.fori_loop(0, n_tiles, minmax, (jnp.full((rows, LANES), jnp.inf, jnp.float32),
                                                    jnp.full((rows, LANES), -jnp.inf, jnp.float32)))
        row_min = jnp.broadcast_to(jnp.min(mn, axis=1, keepdims=True), (rows, LANES))
        row_max = jnp.broadcast_to(jnp.max(mx, axis=1, keepdims=True), (rows, LANES))
        n_adm = jnp.full((rows, LANES), past + 1, jnp.int32)
        thr, cut = _topk_threshold(count_ge, count_tie, row_min, row_max, n_adm, topk, n_tiles * LANES)
        thr_ref[...] = thr
        cut_ref[...] = cut


def _dsa_sample_attend_kernel(n_pages, pt_ref, q_ref, sc_ref, thr_ref, cut_ref, kvnew_ref,
                              cache_k_ref, cache_v_ref, o_ref, kbuf_ref, vbuf_ref, ksem_ref, vsem_ref):
    s = pl.program_id(0)
    n_seq = pl.num_programs(0)
    past = n_pages * PAGE_SIZE
    slot = s % 2

    def start(seq, sl):
        _pages_start(pt_ref, cache_k_ref, kbuf_ref, ksem_ref, seq, sl, n_pages)
        _pages_start(pt_ref, cache_v_ref, vbuf_ref, vsem_ref, seq, sl, n_pages)

    @pl.when(s == 0)
    def _():
        start(0, 0)

    @pl.when(s + 1 < n_seq)
    def _():
        start(s + 1, 1 - slot)

    _pages_wait(pt_ref, cache_k_ref, kbuf_ref, ksem_ref, s, slot, n_pages)
    _pages_wait(pt_ref, cache_v_ref, vbuf_ref, vsem_ref, s, slot, n_pages)

    q = q_ref[0]
    thr = thr_ref[0][:, 0:1]
    cut = cut_ref[0][:, 0:1]
    kw = N_KV_A * HEAD_DIM_A
    k_new = kvnew_ref[0][:, 0:kw].astype(MXU_DTYPE).astype(jnp.float32)
    v_new = kvnew_ref[0][:, kw:2 * kw].astype(MXU_DTYPE).astype(jnp.float32)
    sc = sc_ref[0]

    def selected(srow, kpos):
        return jnp.logical_or(srow > thr, jnp.logical_and(srow == thr, kpos < cut))

    kpos = lax.broadcasted_iota(jnp.int32, (1, past), 1)
    bias = jnp.where(selected(sc[:, 0:past], kpos), 0.0, NEG_INF)
    lg = _mm(q, kbuf_ref[slot]) + bias
    lg_self = jnp.sum(q.astype(jnp.float32) * k_new, axis=1, keepdims=True)
    lg_self = jnp.where(selected(sc[:, past:past + 1], past), lg_self, NEG_INF)
    m = jnp.maximum(jnp.max(lg, axis=1, keepdims=True), lg_self)
    p = jnp.exp(lg - m)
    p_self = jnp.exp(lg_self - m)
    denom = jnp.sum(p, axis=1, keepdims=True) + p_self
    o = (_mm_nt(p, vbuf_ref[slot]) + p_self * v_new) / denom
    parts = []
    for h in range(N_HEADS_A):
        n = h // GROUP_A
        parts.append(o[h:h + 1, n * HEAD_DIM_A:(n + 1) * HEAD_DIM_A])
    o_ref[0] = jnp.concatenate(parts, axis=1)


def _dsa_sample(qexp, iqhm, small, ik_new, kv_new, cache_k, cache_v, cache_ik, page_table):
    db, n_pages = page_table.shape
    past = n_pages * PAGE_SIZE
    n_pool = cache_ik.shape[0]
    topk = min(TOPK_MAX, (past + 1) // 4)
    n_tiles = n_pages + 1
    kw = N_KV_A * HEAD_DIM_A
    q_s = jnp.swapaxes(qexp, 0, 1)
    iq_s = jnp.swapaxes(iqhm, 0, 1)
    iw_s = small[:, 0:IDX_HEADS].reshape(db, IDX_HEADS, 1)
    ck_t = jnp.transpose(cache_k, (0, 2, 3, 1)).reshape(n_pool, kw, PAGE_SIZE)
    cv_t = jnp.transpose(cache_v, (0, 2, 3, 1)).reshape(n_pool, kw, PAGE_SIZE)
    cik_t = jnp.swapaxes(cache_ik, 1, 2)
    cparams = pltpu.CompilerParams(dimension_semantics=("arbitrary",), vmem_limit_bytes=VMEM_LIMIT_BYTES)
    per_seq = lambda *shape: pl.BlockSpec((1,) + shape, lambda s, pt: (s,) + (0,) * len(shape))
    whole = lambda *shape: pl.BlockSpec(shape, lambda s, pt: (0,) * len(shape))
    any_spec = pl.BlockSpec(memory_space=pl.ANY)

    sc, thr, cut = pl.pallas_call(
        functools.partial(_dsa_sample_score_kernel, topk, n_pages),
        grid_spec=pltpu.PrefetchScalarGridSpec(
            num_scalar_prefetch=1,
            grid=(db,),
            in_specs=[per_seq(IDX_HEADS, IDX_DIM), per_seq(IDX_HEADS, 1), per_seq(1, IDX_DIM), any_spec],
            out_specs=(whole(n_tiles, db, LANES), whole(db, LANES), whole(db, LANES)),
            scratch_shapes=[pltpu.VMEM((2, IDX_DIM, past), jnp.float32), pltpu.SemaphoreType.DMA((2,))],
        ),
        out_shape=(jax.ShapeDtypeStruct((n_tiles, db, LANES), jnp.float32),
                   jax.ShapeDtypeStruct((db, LANES), jnp.float32),
                   jax.ShapeDtypeStruct((db, LANES), jnp.int32)),
        compiler_params=cparams,
        name="dsa_sample_score",
    )(page_table, iq_s, iw_s, ik_new.reshape(db, 1, IDX_DIM), cik_t)

    o = pl.pallas_call(
        functools.partial(_dsa_sample_attend_kernel, n_pages),
        grid_spec=pltpu.PrefetchScalarGridSpec(
            num_scalar_prefetch=1,
            grid=(db,),
            in_specs=[per_seq(N_HEADS_A, LANES), per_seq(1, n_tiles * LANES), per_seq(1, LANES), per_seq(1, LANES),
                      per_seq(1, 2 * kw), any_spec, any_spec],
            out_specs=per_seq(1, BRANCH_WIDTH),
            scratch_shapes=[pltpu.VMEM((2, kw, past), jnp.float32), pltpu.VMEM((2, kw, past), jnp.float32),
                            pltpu.SemaphoreType.DMA((2,)), pltpu.SemaphoreType.DMA((2,))],
        ),
        out_shape=jax.ShapeDtypeStruct((db, 1, BRANCH_WIDTH), jnp.float32),
        compiler_params=cparams,
        name="dsa_sample_attend",
    )(page_table, q_s, jnp.swapaxes(sc, 0, 1).reshape(db, 1, n_tiles * LANES),
      thr.reshape(db, 1, LANES), cut.reshape(db, 1, LANES),
      kv_new.reshape(db, 1, 2 * kw), ck_t, cv_t)
    return o.reshape(db, BRANCH_WIDTH)


GDN_CHUNK = 128
A_LANE = IDX_HEADS
B_LANE = IDX_HEADS + H_B


def _split2(x):
    h = x.astype(MXU_DTYPE)
    return h, (x - h.astype(jnp.float32)).astype(MXU_DTYPE)


def _mm2(a, b):
    a1, a2 = _split2(a)
    b1, b2 = _split2(b)
    d = functools.partial(jnp.dot, preferred_element_type=jnp.float32)
    return d(a1, b1) + (d(a1, b2) + d(a2, b1))


def _unit_lower_inverses(mats):
    n = mats[0].shape[0]
    eye = (lax.broadcasted_iota(jnp.int32, (n, n), 0) == lax.broadcasted_iota(jnp.int32, (n, n), 1))
    ss = [jnp.where(eye, 1.0, 0.0) - a for a in mats]
    ps = [_mm2(a, a) for a in mats]
    k = 2
    while k < n:
        ss = [s + _mm2(s, p) for s, p in zip(ss, ps)]
        k *= 2
        if k < n:
            ps = [_mm2(p, p) for p in ps]
    return ss


def _l2norm(x):
    return x * lax.rsqrt(jnp.sum(x * x, axis=-1, keepdims=True) + NORM_EPS)


def _gdn_prompt_kernel(u_ref, small_ref, z_ref, conv0_ref, s0_ref, wconv_ref, alog_ref, dtb_ref, dnorm_ref,
                       o_ref, conv_out_ref, s_out_ref, ucat_ref, state_ref):
    n = pl.program_id(0)
    nb = u_ref.shape[0]
    c = GDN_CHUNK
    head = SUBLANES
    tail = CONV_K - 1

    @pl.when(n == 0)
    def _():
        ucat_ref[:, head - tail:head, :] = conv0_ref[...]
        state_ref[...] = s0_ref[...]

    row = lax.broadcasted_iota(jnp.int32, (c, c), 0)
    col = lax.broadcasted_iota(jnp.int32, (c, c), 1)
    lower = row >= col
    strict = row > col
    ltri = jnp.where(lower, 1.0, 0.0).astype(MXU_DTYPE)
    d = functools.partial(jnp.dot, preferred_element_type=jnp.float32)

    chains = [(b, h) for b in range(nb) for h in range(H_B)]
    qw = H_B * DK_B
    qs, ks, vs, betas, gcols = [], [], [], [], []
    for b in range(nb):
        ucat_ref[b, head:head + c, :] = u_ref[b]
        y = wconv_ref[tail:tail + 1, :] * ucat_ref[b, head:head + c, :]
        for j in range(tail):
            y = y + wconv_ref[j:j + 1, :] * ucat_ref[b, head - tail + j:head - tail + j + c, :]
        cv = _silu(y)
        carry_rows = ucat_ref[b, head + c - tail:head + c, :]
        ucat_ref[b, head - tail:head, :] = carry_rows
        conv_out_ref[b] = carry_rows
        sm = small_ref[b]
        g_all = -jnp.exp(alog_ref[...]) * _softplus(sm + dtb_ref[...])
        beta_all = _sigmoid(sm)
        a1, a2, a3 = _split3(g_all)
        gc_all = d(ltri, a1) + (d(ltri, a2) + d(ltri, a3))
        for h in range(H_B):
            qs.append(_l2norm(cv[:, h * DK_B:(h + 1) * DK_B]) * (DK_B ** -0.5))
            ks.append(_l2norm(cv[:, qw + h * DK_B:qw + (h + 1) * DK_B]))
            vs.append(cv[:, 2 * qw + h * DV_B:2 * qw + (h + 1) * DV_B])
            betas.append(jnp.broadcast_to(beta_all[:, B_LANE + h:B_LANE + h + 1], (c, LANES)))
            gcols.append(jnp.broadcast_to(gc_all[:, A_LANE + h:A_LANE + h + 1], (c, c)))
    decays = [jnp.where(lower, jnp.exp(jnp.where(lower, g - g.T, 0.0)), 0.0) for g in gcols]
    egs = [jnp.exp(g) for g in gcols]
    g_lasts = [g[c - 1:c, :] for g in gcols]
    kbs = [k * b for k, b in zip(ks, betas)]
    vbs = [v * b for v, b in zip(vs, betas)]
    kks = [_mm_nt(kb, k) for kb, k in zip(kbs, ks)]
    qks = [_mm_nt(q, k) for q, k in zip(qs, ks)]
    t_invs = _unit_lower_inverses([jnp.where(strict, kk * dc, 0.0) for kk, dc in zip(kks, decays)])
    sols = [_mm2(t, jnp.concatenate([vb, kb * eg], axis=1)) for t, vb, kb, eg in zip(t_invs, vbs, kbs, egs)]
    s_olds = [state_ref[b, h] for b, h in chains]
    v_news = [sol[:, 0:DV_B] - _mm(sol[:, DV_B:DV_B + DK_B], s) for sol, s in zip(sols, s_olds)]
    o_hs = [_mm(q * eg, s) + _mm(qk * dc, v_new)
            for q, eg, s, qk, dc, v_new in zip(qs, egs, s_olds, qks, decays, v_news)]
    for i, (b, h) in enumerate(chains):
        k_dec = ks[i] * jnp.exp(g_lasts[i] - gcols[i])
        state_ref[b, h] = s_olds[i] * jnp.exp(g_lasts[i]) + _mm_tn(k_dec, v_news[i])
    for i, (b, h) in enumerate(chains):
        o_h = o_hs[i]
        ms = jnp.mean(o_h * o_h, axis=-1, keepdims=True)
        o_n = o_h * lax.rsqrt(ms + NORM_EPS) * dnorm_ref[...]
        gate = _silu(z_ref[b, :, h * DV_B:(h + 1) * DV_B])
        o_ref[b, :, h * DV_B:(h + 1) * DV_B] = (o_n * gate).astype(o_ref.dtype)

    @pl.when(n == pl.num_programs(0) - 1)
    def _():
        s_out_ref[...] = state_ref[...]


GDN_SEQ_TILE = 8


def _gdn_sample_kernel(u_ref, cb_ref, small_ref, z_ref, s0_ref, wconv_ref, alog_ref, dtb_ref, dnorm_ref,
                       o_ref, conv_out_ref, s_out_ref):
    ts = GDN_SEQ_TILE
    tail = CONV_K - 1
    u_new = u_ref[...]
    y = wconv_ref[tail:tail + 1, :] * u_new
    for j in range(tail):
        y = y + wconv_ref[j:j + 1, :] * cb_ref[j]
    cv = _silu(y)
    for j in range(tail - 1):
        conv_out_ref[j] = cb_ref[j + 1]
    conv_out_ref[tail - 1] = u_new

    sm = small_ref[...]
    eg_all = jnp.exp(-jnp.exp(alog_ref[...]) * _softplus(sm + dtb_ref[...]))
    beta_all = _sigmoid(sm)
    qw = H_B * DK_B
    for h in range(H_B):
        q = _l2norm(cv[:, h * DK_B:(h + 1) * DK_B]) * (DK_B ** -0.5)
        k = _l2norm(cv[:, qw + h * DK_B:qw + (h + 1) * DK_B])
        v = cv[:, 2 * qw + h * DV_B:2 * qw + (h + 1) * DV_B]
        eg = eg_all[:, A_LANE + h:A_LANE + h + 1]
        beta = beta_all[:, B_LANE + h:B_LANE + h + 1]
        qk = jnp.sum(q * k, axis=-1, keepdims=True)
        k_t, q_t = k.T, q.T
        rows = []
        for r in range(ts):
            s_old = s0_ref[r, h]
            kc = k_t[:, r:r + 1]
            ks = jnp.sum(s_old * kc, axis=0, keepdims=True)
            qs = jnp.sum(s_old * q_t[:, r:r + 1], axis=0, keepdims=True)
            eg_r = eg[r:r + 1, :]
            v_new = beta[r:r + 1, :] * (v[r:r + 1, :] - eg_r * ks)
            rows.append(eg_r * qs + qk[r:r + 1, :] * v_new)
            s_out_ref[r, h] = s_old * eg_r + kc * v_new
        o_h = jnp.concatenate(rows, axis=0)
        ms = jnp.mean(o_h * o_h, axis=-1, keepdims=True)
        o_n = o_h * lax.rsqrt(ms + NORM_EPS) * dnorm_ref[...]
        o_ref[:, h * DV_B:(h + 1) * DV_B] = (o_n * _silu(z_ref[:, h * DV_B:(h + 1) * DV_B])).astype(o_ref.dtype)


def _gdn_sample(u, small, z, conv_buf, s0, w_conv, a_log, dt_bias, delta_norm):
    db = u.shape[0]
    ts = GDN_SEQ_TILE
    assert db % ts == 0
    tail = CONV_K - 1
    alog_row, dtb_row = _gate_rows(a_log, dt_bias)
    row = lambda w: pl.BlockSpec((ts, w), lambda i: (i, 0))
    full = lambda *shape: pl.BlockSpec(shape, lambda i: (0,) * len(shape))
    cb_spec = pl.BlockSpec((tail, ts, CONV_DIM), lambda i: (0, i, 0))
    st_spec = pl.BlockSpec((ts, H_B, DK_B, DV_B), lambda i: (i, 0, 0, 0))
    o, conv_t, s_new = pl.pallas_call(
        _gdn_sample_kernel,
        grid=(db // ts,),
        in_specs=[row(CONV_DIM), cb_spec, row(SMALL_W), row(H_B * DV_B), st_spec,
                  full(CONV_K, CONV_DIM), full(1, SMALL_W), full(1, SMALL_W), full(1, DV_B)],
        out_specs=(row(H_B * DV_B), cb_spec, st_spec),
        out_shape=(jax.ShapeDtypeStruct((db, H_B * DV_B), jnp.float32),
                   jax.ShapeDtypeStruct((tail, db, CONV_DIM), jnp.float32),
                   jax.ShapeDtypeStruct((db, H_B, DK_B, DV_B), jnp.float32)),
        compiler_params=pltpu.CompilerParams(dimension_semantics=("arbitrary",),
                                             vmem_limit_bytes=VMEM_LIMIT_BYTES),
        name="gdn_sample",
    )(u, jnp.swapaxes(conv_buf, 0, 1), small, z, s0, w_conv.astype(jnp.float32), alog_row, dtb_row,
      delta_norm.astype(jnp.float32)[None, :])
    return o, jnp.swapaxes(conv_t, 0, 1), s_new


def _gate_rows(a_log, dt_bias):
    alog_row = jnp.zeros((1, SMALL_W), jnp.float32).at[0, A_LANE:A_LANE + H_B].set(a_log.astype(jnp.float32))
    dtb_row = jnp.zeros((1, SMALL_W), jnp.float32).at[0, A_LANE:A_LANE + H_B].set(dt_bias.astype(jnp.float32))
    return alog_row, dtb_row


def _gdn_prompt(u, small, z, conv0, s0, w_conv, a_log, dt_bias, delta_norm, batch, seq):
    c = GDN_CHUNK
    assert seq % c == 0
    alog_row, dtb_row = _gate_rows(a_log, dt_bias)
    row = lambda w: pl.BlockSpec((batch, c, w), lambda i: (0, i, 0))
    full = lambda *shape: pl.BlockSpec(shape, lambda i: (0,) * len(shape))
    o, conv_new, s_new = pl.pallas_call(
        _gdn_prompt_kernel,
        grid=(seq // c,),
        in_specs=[row(CONV_DIM), row(SMALL_W), row(H_B * DV_B), full(batch, CONV_K - 1, CONV_DIM),
                  full(batch, H_B, DK_B, DV_B), full(CONV_K, CONV_DIM), full(1, SMALL_W), full(1, SMALL_W),
                  full(1, DV_B)],
        out_specs=(row(H_B * DV_B), full(batch, CONV_K - 1, CONV_DIM), full(batch, H_B, DK_B, DV_B)),
        out_shape=(jax.ShapeDtypeStruct((batch, seq, H_B * DV_B), MXU_DTYPE),
                   jax.ShapeDtypeStruct((batch, CONV_K - 1, CONV_DIM), jnp.float32),
                   jax.ShapeDtypeStruct((batch, H_B, DK_B, DV_B), jnp.float32)),
        scratch_shapes=[pltpu.VMEM((batch, SUBLANES + c, CONV_DIM), jnp.float32),
                        pltpu.VMEM((batch, H_B, DK_B, DV_B), jnp.float32)],
        compiler_params=pltpu.CompilerParams(dimension_semantics=("arbitrary",),
                                             vmem_limit_bytes=VMEM_LIMIT_BYTES),
        name="gdn_prompt",
    )(u.reshape(batch, seq, CONV_DIM), small.reshape(batch, seq, SMALL_W), z.reshape(batch, seq, H_B * DV_B),
      conv0, s0, w_conv.astype(jnp.float32), alog_row, dtb_row, delta_norm.astype(jnp.float32)[None, :])
    return o.reshape(batch * seq, H_B * DV_B), conv_new, s_new


def _merge_kernel(x_ref, oa_ref, ob_ref, gl_ref, wba_ref, wbb_ref, wout_ref, gain_ref, x1_ref, hn_ref):
    pa = _mm(oa_ref[...], wba_ref[...])
    pb = _mm(ob_ref[...], wbb_ref[...])
    mix = _sigmoid(gl_ref[:, 0:D_MODEL]) * pa + _sigmoid(gl_ref[:, D_MODEL:2 * D_MODEL]) * pb
    x1 = x_ref[...] + _mm(mix, wout_ref[...])
    x1_ref[...] = x1
    ms = jnp.mean(x1 * x1, axis=-1, keepdims=True)
    hn_ref[...] = (x1 * lax.rsqrt(ms + NORM_EPS) * gain_ref[...]).astype(hn_ref.dtype)


def _merge(x2d, o_a, o_b, gl, w_branch, w_out, norm_ffn, tm):
    n = x2d.shape[0]
    assert n % tm == 0
    row = lambda w: pl.BlockSpec((tm, w), lambda i: (i, 0))
    full = lambda *shape: pl.BlockSpec(shape, lambda i: (0,) * len(shape))
    return pl.pallas_call(
        _merge_kernel,
        grid=(n // tm,),
        in_specs=[row(D_MODEL), row(BRANCH_WIDTH), row(BRANCH_WIDTH), row(2 * D_MODEL),
                  full(BRANCH_WIDTH, D_MODEL), full(BRANCH_WIDTH, D_MODEL), full(D_MODEL, D_MODEL),
                  full(1, D_MODEL)],
        out_specs=(row(D_MODEL), row(D_MODEL)),
        out_shape=(jax.ShapeDtypeStruct((n, D_MODEL), jnp.float32),
                   jax.ShapeDtypeStruct((n, D_MODEL), MXU_DTYPE)),
        compiler_params=pltpu.CompilerParams(dimension_semantics=("arbitrary",),
                                             vmem_limit_bytes=VMEM_LIMIT_BYTES),
        name="merge",
    )(x2d, o_a, o_b, gl, w_branch[0].astype(MXU_DTYPE), w_branch[1].astype(MXU_DTYPE),
      w_out.astype(MXU_DTYPE), norm_ffn.astype(jnp.float32)[None, :])


FFN_TILE = D_FF // 2


def _ffn_kernel(hn_ref, x1_ref, wg_ref, wu_ref, wd_ref, y_ref, acc_ref):
    j = pl.program_id(1)

    @pl.when(j == 0)
    def _():
        acc_ref[...] = x1_ref[...]

    hn = hn_ref[...]
    g = jnp.dot(hn, wg_ref[...], preferred_element_type=jnp.float32)
    u = jnp.dot(hn, wu_ref[...], preferred_element_type=jnp.float32)
    acc_ref[...] += _mm(_silu(g) * u, wd_ref[...])

    @pl.when(j == pl.num_programs(1) - 1)
    def _():
        y_ref[...] = acc_ref[...]


def _ffn(hn, x1, w_gate_up, w_down, tm):
    n = hn.shape[0]
    tf = FFN_TILE
    assert n % tm == 0 and D_FF % tf == 0 and tf % LANES == 0
    nf = D_FF // tf
    wgu = w_gate_up.astype(MXU_DTYPE)
    return pl.pallas_call(
        _ffn_kernel,
        grid=(n // tm, nf),
        in_specs=[pl.BlockSpec((tm, D_MODEL), lambda i, j: (i, 0)),
                  pl.BlockSpec((tm, D_MODEL), lambda i, j: (i, 0)),
                  pl.BlockSpec((D_MODEL, tf), lambda i, j: (0, j)),
                  pl.BlockSpec((D_MODEL, tf), lambda i, j: (0, j + nf)),
                  pl.BlockSpec((tf, D_MODEL), lambda i, j: (j, 0))],
        out_specs=pl.BlockSpec((tm, D_MODEL), lambda i, j: (i, 0)),
        out_shape=jax.ShapeDtypeStruct((n, D_MODEL), jnp.float32),
        scratch_shapes=[pltpu.VMEM((tm, D_MODEL), jnp.float32)],
        compiler_params=pltpu.CompilerParams(dimension_semantics=("arbitrary", "arbitrary"),
                                             vmem_limit_bytes=VMEM_LIMIT_BYTES),
        name="ffn",
    )(hn, x1, wgu, wgu, w_down.astype(MXU_DTYPE))


IN_PROJ_TILE = K_TILE
MERGE_TILE = 512
FFN_ROW_TILE = 512


def _layer(x_p, x_s, cache_k, cache_v, cache_ik, conv_s, delta_s, page_table, norm_mix, w_in, q_norm, k_norm,
           w_conv, a_log, dt_bias, delta_norm, w_branch, w_out, norm_ffn, w_gate_up, w_down):
    b, t, d = x_p.shape
    db = x_s.shape[0]
    past = page_table.shape[1] * PAGE_SIZE
    kw = N_KV_A * HEAD_DIM_A
    w_packed = _pack_w_in(w_in)

    xp2 = x_p.reshape(b * t, d)
    tm = IN_PROJ_TILE
    assert t % tm == 0
    kv, ik, small, u, z, gl, qt, iqt, wt, kb, ikb, vt = _in_proj(
        xp2, _rope_tables(jnp.arange(t)), t // tm, tm, norm_mix, w_packed, q_norm, k_norm, key_major=True)
    o_a = _dsa_prompt(qt, iqt, wt, kb, ikb, vt, b, t)
    conv0 = jnp.zeros((b, CONV_K - 1, CONV_DIM), jnp.float32)
    delta0 = jnp.zeros((b, H_B, DK_B, DV_B), jnp.float32)
    o_b, conv_p, delta_p = _gdn_prompt(u, small, z, conv0, delta0, w_conv, a_log, dt_bias, delta_norm, b, t)
    x1, hn = _merge(xp2, o_a, o_b, gl, w_branch, w_out, norm_ffn, min(MERGE_TILE, b * t))
    y_p = _ffn(hn, x1, w_gate_up, w_down, min(FFN_ROW_TILE, b * t)).reshape(b, t, d)
    kv5 = kv.reshape(b, 2, N_KV_A, HEAD_DIM_A, t)
    st_p = (jnp.transpose(kv5[:, 0], (0, 3, 1, 2)), jnp.transpose(kv5[:, 1], (0, 3, 1, 2)),
            jnp.swapaxes(ik, 1, 2), conv_p, delta_p)

    xs2 = x_s.reshape(db, d)
    kv, ik, small, u, z, gl, qexp, iqhm = _in_proj(
        xs2, _rope_tables(jnp.full((db,), past, jnp.int32)), 1, db, norm_mix, w_packed, q_norm, k_norm,
        key_major=False)
    o_a = _dsa_sample(qexp, iqhm, small, ik, kv, cache_k, cache_v, cache_ik, page_table)
    o_b, conv_n, delta_n = _gdn_sample(u, small, z, conv_s, delta_s, w_conv, a_log, dt_bias, delta_norm)
    x1, hn = _merge(xs2, o_a, o_b, gl, w_branch, w_out, norm_ffn, db)
    y_s = _ffn(hn, x1, w_gate_up, w_down, db).reshape(db, 1, d)
    st_s = (kv[:, 0:kw].reshape(db, 1, N_KV_A, HEAD_DIM_A), kv[:, kw:2 * kw].reshape(db, 1, N_KV_A, HEAD_DIM_A),
            ik.reshape(db, 1, IDX_DIM), conv_n, delta_n)
    return y_p, y_s, st_p, st_s


def kernel(x_prompt, x_sample, cache_k, cache_v, cache_idx_k, state_conv, state_delta, page_table,
           norm_mix, w_in, q_norm, k_norm, w_conv, a_log, dt_bias, delta_norm, w_branch, w_out,
           norm_ffn, w_gate_up, w_down):
    assert x_sample.shape[1] == 1, "the sample group decodes one token per sequence"
    y_p, y_s = x_prompt, x_sample
    new_p, new_s = [], []
    for l in range(w_in.shape[0]):
        y_p, y_s, st_p, st_s = _layer(
            y_p, y_s, cache_k[l], cache_v[l], cache_idx_k[l], state_conv[l], state_delta[l], page_table,
            norm_mix[l], w_in[l], q_norm[l], k_norm[l], w_conv[l], a_log[l], dt_bias[l], delta_norm[l],
            w_branch[l], w_out[l], norm_ffn[l], w_gate_up[l], w_down[l])
        new_p.append(st_p)
        new_s.append(st_s)
    k_p, v_p, ik_p, conv_p, delta_p = [jnp.stack(a) for a in zip(*new_p)]
    k_s, v_s, ik_s, conv_s, delta_s = [jnp.stack(a) for a in zip(*new_s)]
    return (y_p, y_s, k_p, v_p, ik_p, conv_p, delta_p, k_s, v_s, ik_s, conv_s, delta_s)
```

```python
import functools
import math

import jax
import jax.numpy as jnp
import numpy as np
from jax import lax
from jax.experimental import pallas as pl
from jax.experimental.pallas import tpu as pltpu

D_MODEL = 1024
PAGE_SIZE = 128
N_HEADS_A = 8
N_KV_A = 2
HEAD_DIM_A = 64
GROUP_A = N_HEADS_A // N_KV_A
IDX_HEADS = 8
IDX_DIM = 64
TOPK_MAX = 256
ROPE_THETA = 500000.0
H_B = 4
DK_B = 128
DV_B = 128
CONV_K = 4
CONV_DIM = 2 * H_B * DK_B + H_B * DV_B
BRANCH_WIDTH = N_HEADS_A * HEAD_DIM_A
D_FF = -(-8 * D_MODEL // (3 * 256)) * 256
NORM_EPS = 1e-6
NEG_INF = -1e30
IN_SIZES = (N_HEADS_A * HEAD_DIM_A, N_KV_A * HEAD_DIM_A, N_KV_A * HEAD_DIM_A,
            IDX_HEADS * IDX_DIM, IDX_DIM, IDX_HEADS,
            CONV_DIM, H_B, H_B, H_B * DV_B, 2 * D_MODEL)

LANES = 128
SUBLANES = 8
VMEM_LIMIT_BYTES = 56 * 1024 * 1024

MXU_DTYPE = jnp.bfloat16

SMALL_W = LANES
SEG_A = BRANCH_WIDTH + 2 * N_KV_A * HEAD_DIM_A + IDX_HEADS * IDX_DIM + IDX_DIM
SEG_A_PAD = -(-SEG_A // LANES) * LANES
OFF_SMALL = SEG_A_PAD
OFF_U = OFF_SMALL + SMALL_W
OFF_Z = OFF_U + CONV_DIM
OFF_GL = OFF_Z + H_B * DV_B
D_IN_PACKED = OFF_GL + 2 * D_MODEL


def _mm(a, b):
    return jnp.dot(a.astype(MXU_DTYPE), b.astype(MXU_DTYPE), preferred_element_type=jnp.float32)


def _mm_nt(a, b):
    return lax.dot_general(a.astype(MXU_DTYPE), b.astype(MXU_DTYPE), (((1,), (1,)), ((), ())),
                           preferred_element_type=jnp.float32)


def _mm_tn(a, b):
    return lax.dot_general(a.astype(MXU_DTYPE), b.astype(MXU_DTYPE), (((0,), (0,)), ((), ())),
                           preferred_element_type=jnp.float32)


def _split3(x):
    x = x.astype(jnp.float32)
    h = x.astype(MXU_DTYPE)
    r = x - h.astype(jnp.float32)
    m = r.astype(MXU_DTYPE)
    l = (r - m.astype(jnp.float32)).astype(MXU_DTYPE)
    return h, m, l


def _sigmoid(x):
    return 1.0 / (1.0 + jnp.exp(-x))


def _silu(x):
    return x * _sigmoid(x)


def _softplus(x):
    return jnp.maximum(x, 0.0) + jnp.log(1.0 + jnp.exp(-jnp.abs(x)))


def _rope_tile(x, cos_t, sin_lo, sin_hi):
    half = HEAD_DIM_A // 8
    up = pltpu.roll(x, LANES - half, 1)
    dn = pltpu.roll(x, half, 1)
    return x * cos_t + up * sin_lo + dn * sin_hi


def _in_proj_kernel(key_major, x_ref, gain_ref, w_ref, bd_ref, qg_ref, kg_ref, cos_ref, slo_ref, shi_ref,
                    kv_ref, ik_ref, small_ref, u_ref, z_ref, gl_ref, *attn_refs):
    x = x_ref[...]
    ms = jnp.mean(x * x, axis=-1, keepdims=True)
    xn = (x * lax.rsqrt(ms + NORM_EPS) * gain_ref[...]).astype(MXU_DTYPE)

    cos_t, sin_lo, sin_hi = cos_ref[...], slo_ref[...], shi_ref[...]
    lane = lax.broadcasted_iota(jnp.int32, (x.shape[0], LANES), 1)
    lo_half = lane < HEAD_DIM_A

    def head_rms(t, gain):
        tt = t * t
        hi = tt.astype(MXU_DTYPE)
        lo = (tt - hi.astype(jnp.float32)).astype(MXU_DTYPE)
        bd = bd_ref[0:t.shape[1], 0:t.shape[1]]
        msq = (jnp.dot(hi, bd, preferred_element_type=jnp.float32)
               + jnp.dot(lo, bd, preferred_element_type=jnp.float32))
        return t * lax.rsqrt(msq + NORM_EPS) * gain

    q = jnp.dot(xn, w_ref[:, 0:BRANCH_WIDTH], preferred_element_type=jnp.float32)
    q = head_rms(q, qg_ref[...])
    if key_major:
        qt_ref, iqt_ref, wt_ref, kb_ref, ikb_ref, vt_ref = attn_refs
        n_qb = x.shape[0] // Q_TILE
        q_scale = HEAD_DIM_A ** -0.5 * math.log2(math.e)
    else:
        qexp_ref, iqhm_ref = attn_refs
        q_scale = HEAD_DIM_A ** -0.5
    for p in range(BRANCH_WIDTH // LANES):
        t = _rope_tile(q[:, p * LANES:(p + 1) * LANES], cos_t, sin_lo, sin_hi) * q_scale
        t_sw = pltpu.roll(t, HEAD_DIM_A, 1)
        for e in range(2):
            h = 2 * p + e
            n = h // GROUP_A
            src = t if e == n else t_sw
            keep = lo_half if n == 0 else jnp.logical_not(lo_half)
            qe = jnp.where(keep, src, 0.0)
            if key_major:
                qe_t = qe.T
                for j in range(n_qb):
                    qt_ref[j, :, h * Q_TILE:(h + 1) * Q_TILE] = qe_t[:, j * Q_TILE:(j + 1) * Q_TILE].astype(qt_ref.dtype)
            else:
                qexp_ref[h] = qe.astype(qexp_ref.dtype)

    c0 = BRANCH_WIDTH
    kw = N_KV_A * HEAD_DIM_A
    k = jnp.dot(xn, w_ref[:, c0:c0 + kw], preferred_element_type=jnp.float32)
    k = _rope_tile(head_rms(k, kg_ref[...]), cos_t, sin_lo, sin_hi)
    v = jnp.dot(xn, w_ref[:, c0 + kw:c0 + 2 * kw], preferred_element_type=jnp.float32)
    if key_major:
        v_t = v.T
        kv_ref[0, 0] = k.T
        kv_ref[0, 1] = v_t
        kb_ref[...] = k.astype(kb_ref.dtype)
        vt_ref[0, 0:kw, :] = v_t.astype(vt_ref.dtype)
        vt_ref[0, kw:kw + ONES_ROWS, :] = jnp.ones((ONES_ROWS, x.shape[0]), vt_ref.dtype)
    else:
        kv_ref[:, 0:kw] = k
        kv_ref[:, kw:2 * kw] = v

    c1 = c0 + 2 * kw
    iqw = IDX_HEADS * IDX_DIM
    iq = jnp.dot(xn, w_ref[:, c1:c1 + iqw], preferred_element_type=jnp.float32)
    for p in range(iqw // LANES):
        t = _rope_tile(iq[:, p * LANES:(p + 1) * LANES], cos_t, sin_lo, sin_hi)
        if key_major:
            t_t = t.T
            for e in range(2):
                h = 2 * p + e
                for j in range(n_qb):
                    iqt_ref[j, :, h * Q_TILE:(h + 1) * Q_TILE] = (
                        t_t[e * IDX_DIM:(e + 1) * IDX_DIM, j * Q_TILE:(j + 1) * Q_TILE].astype(iqt_ref.dtype))
        else:
            t = t.astype(iqhm_ref.dtype)
            iqhm_ref[2 * p] = t[:, 0:IDX_DIM]
            iqhm_ref[2 * p + 1] = t[:, IDX_DIM:2 * IDX_DIM]

    c2 = c1 + iqw
    ik_sm = jnp.dot(xn, w_ref[:, c2:c2 + 2 * LANES], preferred_element_type=jnp.float32)
    ik_tile = _rope_tile(ik_sm[:, 0:LANES], cos_t, sin_lo, sin_hi)
    ik = ik_tile[:, 0:IDX_DIM]
    small = ik_sm[:, LANES:2 * LANES]
    small_ref[...] = small
    if not key_major:
        ik_ref[...] = ik
    else:
        ik_ref[0] = ik_tile.T[0:IDX_DIM]
        ikb_ref[...] = ik.astype(ikb_ref.dtype)
        small_t = small.T
        for j in range(n_qb):
            wt_ref[j] = small_t[0:IDX_HEADS, j * Q_TILE:(j + 1) * Q_TILE]

    u_ref[...] = jnp.dot(xn, w_ref[:, OFF_U:OFF_U + CONV_DIM], preferred_element_type=jnp.float32)
    z_ref[...] = jnp.dot(xn, w_ref[:, OFF_Z:OFF_Z + H_B * DV_B], preferred_element_type=jnp.float32)
    gl_ref[...] = jnp.dot(xn, w_ref[:, OFF_GL:OFF_GL + 2 * D_MODEL], preferred_element_type=jnp.float32)


def _pack_w_in(w_in):
    pts = np.cumsum(IN_SIZES)[:-1].tolist()
    q, k, v, iq, ik, iw, u, a, b, z, gl = jnp.split(w_in, pts, axis=-1)
    d = w_in.shape[0]
    seg_a = jnp.concatenate([q, k, v, iq, ik, jnp.zeros((d, SEG_A_PAD - SEG_A), w_in.dtype)], axis=1)
    small = jnp.concatenate([iw, a, b, jnp.zeros((d, SMALL_W - IDX_HEADS - 2 * H_B), w_in.dtype)], axis=1)
    return jnp.concatenate([seg_a, small, u, z, gl], axis=1).astype(MXU_DTYPE)


def _rope_tables(pos):
    rot = HEAD_DIM_A // 4
    half = rot // 2
    inv_freq = ROPE_THETA ** (-jnp.arange(half, dtype=jnp.float32) / half)
    ang = pos.astype(jnp.float32)[:, None] * inv_freq[None, :]
    cos, sin = jnp.cos(ang), jnp.sin(ang)
    rows = pos.shape[0]
    one = jnp.ones((rows, HEAD_DIM_A - rot), jnp.float32)
    zero = jnp.zeros((rows, HEAD_DIM_A - rot), jnp.float32)
    zh = jnp.zeros((rows, half), jnp.float32)
    cos_h = jnp.concatenate([cos, cos, one], axis=1)
    slo_h = jnp.concatenate([-sin, zh, zero], axis=1)
    shi_h = jnp.concatenate([zh, sin, zero], axis=1)
    rep = LANES // HEAD_DIM_A
    return jnp.tile(cos_h, (1, rep)), jnp.tile(slo_h, (1, rep)), jnp.tile(shi_h, (1, rep))


def _in_proj(x2d, pos_tables, n_table_blocks, tm, norm_mix, w_packed, q_norm, k_norm, key_major):
    n = x2d.shape[0]
    assert n % tm == 0 and (not key_major or tm == K_TILE)
    cos_t, sin_lo, sin_hi = pos_tables
    bd = jnp.kron(jnp.eye(BRANCH_WIDTH // HEAD_DIM_A, dtype=jnp.float32),
                  jnp.full((HEAD_DIM_A, HEAD_DIM_A), 1.0 / HEAD_DIM_A, jnp.float32)).astype(MXU_DTYPE)
    qg = jnp.tile(q_norm.astype(jnp.float32), BRANCH_WIDTH // HEAD_DIM_A)[None, :]
    kg = jnp.tile(k_norm.astype(jnp.float32), N_KV_A)[None, :]
    kw = N_KV_A * HEAD_DIM_A
    row = lambda w: pl.BlockSpec((tm, w), lambda i: (i, 0))
    full = lambda a: pl.BlockSpec(a.shape, lambda i: (0,) * a.ndim)
    tab = pl.BlockSpec((tm, LANES), lambda i: (i % n_table_blocks, 0))
    if key_major:
        seq = n_table_blocks * tm
        nt = n_table_blocks
        kv_shape, ik_shape = (n // seq, 2, kw, seq), (n // seq, IDX_DIM, seq)
        kv_spec = pl.BlockSpec((1, 2, kw, tm), lambda i: (i // nt, 0, 0, i % nt))
        ik_spec = pl.BlockSpec((1, IDX_DIM, tm), lambda i: (i // nt, 0, i % nt))
    else:
        kv_shape, ik_shape = (n, 2 * kw), (n, IDX_DIM)
        kv_spec, ik_spec = row(2 * kw), row(IDX_DIM)
    out_shape = [
        jax.ShapeDtypeStruct(kv_shape, jnp.float32),
        jax.ShapeDtypeStruct(ik_shape, jnp.float32),
        jax.ShapeDtypeStruct((n, SMALL_W), jnp.float32),
        jax.ShapeDtypeStruct((n, CONV_DIM), jnp.float32),
        jax.ShapeDtypeStruct((n, H_B * DV_B), jnp.float32),
        jax.ShapeDtypeStruct((n, 2 * D_MODEL), jnp.float32),
    ]
    out_specs = [kv_spec, ik_spec, row(SMALL_W), row(CONV_DIM), row(H_B * DV_B), row(2 * D_MODEL)]
    if key_major:
        n_qb = tm // Q_TILE
        blk = lambda r, w: pl.BlockSpec((n_qb, r, w), lambda i: (i, 0, 0))
        out_shape += [
            jax.ShapeDtypeStruct((n // Q_TILE, LANES, N_HEADS_A * Q_TILE), MXU_DTYPE),
            jax.ShapeDtypeStruct((n // Q_TILE, IDX_DIM, IDX_HEADS * Q_TILE), MXU_DTYPE),
            jax.ShapeDtypeStruct((n // Q_TILE, IDX_HEADS, Q_TILE), jnp.float32),
            jax.ShapeDtypeStruct((n, kw), MXU_DTYPE),
            jax.ShapeDtypeStruct((n, IDX_DIM), MXU_DTYPE),
            jax.ShapeDtypeStruct((n // K_TILE, kw + ONES_ROWS, K_TILE), MXU_DTYPE),
        ]
        out_specs += [blk(LANES, N_HEADS_A * Q_TILE), blk(IDX_DIM, IDX_HEADS * Q_TILE), blk(IDX_HEADS, Q_TILE),
                      row(kw), row(IDX_DIM), pl.BlockSpec((1, kw + ONES_ROWS, K_TILE), lambda i: (i, 0, 0))]
    else:
        out_shape += [jax.ShapeDtypeStruct((N_HEADS_A, n, LANES), MXU_DTYPE),
                      jax.ShapeDtypeStruct((IDX_HEADS, n, IDX_DIM), MXU_DTYPE)]
        out_specs += [pl.BlockSpec((N_HEADS_A, tm, LANES), lambda i: (0, i, 0)),
                      pl.BlockSpec((IDX_HEADS, tm, IDX_DIM), lambda i: (0, i, 0))]
    return pl.pallas_call(
        functools.partial(_in_proj_kernel, key_major),
        grid=(n // tm,),
        in_specs=[row(D_MODEL), full(norm_mix[None, :]), full(w_packed), full(bd), full(qg), full(kg),
                  tab, tab, tab],
        out_specs=tuple(out_specs),
        out_shape=tuple(out_shape),
        compiler_params=pltpu.CompilerParams(dimension_semantics=("arbitrary",),
                                             vmem_limit_bytes=VMEM_LIMIT_BYTES),
        name="in_proj",
    )(x2d, norm_mix[None, :].astype(jnp.float32), w_packed, bd, qg, kg, cos_t, sin_lo, sin_hi)


_INT_MAG = 0x7FFFFFFF


def _f32_key(x):
    b = lax.bitcast_convert_type(x, jnp.int32)
    return b ^ (lax.shift_right_arithmetic(b, 31) & _INT_MAG)


def _key_f32(k):
    b = k ^ (lax.shift_right_arithmetic(k, 31) & _INT_MAG)
    return lax.bitcast_convert_type(b, jnp.float32)


def _topk_threshold(count_ge, count_tie, row_min, row_max, n_adm, topk, n_keys, fixed_steps=18, linear_steps=24):
    kf = jnp.float32(topk)
    need = n_adm > topk
    lo_k = _f32_key(row_min)
    hi_k = _f32_key(row_max) + 1
    thr = jnp.where(need, row_min, -jnp.inf)
    done = jnp.where(need, 0, 1).astype(jnp.int32)

    zero = jnp.zeros_like(row_min)
    ge0 = count_ge(zero)
    gt0 = ge0 - count_tie(zero, jnp.full_like(lo_k, n_keys + 1))
    live = jnp.logical_and(need, lo_k < hi_k - 1)
    hit0 = jnp.logical_and(live, ge0 == kf)
    tie0 = jnp.logical_and(live, jnp.logical_and(gt0 < kf, ge0 > kf))
    thr = jnp.where(jnp.logical_or(hit0, tie0), zero, thr)
    done = jnp.where(jnp.logical_or(hit0, tie0), 1, done)
    zero_k = _f32_key(zero)
    lo_k = jnp.where(jnp.logical_and(live, ge0 > kf), jnp.maximum(lo_k, zero_k), lo_k)
    hi_k = jnp.where(jnp.logical_and(live, ge0 < kf), jnp.minimum(hi_k, zero_k), hi_k)
    state = (lo_k, hi_k, thr, jnp.where(tie0, gt0, zero), done, jnp.where(tie0, 1, 0).astype(jnp.int32))

    def step(linear, st):
        lo_k, hi_k, thr, cnt_hi, done, tie = st
        adjacent = hi_k == lo_k + 1
        lo_f, hi_f = _key_f32(lo_k), _key_f32(hi_k)
        mid_lin = _f32_key(lo_f + 0.5 * (hi_f - lo_f))
        mid_lin = jnp.minimum(jnp.maximum(mid_lin, lo_k + 1), hi_k - 1)
        mid_int = (lo_k & hi_k) + lax.shift_right_arithmetic(lo_k ^ hi_k, 1)
        mid = mid_lin if linear is True else jnp.where(linear, mid_lin, mid_int)
        mid_f = _key_f32(mid)
        cnt = count_ge(mid_f)
        live = jnp.logical_and(done == 0, jnp.logical_not(adjacent))
        hit = jnp.logical_and(live, cnt == kf)
        up = jnp.logical_and(live, cnt > kf)
        dn = jnp.logical_and(live, cnt < kf)
        new_tie = jnp.logical_and(done == 0, adjacent)
        thr = jnp.where(hit, mid_f, jnp.where(new_tie, lo_f, thr))
        tie = jnp.where(new_tie, 1, tie)
        done = jnp.where(jnp.logical_or(hit, new_tie), 1, done)
        lo_k = jnp.where(up, mid, lo_k)
        hi_k = jnp.where(dn, mid, hi_k)
        cnt_hi = jnp.where(dn, cnt, cnt_hi)
        return (lo_k, hi_k, thr, cnt_hi, done, tie)

    state = lax.fori_loop(0, fixed_steps, lambda _, st: step(True, st), state)

    def cond(st):
        it, active = st[0], st[1]
        return jnp.logical_and(it < 80, active > 0)

    def body(st):
        it = st[0]
        new = step(it < linear_steps, st[2:])
        return (it + 1, jnp.max(1 - new[4])) + new

    st = lax.while_loop(cond, body, (jnp.int32(fixed_steps), jnp.max(1 - state[4])) + state)
    thr, cnt_hi, tie = st[4], st[5], st[7]

    need_ties = kf - cnt_hi
    n_bits = max(1, int(math.ceil(math.log2(n_keys + 1))))
    any_tie = jnp.max(tie)

    def tie_body(_, lm):
        lo_m, hi_m = lm
        mid = lax.shift_right_arithmetic(lo_m + hi_m, 1)
        ge = count_tie(thr, mid) >= need_ties
        return jnp.where(ge, lo_m, mid), jnp.where(ge, mid, hi_m)

    lo_m0 = jnp.zeros_like(lo_k)
    hi_m0 = jnp.full_like(lo_k, n_keys)
    _, hi_m = lax.fori_loop(0, jnp.where(any_tie > 0, n_bits + 1, 0), tie_body, (lo_m0, hi_m0))
    cut = jnp.where(tie > 0, hi_m, n_keys + 1)
    return thr, cut


Q_TILE = 128
K_TILE = 256
K_UNROLL = 4
SCAN_UNROLL = 4
ONES_ROWS = 16


def _dsa_prompt_kernel(topk, qt_ref, iqt_ref, wt_ref, kb_ref, ikb_ref, vt_ref, o_ref, sc_ref, acc_ref, m_ref):
    i = pl.program_id(1)
    tq, kc = Q_TILE, K_TILE
    n_keys = sc_ref.shape[0] * kc
    nchunk = (i + 2) // 2
    qpos = i * tq + lax.broadcasted_iota(jnp.int32, (kc, tq), 1)
    krow = lax.broadcasted_iota(jnp.int32, (kc, tq), 0)
    qpos8 = qpos[0:SUBLANES]

    def col_reduce(x, op):
        return op(x.reshape(kc // SUBLANES, SUBLANES, tq), axis=0)

    def all_rows(x, op2):
        for shift in (4, 2, 1):
            x = op2(x, pltpu.roll(x, shift, 0))
        return x

    def tile_loop(first, rest, init, unroll):
        def trip(t, carry):
            heads = [first(t * unroll + sub) for sub in range(unroll)]
            for sub in range(unroll):
                carry = rest(t * unroll + sub, heads[sub], carry)
            return carry
        full = nchunk // unroll
        carry = lax.fori_loop(0, full, trip, init)
        return lax.fori_loop(full * unroll, nchunk, lambda c, carry: rest(c, first(c), carry), carry)

    def key_rows(c):
        return pl.ds(pl.multiple_of(c * kc, kc), kc)

    w = wt_ref[0]
    iqt = iqt_ref[0]
    s_scale = IDX_DIM ** -0.5 * IDX_HEADS ** -0.5

    def score_dots(c):
        return jnp.dot(ikb_ref[key_rows(c), :], iqt, preferred_element_type=jnp.float32)

    def score_tile(c, d, carry):
        s = w[0:1] * jnp.maximum(d[:, 0:tq], 0.0)
        for h in range(1, IDX_HEADS):
            s = s + w[h:h + 1] * jnp.maximum(d[:, h * tq:(h + 1) * tq], 0.0)
        sc_ref[c] = jnp.where(c * kc + krow <= qpos, s * s_scale, NEG_INF)
        return carry

    tile_loop(score_dots, score_tile, 0, K_UNROLL)

    def scan(body, init):
        return tile_loop(lambda c: sc_ref[c], body, init, SCAN_UNROLL)

    def count(pred):
        def body(c, s, acc):
            return acc + col_reduce(jnp.where(pred(s, c * kc + krow), 1.0, 0.0), jnp.sum)
        return all_rows(scan(body, jnp.zeros((SUBLANES, tq), jnp.float32)), jnp.add)

    def count_ge(c):
        return count(lambda s, kpos: s >= c[0:1])

    def count_tie(v, m):
        return count(lambda s, kpos: jnp.logical_and(s == v[0:1], kpos < m[0:1]))

    def minmax(c, s, mm):
        adm = c * kc + krow <= qpos
        return (jnp.minimum(mm[0], col_reduce(jnp.where(adm, s, jnp.inf), jnp.min)),
                jnp.maximum(mm[1], col_reduce(jnp.where(adm, s, -jnp.inf), jnp.max)))

    mn, mx = scan(minmax, (jnp.full((SUBLANES, tq), jnp.inf, jnp.float32),
                           jnp.full((SUBLANES, tq), -jnp.inf, jnp.float32)))
    thr, cut = _topk_threshold(count_ge, count_tie, all_rows(mn, jnp.minimum), all_rows(mx, jnp.maximum),
                               qpos8 + 1, topk, n_keys)
    thr_row, cut_row = thr[0:1], cut[0:1]

    m_ref[...] = jnp.full(m_ref.shape, 0.5 * NEG_INF, jnp.float32)
    acc_ref[...] = jnp.zeros(acc_ref.shape, jnp.float32)
    qt = qt_ref[0]
    kw = N_KV_A * HEAD_DIM_A

    def logits(c):
        return jnp.dot(kb_ref[key_rows(c), :], qt, preferred_element_type=jnp.float32)

    def attend_group(cs, lgs):
        biases = []
        for c in cs:
            s = sc_ref[c]
            kpos = c * kc + krow
            sel = jnp.logical_or(s > thr_row, jnp.logical_and(s == thr_row, kpos < cut_row))
            biases.append(jnp.where(jnp.logical_and(sel, kpos <= qpos), 0.0, NEG_INF))
        ps, alphas = [], []
        for h in range(N_HEADS_A):
            cols = slice(h * tq, (h + 1) * tq)
            lghs = [lg[:, cols] + bias for lg, bias in zip(lgs, biases)]
            tile_max = functools.reduce(jnp.maximum, [col_reduce(lgh, jnp.max) for lgh in lghs])
            m_old = m_ref[:, cols]
            m_new = jnp.maximum(m_old, all_rows(tile_max, jnp.maximum))
            alphas.append(jnp.exp2(m_old - m_new)[0:1])
            ps.append(jnp.concatenate([jnp.exp2(lgh - m_new[0:1]).astype(MXU_DTYPE) for lgh in lghs], axis=0))
            m_ref[:, cols] = m_new
        vt = jnp.concatenate([vt_ref[c] for c in cs], axis=1)
        pv = jnp.dot(vt, jnp.concatenate(ps, axis=1), preferred_element_type=jnp.float32)
        acc_ref[...] = acc_ref[...] * jnp.concatenate(alphas, axis=1) + pv

    def attend_trip(t, carry):
        cs = [t * K_UNROLL + sub for sub in range(K_UNROLL)]
        lgs = [logits(c) for c in cs]
        for j in range(0, K_UNROLL, 2):
            attend_group(cs[j:j + 2], lgs[j:j + 2])
        return carry

    def attend_single(c, carry):
        attend_group([c], [logits(c)])
        return carry

    full_trips = nchunk // K_UNROLL
    lax.fori_loop(0, full_trips, attend_trip, 0)
    lax.fori_loop(full_trips * K_UNROLL, nchunk, attend_single, 0)

    acc = acc_ref[...]
    o_t = acc[0:kw] / acc[kw:kw + 1]
    for p in range(N_HEADS_A // 2):
        n = (2 * p) // GROUP_A
        pair = jnp.concatenate([o_t[n * HEAD_DIM_A:(n + 1) * HEAD_DIM_A, (2 * p + e) * tq:(2 * p + e + 1) * tq]
                                for e in range(2)], axis=0)
        o_ref[:, p * LANES:(p + 1) * LANES] = pair.T.astype(o_ref.dtype)


def _dsa_prompt(qt, iqt, wt, kb, ikb, vt, batch, seq):
    tq, kc = Q_TILE, K_TILE
    assert seq % kc == 0
    nq = seq // tq
    nk = seq // kc
    n = batch * seq
    kw = N_KV_A * HEAD_DIM_A
    topk = min(TOPK_MAX, seq // 4)
    return pl.pallas_call(
        functools.partial(_dsa_prompt_kernel, topk),
        grid=(batch, nq),
        in_specs=[
            pl.BlockSpec((1, LANES, N_HEADS_A * tq), lambda b, i: (b * nq + i, 0, 0)),
            pl.BlockSpec((1, IDX_DIM, IDX_HEADS * tq), lambda b, i: (b * nq + i, 0, 0)),
            pl.BlockSpec((1, IDX_HEADS, tq), lambda b, i: (b * nq + i, 0, 0)),
            pl.BlockSpec((seq, kw), lambda b, i: (b, 0)),
            pl.BlockSpec((seq, IDX_DIM), lambda b, i: (b, 0)),
            pl.BlockSpec((nk, kw + ONES_ROWS, kc), lambda b, i: (b, 0, 0)),
        ],
        out_specs=pl.BlockSpec((tq, BRANCH_WIDTH), lambda b, i: (b * nq + i, 0)),
        out_shape=jax.ShapeDtypeStruct((n, BRANCH_WIDTH), MXU_DTYPE),
        scratch_shapes=[
            pltpu.VMEM((nk, kc, tq), jnp.float32),
            pltpu.VMEM((kw + ONES_ROWS, N_HEADS_A * tq), jnp.float32),
            pltpu.VMEM((SUBLANES, N_HEADS_A * tq), jnp.float32),
        ],
        compiler_params=pltpu.CompilerParams(dimension_semantics=("arbitrary", "arbitrary"),
                                             vmem_limit_bytes=VMEM_LIMIT_BYTES),
        name="dsa_prompt",
    )(qt, iqt, wt, kb, ikb, vt)


def _page_copy(pt_ref, cache_ref, buf_ref, sem_ref, seq, page, slot):
    lanes = pl.ds(page * PAGE_SIZE, PAGE_SIZE)
    return pltpu.make_async_copy(cache_ref.at[pt_ref[seq, page]], buf_ref.at[slot, :, lanes], sem_ref.at[slot])


def _pages_start(pt_ref, cache_ref, buf_ref, sem_ref, seq, slot, n_pages):
    for p in range(n_pages):
        _page_copy(pt_ref, cache_ref, buf_ref, sem_ref, seq, p, slot).start()


def _pages_wait(pt_ref, cache_ref, buf_ref, sem_ref, seq, slot, n_pages):
    for p in range(n_pages):
        _page_copy(pt_ref, cache_ref, buf_ref, sem_ref, seq, p, slot).wait()


def _dsa_sample_score_kernel(topk, n_pages, pt_ref, iq_ref, iw_ref, iknew_ref, cache_ik_ref,
                             sc_ref, thr_ref, cut_ref, ikbuf_ref, sem_ref):
    s = pl.program_id(0)
    n_seq = pl.num_programs(0)
    past = n_pages * PAGE_SIZE
    n_tiles = sc_ref.shape[0]
    slot = s % 2

    @pl.when(s == 0)
    def _():
        _pages_start(pt_ref, cache_ik_ref, ikbuf_ref, sem_ref, 0, 0, n_pages)

    @pl.when(s + 1 < n_seq)
    def _():
        _pages_start(pt_ref, cache_ik_ref, ikbuf_ref, sem_ref, s + 1, 1 - slot, n_pages)

    _pages_wait(pt_ref, cache_ik_ref, ikbuf_ref, sem_ref, s, slot, n_pages)

    iq = iq_ref[0]
    w = iw_ref[0]
    s_scale = IDX_DIM ** -0.5 * IDX_HEADS ** -0.5
    d = _mm(iq, ikbuf_ref[slot])
    srow = jnp.sum(w * jnp.maximum(d, 0.0), axis=0, keepdims=True) * s_scale
    for j in range(n_pages):
        sc_ref[j, pl.ds(s, 1), :] = srow[:, j * LANES:(j + 1) * LANES]
    ik_new = iknew_ref[0].astype(MXU_DTYPE).astype(jnp.float32)
    d_self = jnp.sum(iq.astype(jnp.float32) * ik_new, axis=1, keepdims=True)
    s_self = jnp.sum(w * jnp.maximum(d_self, 0.0), axis=0, keepdims=True) * s_scale
    lane1 = lax.broadcasted_iota(jnp.int32, (1, LANES), 1)
    sc_ref[n_tiles - 1, pl.ds(s, 1), :] = jnp.where(lane1 == 0, s_self, NEG_INF)

    @pl.when(s == n_seq - 1)
    def _():
        rows = sc_ref.shape[1]
        lane = lax.broadcasted_iota(jnp.int32, (rows, LANES), 1)

        def count(pred):
            def body(j, acc):
                return acc + jnp.where(pred(sc_ref[j], j * LANES + lane), 1.0, 0.0)
            acc = lax.fori_loop(0, n_tiles, body, jnp.zeros((rows, LANES), jnp.float32))
            return jnp.broadcast_to(jnp.sum(acc, axis=1, keepdims=True), (rows, LANES))

        def count_ge(c):
            return count(lambda t, kpos: t >= c)

        def count_tie(v, m):
            return count(lambda t, kpos: jnp.logical_and(t == v, kpos < m))

        def minmax(j, mm):
            t = sc_ref[j]
            adm = j * LANES + lane <= past
            return (jnp.minimum(mm[0], jnp.where(adm, t, jnp.inf)),
                    jnp.maximum(mm[1], jnp.where(adm, t, -jnp.inf)))

        mn, mx = lax.fori_loop(0, n_tiles, minmax, (jnp.full((rows, LANES), jnp.inf, jnp.float32),
                                                    jnp.full((rows, LANES), -jnp.inf, jnp.float32)))
        row_min = jnp.broadcast_to(jnp.min(mn, axis=1, keepdims=True), (rows, LANES))
        row_max = jnp.broadcast_to(jnp.max(mx, axis=1, keepdims=True), (rows, LANES))
        n_adm = jnp.full((rows, LANES), past + 1, jnp.int32)
        thr, cut = _topk_threshold(count_ge, count_tie, row_min, row_max, n_adm, topk, n_tiles * LANES)
        thr_ref[...] = thr
        cut_ref[...] = cut


def _dsa_sample_attend_kernel(n_pages, pt_ref, q_ref, sc_ref, thr_ref, cut_ref, kvnew_ref,
                              cache_k_ref, cache_v_ref, o_ref, kbuf_ref, vbuf_ref, ksem_ref, vsem_ref):
    s = pl.program_id(0)
    n_seq = pl.num_programs(0)
    past = n_pages * PAGE_SIZE
    slot = s % 2

    def start(seq, sl):
        _pages_start(pt_ref, cache_k_ref, kbuf_ref, ksem_ref, seq, sl, n_pages)
        _pages_start(pt_ref, cache_v_ref, vbuf_ref, vsem_ref, seq, sl, n_pages)

    @pl.when(s == 0)
    def _():
        start(0, 0)

    @pl.when(s + 1 < n_seq)
    def _():
        start(s + 1, 1 - slot)

    _pages_wait(pt_ref, cache_k_ref, kbuf_ref, ksem_ref, s, slot, n_pages)
    _pages_wait(pt_ref, cache_v_ref, vbuf_ref, vsem_ref, s, slot, n_pages)

    q = q_ref[0]
    thr = thr_ref[0][:, 0:1]
    cut = cut_ref[0][:, 0:1]
    kw = N_KV_A * HEAD_DIM_A
    k_new = kvnew_ref[0][:, 0:kw].astype(MXU_DTYPE).astype(jnp.float32)
    v_new = kvnew_ref[0][:, kw:2 * kw].astype(MXU_DTYPE).astype(jnp.float32)
    sc = sc_ref[0]

    def selected(srow, kpos):
        return jnp.logical_or(srow > thr, jnp.logical_and(srow == thr, kpos < cut))

    kpos = lax.broadcasted_iota(jnp.int32, (1, past), 1)
    bias = jnp.where(selected(sc[:, 0:past], kpos), 0.0, NEG_INF)
    lg = _mm(q, kbuf_ref[slot]) + bias
    lg_self = jnp.sum(q.astype(jnp.float32) * k_new, axis=1, keepdims=True)
    lg_self = jnp.where(selected(sc[:, past:past + 1], past), lg_self, NEG_INF)
    m = jnp.maximum(jnp.max(lg, axis=1, keepdims=True), lg_self)
    p = jnp.exp(lg - m)
    p_self = jnp.exp(lg_self - m)
    denom = jnp.sum(p, axis=1, keepdims=True) + p_self
    o = (_mm_nt(p, vbuf_ref[slot]) + p_self * v_new) / denom
    parts = []
    for h in range(N_HEADS_A):
        n = h // GROUP_A
        parts.append(o[h:h + 1, n * HEAD_DIM_A:(n + 1) * HEAD_DIM_A])
    o_ref[0] = jnp.concatenate(parts, axis=1)


def _dsa_sample(qexp, iqhm, small, ik_new, kv_new, cache_k, cache_v, cache_ik, page_table):
    db, n_pages = page_table.shape
    past = n_pages * PAGE_SIZE
    n_pool = cache_ik.shape[0]
    topk = min(TOPK_MAX, (past + 1) // 4)
    n_tiles = n_pages + 1
    kw = N_KV_A * HEAD_DIM_A
    q_s = jnp.swapaxes(qexp, 0, 1)
    iq_s = jnp.swapaxes(iqhm, 0, 1)
    iw_s = small[:, 0:IDX_HEADS].reshape(db, IDX_HEADS, 1)
    ck_t = jnp.transpose(cache_k, (0, 2, 3, 1)).reshape(n_pool, kw, PAGE_SIZE)
    cv_t = jnp.transpose(cache_v, (0, 2, 3, 1)).reshape(n_pool, kw, PAGE_SIZE)
    cik_t = jnp.swapaxes(cache_ik, 1, 2)
    cparams = pltpu.CompilerParams(dimension_semantics=("arbitrary",), vmem_limit_bytes=VMEM_LIMIT_BYTES)
    per_seq = lambda *shape: pl.BlockSpec((1,) + shape, lambda s, pt: (s,) + (0,) * len(shape))
    whole = lambda *shape: pl.BlockSpec(shape, lambda s, pt: (0,) * len(shape))
    any_spec = pl.BlockSpec(memory_space=pl.ANY)

    sc, thr, cut = pl.pallas_call(
        functools.partial(_dsa_sample_score_kernel, topk, n_pages),
        grid_spec=pltpu.PrefetchScalarGridSpec(
            num_scalar_prefetch=1,
            grid=(db,),
            in_specs=[per_seq(IDX_HEADS, IDX_DIM), per_seq(IDX_HEADS, 1), per_seq(1, IDX_DIM), any_spec],
            out_specs=(whole(n_tiles, db, LANES), whole(db, LANES), whole(db, LANES)),
            scratch_shapes=[pltpu.VMEM((2, IDX_DIM, past), jnp.float32), pltpu.SemaphoreType.DMA((2,))],
        ),
        out_shape=(jax.ShapeDtypeStruct((n_tiles, db, LANES), jnp.float32),
                   jax.ShapeDtypeStruct((db, LANES), jnp.float32),
                   jax.ShapeDtypeStruct((db, LANES), jnp.int32)),
        compiler_params=cparams,
        name="dsa_sample_score",
    )(page_table, iq_s, iw_s, ik_new.reshape(db, 1, IDX_DIM), cik_t)

    o = pl.pallas_call(
        functools.partial(_dsa_sample_attend_kernel, n_pages),
        grid_spec=pltpu.PrefetchScalarGridSpec(
            num_scalar_prefetch=1,
            grid=(db,),
            in_specs=[per_seq(N_HEADS_A, LANES), per_seq(1, n_tiles * LANES), per_seq(1, LANES), per_seq(1, LANES),
                      per_seq(1, 2 * kw), any_spec, any_spec],
            out_specs=per_seq(1, BRANCH_WIDTH),
            scratch_shapes=[pltpu.VMEM((2, kw, past), jnp.float32), pltpu.VMEM((2, kw, past), jnp.float32),
                            pltpu.SemaphoreType.DMA((2,)), pltpu.SemaphoreType.DMA((2,))],
        ),
        out_shape=jax.ShapeDtypeStruct((db, 1, BRANCH_WIDTH), jnp.float32),
        compiler_params=cparams,
        name="dsa_sample_attend",
    )(page_table, q_s, jnp.swapaxes(sc, 0, 1).reshape(db, 1, n_tiles * LANES),
      thr.reshape(db, 1, LANES), cut.reshape(db, 1, LANES),
      kv_new.reshape(db, 1, 2 * kw), ck_t, cv_t)
    return o.reshape(db, BRANCH_WIDTH)


GDN_CHUNK = 128
A_LANE = IDX_HEADS
B_LANE = IDX_HEADS + H_B


def _split2(x):
    h = x.astype(MXU_DTYPE)
    return h, (x - h.astype(jnp.float32)).astype(MXU_DTYPE)


def _mm2(a, b):
    a1, a2 = _split2(a)
    b1, b2 = _split2(b)
    d = functools.partial(jnp.dot, preferred_element_type=jnp.float32)
    return d(a1, b1) + (d(a1, b2) + d(a2, b1))


def _unit_lower_inverses(mats):
    n = mats[0].shape[0]
    eye = (lax.broadcasted_iota(jnp.int32, (n, n), 0) == lax.broadcasted_iota(jnp.int32, (n, n), 1))
    ss = [jnp.where(eye, 1.0, 0.0) - a for a in mats]
    ps = [_mm2(a, a) for a in mats]
    k = 2
    while k < n:
        ss = [s + _mm2(s, p) for s, p in zip(ss, ps)]
        k *= 2
        if k < n:
            ps = [_mm2(p, p) for p in ps]
    return ss


def _l2norm(x):
    return x * lax.rsqrt(jnp.sum(x * x, axis=-1, keepdims=True) + NORM_EPS)


def _gdn_prompt_kernel(u_ref, small_ref, z_ref, conv0_ref, s0_ref, wconv_ref, alog_ref, dtb_ref, dnorm_ref,
                       o_ref, conv_out_ref, s_out_ref, ucat_ref, state_ref):
    n = pl.program_id(0)
    nb = u_ref.shape[0]
    c = GDN_CHUNK
    head = SUBLANES
    tail = CONV_K - 1

    @pl.when(n == 0)
    def _():
        ucat_ref[:, head - tail:head, :] = conv0_ref[...]
        state_ref[...] = s0_ref[...]

    row = lax.broadcasted_iota(jnp.int32, (c, c), 0)
    col = lax.broadcasted_iota(jnp.int32, (c, c), 1)
    lower = row >= col
    strict = row > col
    ltri = jnp.where(lower, 1.0, 0.0).astype(MXU_DTYPE)
    d = functools.partial(jnp.dot, preferred_element_type=jnp.float32)

    chains = [(b, h) for b in range(nb) for h in range(H_B)]
    qw = H_B * DK_B
    qs, ks, vs, betas, gcols = [], [], [], [], []
    for b in range(nb):
        ucat_ref[b, head:head + c, :] = u_ref[b]
        y = wconv_ref[tail:tail + 1, :] * ucat_ref[b, head:head + c, :]
        for j in range(tail):
            y = y + wconv_ref[j:j + 1, :] * ucat_ref[b, head - tail + j:head - tail + j + c, :]
        cv = _silu(y)
        carry_rows = ucat_ref[b, head + c - tail:head + c, :]
        ucat_ref[b, head - tail:head, :] = carry_rows
        conv_out_ref[b] = carry_rows
        sm = small_ref[b]
        g_all = -jnp.exp(alog_ref[...]) * _softplus(sm + dtb_ref[...])
        beta_all = _sigmoid(sm)
        a1, a2, a3 = _split3(g_all)
        gc_all = d(ltri, a1) + (d(ltri, a2) + d(ltri, a3))
        for h in range(H_B):
            qs.append(_l2norm(cv[:, h * DK_B:(h + 1) * DK_B]) * (DK_B ** -0.5))
            ks.append(_l2norm(cv[:, qw + h * DK_B:qw + (h + 1) * DK_B]))
            vs.append(cv[:, 2 * qw + h * DV_B:2 * qw + (h + 1) * DV_B])
            betas.append(jnp.broadcast_to(beta_all[:, B_LANE + h:B_LANE + h + 1], (c, LANES)))
            gcols.append(jnp.broadcast_to(gc_all[:, A_LANE + h:A_LANE + h + 1], (c, c)))
    decays = [jnp.where(lower, jnp.exp(jnp.where(lower, g - g.T, 0.0)), 0.0) for g in gcols]
    egs = [jnp.exp(g) for g in gcols]
    g_lasts = [g[c - 1:c, :] for g in gcols]
    kbs = [k * b for k, b in zip(ks, betas)]
    vbs = [v * b for v, b in zip(vs, betas)]
    kks = [_mm_nt(kb, k) for kb, k in zip(kbs, ks)]
    qks = [_mm_nt(q, k) for q, k in zip(qs, ks)]
    t_invs = _unit_lower_inverses([jnp.where(strict, kk * dc, 0.0) for kk, dc in zip(kks, decays)])
    sols = [_mm2(t, jnp.concatenate([vb, kb * eg], axis=1)) for t, vb, kb, eg in zip(t_invs, vbs, kbs, egs)]
    s_olds = [state_ref[b, h] for b, h in chains]
    v_news = [sol[:, 0:DV_B] - _mm(sol[:, DV_B:DV_B + DK_B], s) for sol, s in zip(sols, s_olds)]
    o_hs = [_mm(q * eg, s) + _mm(qk * dc, v_new)
            for q, eg, s, qk, dc, v_new in zip(qs, egs, s_olds, qks, decays, v_news)]
    for i, (b, h) in enumerate(chains):
        k_dec = ks[i] * jnp.exp(g_lasts[i] - gcols[i])
        state_ref[b, h] = s_olds[i] * jnp.exp(g_lasts[i]) + _mm_tn(k_dec, v_news[i])
    for i, (b, h) in enumerate(chains):
        o_h = o_hs[i]
        ms = jnp.mean(o_h * o_h, axis=-1, keepdims=True)
        o_n = o_h * lax.rsqrt(ms + NORM_EPS) * dnorm_ref[...]
        gate = _silu(z_ref[b, :, h * DV_B:(h + 1) * DV_B])
        o_ref[b, :, h * DV_B:(h + 1) * DV_B] = (o_n * gate).astype(o_ref.dtype)

    @pl.when(n == pl.num_programs(0) - 1)
    def _():
        s_out_ref[...] = state_ref[...]


GDN_SEQ_TILE = 8


def _gdn_sample_kernel(u_ref, cb_ref, small_ref, z_ref, s0_ref, wconv_ref, alog_ref, dtb_ref, dnorm_ref,
                       o_ref, conv_out_ref, s_out_ref):
    ts = GDN_SEQ_TILE
    tail = CONV_K - 1
    u_new = u_ref[...]
    y = wconv_ref[tail:tail + 1, :] * u_new
    for j in range(tail):
        y = y + wconv_ref[j:j + 1, :] * cb_ref[j]
    cv = _silu(y)
    for j in range(tail - 1):
        conv_out_ref[j] = cb_ref[j + 1]
    conv_out_ref[tail - 1] = u_new

    sm = small_ref[...]
    eg_all = jnp.exp(-jnp.exp(alog_ref[...]) * _softplus(sm + dtb_ref[...]))
    beta_all = _sigmoid(sm)
    qw = H_B * DK_B
    for h in range(H_B):
        q = _l2norm(cv[:, h * DK_B:(h + 1) * DK_B]) * (DK_B ** -0.5)
        k = _l2norm(cv[:, qw + h * DK_B:qw + (h + 1) * DK_B])
        v = cv[:, 2 * qw + h * DV_B:2 * qw + (h + 1) * DV_B]
        eg = eg_all[:, A_LANE + h:A_LANE + h + 1]
        beta = beta_all[:, B_LANE + h:B_LANE + h + 1]
        qk = jnp.sum(q * k, axis=-1, keepdims=True)
        k_t, q_t = k.T, q.T
        rows = []
        for r in range(ts):
            s_old = s0_ref[r, h]
            kc = k_t[:, r:r + 1]
            ks = jnp.sum(s_old * kc, axis=0, keepdims=True)
            qs = jnp.sum(s_old * q_t[:, r:r + 1], axis=0, keepdims=True)
            eg_r = eg[r:r + 1, :]
            v_new = beta[r:r + 1, :] * (v[r:r + 1, :] - eg_r * ks)
            rows.append(eg_r * qs + qk[r:r + 1, :] * v_new)
            s_out_ref[r, h] = s_old * eg_r + kc * v_new
        o_h = jnp.concatenate(rows, axis=0)
        ms = jnp.mean(o_h * o_h, axis=-1, keepdims=True)
        o_n = o_h * lax.rsqrt(ms + NORM_EPS) * dnorm_ref[...]
        o_ref[:, h * DV_B:(h + 1) * DV_B] = (o_n * _silu(z_ref[:, h * DV_B:(h + 1) * DV_B])).astype(o_ref.dtype)


def _gdn_sample(u, small, z, conv_buf, s0, w_conv, a_log, dt_bias, delta_norm):
    db = u.shape[0]
    ts = GDN_SEQ_TILE
    assert db % ts == 0
    tail = CONV_K - 1
    alog_row, dtb_row = _gate_rows(a_log, dt_bias)
    row = lambda w: pl.BlockSpec((ts, w), lambda i: (i, 0))
    full = lambda *shape: pl.BlockSpec(shape, lambda i: (0,) * len(shape))
    cb_spec = pl.BlockSpec((tail, ts, CONV_DIM), lambda i: (0, i, 0))
    st_spec = pl.BlockSpec((ts, H_B, DK_B, DV_B), lambda i: (i, 0, 0, 0))
    o, conv_t, s_new = pl.pallas_call(
        _gdn_sample_kernel,
        grid=(db // ts,),
        in_specs=[row(CONV_DIM), cb_spec, row(SMALL_W), row(H_B * DV_B), st_spec,
                  full(CONV_K, CONV_DIM), full(1, SMALL_W), full(1, SMALL_W), full(1, DV_B)],
        out_specs=(row(H_B * DV_B), cb_spec, st_spec),
        out_shape=(jax.ShapeDtypeStruct((db, H_B * DV_B), jnp.float32),
                   jax.ShapeDtypeStruct((tail, db, CONV_DIM), jnp.float32),
                   jax.ShapeDtypeStruct((db, H_B, DK_B, DV_B), jnp.float32)),
        compiler_params=pltpu.CompilerParams(dimension_semantics=("arbitrary",),
                                             vmem_limit_bytes=VMEM_LIMIT_BYTES),
        name="gdn_sample",
    )(u, jnp.swapaxes(conv_buf, 0, 1), small, z, s0, w_conv.astype(jnp.float32), alog_row, dtb_row,
      delta_norm.astype(jnp.float32)[None, :])
    return o, jnp.swapaxes(conv_t, 0, 1), s_new


def _gate_rows(a_log, dt_bias):
    alog_row = jnp.zeros((1, SMALL_W), jnp.float32).at[0, A_LANE:A_LANE + H_B].set(a_log.astype(jnp.float32))
    dtb_row = jnp.zeros((1, SMALL_W), jnp.float32).at[0, A_LANE:A_LANE + H_B].set(dt_bias.astype(jnp.float32))
    return alog_row, dtb_row


def _gdn_prompt(u, small, z, conv0, s0, w_conv, a_log, dt_bias, delta_norm, batch, seq):
    c = GDN_CHUNK
    assert seq % c == 0
    alog_row, dtb_row = _gate_rows(a_log, dt_bias)
    row = lambda w: pl.BlockSpec((batch, c, w), lambda i: (0, i, 0))
    full = lambda *shape: pl.BlockSpec(shape, lambda i: (0,) * len(shape))
    o, conv_new, s_new = pl.pallas_call(
        _gdn_prompt_kernel,
        grid=(seq // c,),
        in_specs=[row(CONV_DIM), row(SMALL_W), row(H_B * DV_B), full(batch, CONV_K - 1, CONV_DIM),
                  full(batch, H_B, DK_B, DV_B), full(CONV_K, CONV_DIM), full(1, SMALL_W), full(1, SMALL_W),
                  full(1, DV_B)],
        out_specs=(row(H_B * DV_B), full(batch, CONV_K - 1, CONV_DIM), full(batch, H_B, DK_B, DV_B)),
        out_shape=(jax.ShapeDtypeStruct((batch, seq, H_B * DV_B), MXU_DTYPE),
                   jax.ShapeDtypeStruct((batch, CONV_K - 1, CONV_DIM), jnp.float32),
                   jax.ShapeDtypeStruct((batch, H_B, DK_B, DV_B), jnp.float32)),
        scratch_shapes=[pltpu.VMEM((batch, SUBLANES + c, CONV_DIM), jnp.float32),
                        pltpu.VMEM((batch, H_B, DK_B, DV_B), jnp.float32)],
        compiler_params=pltpu.CompilerParams(dimension_semantics=("arbitrary",),
                                             vmem_limit_bytes=VMEM_LIMIT_BYTES),
        name="gdn_prompt",
    )(u.reshape(batch, seq, CONV_DIM), small.reshape(batch, seq, SMALL_W), z.reshape(batch, seq, H_B * DV_B),
      conv0, s0, w_conv.astype(jnp.float32), alog_row, dtb_row, delta_norm.astype(jnp.float32)[None, :])
    return o.reshape(batch * seq, H_B * DV_B), conv_new, s_new


def _merge_kernel(x_ref, oa_ref, ob_ref, gl_ref, wba_ref, wbb_ref, wout_ref, gain_ref, x1_ref, hn_ref):
    pa = _mm(oa_ref[...], wba_ref[...])
    pb = _mm(ob_ref[...], wbb_ref[...])
    mix = _sigmoid(gl_ref[:, 0:D_MODEL]) * pa + _sigmoid(gl_ref[:, D_MODEL:2 * D_MODEL]) * pb
    x1 = x_ref[...] + _mm(mix, wout_ref[...])
    x1_ref[...] = x1
    ms = jnp.mean(x1 * x1, axis=-1, keepdims=True)
    hn_ref[...] = (x1 * lax.rsqrt(ms + NORM_EPS) * gain_ref[...]).astype(hn_ref.dtype)


def _merge(x2d, o_a, o_b, gl, w_branch, w_out, norm_ffn, tm):
    n = x2d.shape[0]
    assert n % tm == 0
    row = lambda w: pl.BlockSpec((tm, w), lambda i: (i, 0))
    full = lambda *shape: pl.BlockSpec(shape, lambda i: (0,) * len(shape))
    return pl.pallas_call(
        _merge_kernel,
        grid=(n // tm,),
        in_specs=[row(D_MODEL), row(BRANCH_WIDTH), row(BRANCH_WIDTH), row(2 * D_MODEL),
                  full(BRANCH_WIDTH, D_MODEL), full(BRANCH_WIDTH, D_MODEL), full(D_MODEL, D_MODEL),
                  full(1, D_MODEL)],
        out_specs=(row(D_MODEL), row(D_MODEL)),
        out_shape=(jax.ShapeDtypeStruct((n, D_MODEL), jnp.float32),
                   jax.ShapeDtypeStruct((n, D_MODEL), MXU_DTYPE)),
        compiler_params=pltpu.CompilerParams(dimension_semantics=("arbitrary",),
                                             vmem_limit_bytes=VMEM_LIMIT_BYTES),
        name="merge",
    )(x2d, o_a, o_b, gl, w_branch[0].astype(MXU_DTYPE), w_branch[1].astype(MXU_DTYPE),
      w_out.astype(MXU_DTYPE), norm_ffn.astype(jnp.float32)[None, :])


FFN_TILE = D_FF // 2


def _ffn_kernel(hn_ref, x1_ref, wg_ref, wu_ref, wd_ref, y_ref, acc_ref):
    j = pl.program_id(1)

    @pl.when(j == 0)
    def _():
        acc_ref[...] = x1_ref[...]

    hn = hn_ref[...]
    g = jnp.dot(hn, wg_ref[...], preferred_element_type=jnp.float32)
    u = jnp.dot(hn, wu_ref[...], preferred_element_type=jnp.float32)
    acc_ref[...] += _mm(_silu(g) * u, wd_ref[...])

    @pl.when(j == pl.num_programs(1) - 1)
    def _():
        y_ref[...] = acc_ref[...]


def _ffn(hn, x1, w_gate_up, w_down, tm):
    n = hn.shape[0]
    tf = FFN_TILE
    assert n % tm == 0 and D_FF % tf == 0 and tf % LANES == 0
    nf = D_FF // tf
    wgu = w_gate_up.astype(MXU_DTYPE)
    return pl.pallas_call(
        _ffn_kernel,
        grid=(n // tm, nf),
        in_specs=[pl.BlockSpec((tm, D_MODEL), lambda i, j: (i, 0)),
                  pl.BlockSpec((tm, D_MODEL), lambda i, j: (i, 0)),
                  pl.BlockSpec((D_MODEL, tf), lambda i, j: (0, j)),
                  pl.BlockSpec((D_MODEL, tf), lambda i, j: (0, j + nf)),
                  pl.BlockSpec((tf, D_MODEL), lambda i, j: (j, 0))],
        out_specs=pl.BlockSpec((tm, D_MODEL), lambda i, j: (i, 0)),
        out_shape=jax.ShapeDtypeStruct((n, D_MODEL), jnp.float32),
        scratch_shapes=[pltpu.VMEM((tm, D_MODEL), jnp.float32)],
        compiler_params=pltpu.CompilerParams(dimension_semantics=("arbitrary", "arbitrary"),
                                             vmem_limit_bytes=VMEM_LIMIT_BYTES),
        name="ffn",
    )(hn, x1, wgu, wgu, w_down.astype(MXU_DTYPE))


IN_PROJ_TILE = K_TILE
MERGE_TILE = 512
FFN_ROW_TILE = 512


def _layer(x_p, x_s, cache_k, cache_v, cache_ik, conv_s, delta_s, page_table, norm_mix, w_in, q_norm, k_norm,
           w_conv, a_log, dt_bias, delta_norm, w_branch, w_out, norm_ffn, w_gate_up, w_down):
    b, t, d = x_p.shape
    db = x_s.shape[0]
    past = page_table.shape[1] * PAGE_SIZE
    kw = N_KV_A * HEAD_DIM_A
    w_packed = _pack_w_in(w_in)

    xp2 = x_p.reshape(b * t, d)
    tm = IN_PROJ_TILE
    assert t % tm == 0
    kv, ik, small, u, z, gl, qt, iqt, wt, kb, ikb, vt = _in_proj(
        xp2, _rope_tables(jnp.arange(t)), t // tm, tm, norm_mix, w_packed, q_norm, k_norm, key_major=True)
    o_a = _dsa_prompt(qt, iqt, wt, kb, ikb, vt, b, t)
    conv0 = jnp.zeros((b, CONV_K - 1, CONV_DIM), jnp.float32)
    delta0 = jnp.zeros((b, H_B, DK_B, DV_B), jnp.float32)
    o_b, conv_p, delta_p = _gdn_prompt(u, small, z, conv0, delta0, w_conv, a_log, dt_bias, delta_norm, b, t)
    x1, hn = _merge(xp2, o_a, o_b, gl, w_branch, w_out, norm_ffn, min(MERGE_TILE, b * t))
    y_p = _ffn(hn, x1, w_gate_up, w_down, min(FFN_ROW_TILE, b * t)).reshape(b, t, d)
    kv5 = kv.reshape(b, 2, N_KV_A, HEAD_DIM_A, t)
    st_p = (jnp.transpose(kv5[:, 0], (0, 3, 1, 2)), jnp.transpose(kv5[:, 1], (0, 3, 1, 2)),
            jnp.swapaxes(ik, 1, 2), conv_p, delta_p)

    xs2 = x_s.reshape(db, d)
    kv, ik, small, u, z, gl, qexp, iqhm = _in_proj(
        xs2, _rope_tables(jnp.full((db,), past, jnp.int32)), 1, db, norm_mix, w_packed, q_norm, k_norm,
        key_major=False)
    o_a = _dsa_sample(qexp, iqhm, small, ik, kv, cache_k, cache_v, cache_ik, page_table)
    o_b, conv_n, delta_n = _gdn_sample(u, small, z, conv_s, delta_s, w_conv, a_log, dt_bias, delta_norm)
    x1, hn = _merge(xs2, o_a, o_b, gl, w_branch, w_out, norm_ffn, db)
    y_s = _ffn(hn, x1, w_gate_up, w_down, db).reshape(db, 1, d)
    st_s = (kv[:, 0:kw].reshape(db, 1, N_KV_A, HEAD_DIM_A), kv[:, kw:2 * kw].reshape(db, 1, N_KV_A, HEAD_DIM_A),
            ik.reshape(db, 1, IDX_DIM), conv_n, delta_n)
    return y_p, y_s, st_p, st_s


def kernel(x_prompt, x_sample, cache_k, cache_v, cache_idx_k, state_conv, state_delta, page_table,
           norm_mix, w_in, q_norm, k_norm, w_conv, a_log, dt_bias, delta_norm, w_branch, w_out,
           norm_ffn, w_gate_up, w_down):
    assert x_sample.shape[1] == 1, "the sample group decodes one token per sequence"
    y_p, y_s = x_prompt, x_sample
    new_p, new_s = [], []
    for l in range(w_in.shape[0]):
        y_p, y_s, st_p, st_s = _layer(
            y_p, y_s, cache_k[l], cache_v[l], cache_idx_k[l], state_conv[l], state_delta[l], page_table,
            norm_mix[l], w_in[l], q_norm[l], k_norm[l], w_conv[l], a_log[l], dt_bias[l], delta_norm[l],
            w_branch[l], w_out[l], norm_ffn[l], w_gate_up[l], w_down[l])
        new_p.append(st_p)
        new_s.append(st_s)
    k_p, v_p, ik_p, conv_p, delta_p = [jnp.stack(a) for a in zip(*new_p)]
    k_s, v_s, ik_s, conv_s, delta_s = [jnp.stack(a) for a in zip(*new_s)]
    return (y_p, y_s, k_p, v_p, ik_p, conv_p, delta_p, k_s, v_s, ik_s, conv_s, delta_s)
```

```python
import functools
import math

import jax
import jax.numpy as jnp
import numpy as np
from jax import lax
from jax.experimental import pallas as pl
from jax.experimental.pallas import tpu as pltpu

D_MODEL = 1024
PAGE_SIZE = 128
N_HEADS_A = 8
N_KV_A = 2
HEAD_DIM_A = 64
GROUP_A = N_HEADS_A // N_KV_A
IDX_HEADS = 8
IDX_DIM = 64
TOPK_MAX = 256
ROPE_THETA = 500000.0
H_B = 4
DK_B = 128
DV_B = 128
CONV_K = 4
CONV_DIM = 2 * H_B * DK_B + H_B * DV_B
BRANCH_WIDTH = N_HEADS_A * HEAD_DIM_A
D_FF = -(-8 * D_MODEL // (3 * 256)) * 256
NORM_EPS = 1e-6
NEG_INF = -1e30
IN_SIZES = (N_HEADS_A * HEAD_DIM_A, N_KV_A * HEAD_DIM_A, N_KV_A * HEAD_DIM_A,
            IDX_HEADS * IDX_DIM, IDX_DIM, IDX_HEADS,
            CONV_DIM, H_B, H_B, H_B * DV_B, 2 * D_MODEL)

LANES = 128
SUBLANES = 8
VMEM_LIMIT_BYTES = 56 * 1024 * 1024

MXU_DTYPE = jnp.bfloat16

SMALL_W = LANES
SEG_A = BRANCH_WIDTH + 2 * N_KV_A * HEAD_DIM_A + IDX_HEADS * IDX_DIM + IDX_DIM
SEG_A_PAD = -(-SEG_A // LANES) * LANES
OFF_SMALL = SEG_A_PAD
OFF_U = OFF_SMALL + SMALL_W
OFF_Z = OFF_U + CONV_DIM
OFF_GL = OFF_Z + H_B * DV_B
D_IN_PACKED = OFF_GL + 2 * D_MODEL


def _mm(a, b):
    return jnp.dot(a.astype(MXU_DTYPE), b.astype(MXU_DTYPE), preferred_element_type=jnp.float32)


def _mm_nt(a, b):
    return lax.dot_general(a.astype(MXU_DTYPE), b.astype(MXU_DTYPE), (((1,), (1,)), ((), ())),
                           preferred_element_type=jnp.float32)


def _mm_tn(a, b):
    return lax.dot_general(a.astype(MXU_DTYPE), b.astype(MXU_DTYPE), (((0,), (0,)), ((), ())),
                           preferred_element_type=jnp.float32)


def _split3(x):
    x = x.astype(jnp.float32)
    h = x.astype(MXU_DTYPE)
    r = x - h.astype(jnp.float32)
    m = r.astype(MXU_DTYPE)
    l = (r - m.astype(jnp.float32)).astype(MXU_DTYPE)
    return h, m, l


def _sigmoid(x):
    return 1.0 / (1.0 + jnp.exp(-x))


def _silu(x):
    return x * _sigmoid(x)


def _softplus(x):
    return jnp.maximum(x, 0.0) + jnp.log(1.0 + jnp.exp(-jnp.abs(x)))


def _rope_tile(x, cos_t, sin_lo, sin_hi):
    half = HEAD_DIM_A // 8
    up = pltpu.roll(x, LANES - half, 1)
    dn = pltpu.roll(x, half, 1)
    return x * cos_t + up * sin_lo + dn * sin_hi


def _in_proj_kernel(key_major, x_ref, gain_ref, w_ref, bd_ref, qg_ref, kg_ref, cos_ref, slo_ref, shi_ref,
                    kv_ref, ik_ref, small_ref, u_ref, z_ref, gl_ref, *attn_refs):
    x = x_ref[...]
    ms = jnp.mean(x * x, axis=-1, keepdims=True)
    xn = (x * lax.rsqrt(ms + NORM_EPS) * gain_ref[...]).astype(MXU_DTYPE)

    cos_t, sin_lo, sin_hi = cos_ref[...], slo_ref[...], shi_ref[...]
    lane = lax.broadcasted_iota(jnp.int32, (x.shape[0], LANES), 1)
    lo_half = lane < HEAD_DIM_A

    def head_rms(t, gain):
        tt = t * t
        hi = tt.astype(MXU_DTYPE)
        lo = (tt - hi.astype(jnp.float32)).astype(MXU_DTYPE)
        bd = bd_ref[0:t.shape[1], 0:t.shape[1]]
        msq = (jnp.dot(hi, bd, preferred_element_type=jnp.float32)
               + jnp.dot(lo, bd, preferred_element_type=jnp.float32))
        return t * lax.rsqrt(msq + NORM_EPS) * gain

    q = jnp.dot(xn, w_ref[:, 0:BRANCH_WIDTH], preferred_element_type=jnp.float32)
    q = head_rms(q, qg_ref[...])
    if key_major:
        qt_ref, iqt_ref, wt_ref, kb_ref, ikb_ref, vt_ref = attn_refs
        n_qb = x.shape[0] // Q_TILE
        q_scale = HEAD_DIM_A ** -0.5 * math.log2(math.e)
    else:
        qexp_ref, iqhm_ref = attn_refs
        q_scale = HEAD_DIM_A ** -0.5
    for p in range(BRANCH_WIDTH // LANES):
        t = _rope_tile(q[:, p * LANES:(p + 1) * LANES], cos_t, sin_lo, sin_hi) * q_scale
        t_sw = pltpu.roll(t, HEAD_DIM_A, 1)
        for e in range(2):
            h = 2 * p + e
            n = h // GROUP_A
            src = t if e == n else t_sw
            keep = lo_half if n == 0 else jnp.logical_not(lo_half)
            qe = jnp.where(keep, src, 0.0)
            if key_major:
                qe_t = qe.T
                for j in range(n_qb):
                    qt_ref[j, :, h * Q_TILE:(h + 1) * Q_TILE] = qe_t[:, j * Q_TILE:(j + 1) * Q_TILE].astype(qt_ref.dtype)
            else:
                qexp_ref[h] = qe.astype(qexp_ref.dtype)

    c0 = BRANCH_WIDTH
    kw = N_KV_A * HEAD_DIM_A
    k = jnp.dot(xn, w_ref[:, c0:c0 + kw], preferred_element_type=jnp.float32)
    k = _rope_tile(head_rms(k, kg_ref[...]), cos_t, sin_lo, sin_hi)
    v = jnp.dot(xn, w_ref[:, c0 + kw:c0 + 2 * kw], preferred_element_type=jnp.float32)
    if key_major:
        v_t = v.T
        kv_ref[0, 0] = k.T
        kv_ref[0, 1] = v_t
        kb_ref[...] = k.astype(kb_ref.dtype)
        vt_ref[0, 0:kw, :] = v_t.astype(vt_ref.dtype)
        vt_ref[0, kw:kw + ONES_ROWS, :] = jnp.ones((ONES_ROWS, x.shape[0]), vt_ref.dtype)
    else:
        kv_ref[:, 0:kw] = k
        kv_ref[:, kw:2 * kw] = v

    c1 = c0 + 2 * kw
    iqw = IDX_HEADS * IDX_DIM
    iq = jnp.dot(xn, w_ref[:, c1:c1 + iqw], preferred_element_type=jnp.float32)
    for p in range(iqw // LANES):
        t = _rope_tile(iq[:, p * LANES:(p + 1) * LANES], cos_t, sin_lo, sin_hi)
        if key_major:
            t_t = t.T
            for e in range(2):
                h = 2 * p + e
                for j in range(n_qb):
                    iqt_ref[j, :, h * Q_TILE:(h + 1) * Q_TILE] = (
                        t_t[e * IDX_DIM:(e + 1) * IDX_DIM, j * Q_TILE:(j + 1) * Q_TILE].astype(iqt_ref.dtype))
        else:
            t = t.astype(iqhm_ref.dtype)
            iqhm_ref[2 * p] = t[:, 0:IDX_DIM]
            iqhm_ref[2 * p + 1] = t[:, IDX_DIM:2 * IDX_DIM]

    c2 = c1 + iqw
    ik_sm = jnp.dot(xn, w_ref[:, c2:c2 + 2 * LANES], preferred_element_type=jnp.float32)
    ik_tile = _rope_tile(ik_sm[:, 0:LANES], cos_t, sin_lo, sin_hi)
    ik = ik_tile[:, 0:IDX_DIM]
    small = ik_sm[:, LANES:2 * LANES]
    small_ref[...] = small
    if not key_major:
        ik_ref[...] = ik
    else:
        ik_ref[0] = ik_tile.T[0:IDX_DIM]
        ikb_ref[...] = ik.astype(ikb_ref.dtype)
        small_t = small.T
        for j in range(n_qb):
            wt_ref[j] = small_t[0:IDX_HEADS, j * Q_TILE:(j + 1) * Q_TILE]

    u_ref[...] = jnp.dot(xn, w_ref[:, OFF_U:OFF_U + CONV_DIM], preferred_element_type=jnp.float32)
    z_ref[...] = jnp.dot(xn, w_ref[:, OFF_Z:OFF_Z + H_B * DV_B], preferred_element_type=jnp.float32)
    gl_ref[...] = jnp.dot(xn, w_ref[:, OFF_GL:OFF_GL + 2 * D_MODEL], preferred_element_type=jnp.float32)


def _pack_w_in(w_in):
    pts = np.cumsum(IN_SIZES)[:-1].tolist()
    q, k, v, iq, ik, iw, u, a, b, z, gl = jnp.split(w_in, pts, axis=-1)
    d = w_in.shape[0]
    seg_a = jnp.concatenate([q, k, v, iq, ik, jnp.zeros((d, SEG_A_PAD - SEG_A), w_in.dtype)], axis=1)
    small = jnp.concatenate([iw, a, b, jnp.zeros((d, SMALL_W - IDX_HEADS - 2 * H_B), w_in.dtype)], axis=1)
    return jnp.concatenate([seg_a, small, u, z, gl], axis=1).astype(MXU_DTYPE)


def _rope_tables(pos):
    rot = HEAD_DIM_A // 4
    half = rot // 2
    inv_freq = ROPE_THETA ** (-jnp.arange(half, dtype=jnp.float32) / half)
    ang = pos.astype(jnp.float32)[:, None] * inv_freq[None, :]
    cos, sin = jnp.cos(ang), jnp.sin(ang)
    rows = pos.shape[0]
    one = jnp.ones((rows, HEAD_DIM_A - rot), jnp.float32)
    zero = jnp.zeros((rows, HEAD_DIM_A - rot), jnp.float32)
    zh = jnp.zeros((rows, half), jnp.float32)
    cos_h = jnp.concatenate([cos, cos, one], axis=1)
    slo_h = jnp.concatenate([-sin, zh, zero], axis=1)
    shi_h = jnp.concatenate([zh, sin, zero], axis=1)
    rep = LANES // HEAD_DIM_A
    return jnp.tile(cos_h, (1, rep)), jnp.tile(slo_h, (1, rep)), jnp.tile(shi_h, (1, rep))


def _in_proj(x2d, pos_tables, n_table_blocks, tm, norm_mix, w_packed, q_norm, k_norm, key_major):
    n = x2d.shape[0]
    assert n % tm == 0 and (not key_major or tm == K_TILE)
    cos_t, sin_lo, sin_hi = pos_tables
    bd = jnp.kron(jnp.eye(BRANCH_WIDTH // HEAD_DIM_A, dtype=jnp.float32),
                  jnp.full((HEAD_DIM_A, HEAD_DIM_A), 1.0 / HEAD_DIM_A, jnp.float32)).astype(MXU_DTYPE)
    qg = jnp.tile(q_norm.astype(jnp.float32), BRANCH_WIDTH // HEAD_DIM_A)[None, :]
    kg = jnp.tile(k_norm.astype(jnp.float32), N_KV_A)[None, :]
    kw = N_KV_A * HEAD_DIM_A
    row = lambda w: pl.BlockSpec((tm, w), lambda i: (i, 0))
    full = lambda a: pl.BlockSpec(a.shape, lambda i: (0,) * a.ndim)
    tab = pl.BlockSpec((tm, LANES), lambda i: (i % n_table_blocks, 0))
    if key_major:
        seq = n_table_blocks * tm
        nt = n_table_blocks
        kv_shape, ik_shape = (n // seq, 2, kw, seq), (n // seq, IDX_DIM, seq)
        kv_spec = pl.BlockSpec((1, 2, kw, tm), lambda i: (i // nt, 0, 0, i % nt))
        ik_spec = pl.BlockSpec((1, IDX_DIM, tm), lambda i: (i // nt, 0, i % nt))
    else:
        kv_shape, ik_shape = (n, 2 * kw), (n, IDX_DIM)
        kv_spec, ik_spec = row(2 * kw), row(IDX_DIM)
    out_shape = [
        jax.ShapeDtypeStruct(kv_shape, jnp.float32),
        jax.ShapeDtypeStruct(ik_shape, jnp.float32),
        jax.ShapeDtypeStruct((n, SMALL_W), jnp.float32),
        jax.ShapeDtypeStruct((n, CONV_DIM), jnp.float32),
        jax.ShapeDtypeStruct((n, H_B * DV_B), jnp.float32),
        jax.ShapeDtypeStruct((n, 2 * D_MODEL), jnp.float32),
    ]
    out_specs = [kv_spec, ik_spec, row(SMALL_W), row(CONV_DIM), row(H_B * DV_B), row(2 * D_MODEL)]
    if key_major:
        n_qb = tm // Q_TILE
        blk = lambda r, w: pl.BlockSpec((n_qb, r, w), lambda i: (i, 0, 0))
        out_shape += [
            jax.ShapeDtypeStruct((n // Q_TILE, LANES, N_HEADS_A * Q_TILE), MXU_DTYPE),
            jax.ShapeDtypeStruct((n // Q_TILE, IDX_DIM, IDX_HEADS * Q_TILE), MXU_DTYPE),
            jax.ShapeDtypeStruct((n // Q_TILE, IDX_HEADS, Q_TILE), jnp.float32),
            jax.ShapeDtypeStruct((n, kw), MXU_DTYPE),
            jax.ShapeDtypeStruct((n, IDX_DIM), MXU_DTYPE),
            jax.ShapeDtypeStruct((n // K_TILE, kw + ONES_ROWS, K_TILE), MXU_DTYPE),
        ]
        out_specs += [blk(LANES, N_HEADS_A * Q_TILE), blk(IDX_DIM, IDX_HEADS * Q_TILE), blk(IDX_HEADS, Q_TILE),
                      row(kw), row(IDX_DIM), pl.BlockSpec((1, kw + ONES_ROWS, K_TILE), lambda i: (i, 0, 0))]
    else:
        out_shape += [jax.ShapeDtypeStruct((N_HEADS_A, n, LANES), MXU_DTYPE),
                      jax.ShapeDtypeStruct((IDX_HEADS, n, IDX_DIM), MXU_DTYPE)]
        out_specs += [pl.BlockSpec((N_HEADS_A, tm, LANES), lambda i: (0, i, 0)),
                      pl.BlockSpec((IDX_HEADS, tm, IDX_DIM), lambda i: (0, i, 0))]
    return pl.pallas_call(
        functools.partial(_in_proj_kernel, key_major),
        grid=(n // tm,),
        in_specs=[row(D_MODEL), full(norm_mix[None, :]), full(w_packed), full(bd), full(qg), full(kg),
                  tab, tab, tab],
        out_specs=tuple(out_specs),
        out_shape=tuple(out_shape),
        compiler_params=pltpu.CompilerParams(dimension_semantics=("arbitrary",),
                                             vmem_limit_bytes=VMEM_LIMIT_BYTES),
        name="in_proj",
    )(x2d, norm_mix[None, :].astype(jnp.float32), w_packed, bd, qg, kg, cos_t, sin_lo, sin_hi)


_INT_MAG = 0x7FFFFFFF


def _f32_key(x):
    b = lax.bitcast_convert_type(x, jnp.int32)
    return b ^ (lax.shift_right_arithmetic(b, 31) & _INT_MAG)


def _key_f32(k):
    b = k ^ (lax.shift_right_arithmetic(k, 31) & _INT_MAG)
    return lax.bitcast_convert_type(b, jnp.float32)


def _topk_threshold(count_ge, count_tie, row_min, row_max, n_adm, topk, n_keys, zero_counts=None,
                    fixed_steps=18, linear_steps=24):
    kf = jnp.float32(topk)
    need = n_adm > topk
    lo_k = _f32_key(row_min)
    hi_k = _f32_key(row_max) + 1
    thr = jnp.where(need, row_min, -jnp.inf)
    done = jnp.where(need, 0, 1).astype(jnp.int32)

    zero = jnp.zeros_like(row_min)
    if zero_counts is None:
        zero_counts = (count_ge(zero), count_tie(zero, jnp.full_like(lo_k, n_keys + 1)))
    ge0 = zero_counts[0]
    gt0 = ge0 - zero_counts[1]
    live = jnp.logical_and(need, lo_k < hi_k - 1)
    hit0 = jnp.logical_and(live, ge0 == kf)
    tie0 = jnp.logical_and(live, jnp.logical_and(gt0 < kf, ge0 > kf))
    thr = jnp.where(jnp.logical_or(hit0, tie0), zero, thr)
    done = jnp.where(jnp.logical_or(hit0, tie0), 1, done)
    zero_k = _f32_key(zero)
    lo_k = jnp.where(jnp.logical_and(live, ge0 > kf), jnp.maximum(lo_k, zero_k), lo_k)
    hi_k = jnp.where(jnp.logical_and(live, ge0 < kf), jnp.minimum(hi_k, zero_k), hi_k)
    state = (lo_k, hi_k, thr, jnp.where(tie0, gt0, zero), done, jnp.where(tie0, 1, 0).astype(jnp.int32))

    def step(linear, st):
        lo_k, hi_k, thr, cnt_hi, done, tie = st
        adjacent = hi_k == lo_k + 1
        lo_f, hi_f = _key_f32(lo_k), _key_f32(hi_k)
        mid_lin = _f32_key(lo_f + 0.5 * (hi_f - lo_f))
        mid_lin = jnp.minimum(jnp.maximum(mid_lin, lo_k + 1), hi_k - 1)
        mid_int = (lo_k & hi_k) + lax.shift_right_arithmetic(lo_k ^ hi_k, 1)
        mid = mid_lin if linear is True else jnp.where(linear, mid_lin, mid_int)
        mid_f = _key_f32(mid)
        cnt = count_ge(mid_f)
        live = jnp.logical_and(done == 0, jnp.logical_not(adjacent))
        hit = jnp.logical_and(live, cnt == kf)
        up = jnp.logical_and(live, cnt > kf)
        dn = jnp.logical_and(live, cnt < kf)
        new_tie = jnp.logical_and(done == 0, adjacent)
        thr = jnp.where(hit, mid_f, jnp.where(new_tie, lo_f, thr))
        tie = jnp.where(new_tie, 1, tie)
        done = jnp.where(jnp.logical_or(hit, new_tie), 1, done)
        lo_k = jnp.where(up, mid, lo_k)
        hi_k = jnp.where(dn, mid, hi_k)
        cnt_hi = jnp.where(dn, cnt, cnt_hi)
        return (lo_k, hi_k, thr, cnt_hi, done, tie)

    state = lax.fori_loop(0, fixed_steps, lambda _, st: step(True, st), state)

    def cond(st):
        it, active = st[0], st[1]
        return jnp.logical_and(it < 80, active > 0)

    def body(st):
        it = st[0]
        new = step(it < linear_steps, st[2:])
        return (it + 1, jnp.max(1 - new[4])) + new

    st = lax.while_loop(cond, body, (jnp.int32(fixed_steps), jnp.max(1 - state[4])) + state)
    thr, cnt_hi, tie = st[4], st[5], st[7]

    need_ties = kf - cnt_hi
    n_bits = max(1, int(math.ceil(math.log2(n_keys + 1))))
    any_tie = jnp.max(tie)

    def tie_body(_, lm):
        lo_m, hi_m = lm
        mid = lax.shift_right_arithmetic(lo_m + hi_m, 1)
        ge = count_tie(thr, mid) >= need_ties
        return jnp.where(ge, lo_m, mid), jnp.where(ge, mid, hi_m)

    lo_m0 = jnp.zeros_like(lo_k)
    hi_m0 = jnp.full_like(lo_k, n_keys)
    _, hi_m = lax.fori_loop(0, jnp.where(any_tie > 0, n_bits + 1, 0), tie_body, (lo_m0, hi_m0))
    cut = jnp.where(tie > 0, hi_m, n_keys + 1)
    return thr, cut


Q_TILE = 128
K_TILE = 256
K_UNROLL = 4
SCAN_UNROLL = 4
ONES_ROWS = 16


def _dsa_prompt_kernel(topk, qt_ref, iqt_ref, wt_ref, kb_ref, ikb_ref, vt_ref, o_ref, sc_ref, acc_ref, m_ref):
    i = pl.program_id(1)
    tq, kc = Q_TILE, K_TILE
    n_keys = sc_ref.shape[0] * kc
    nchunk = (i + 2) // 2
    qpos = i * tq + lax.broadcasted_iota(jnp.int32, (kc, tq), 1)
    krow = lax.broadcasted_iota(jnp.int32, (kc, tq), 0)
    qpos8 = qpos[0:SUBLANES]

    def col_reduce(x, op):
        return op(x.reshape(kc // SUBLANES, SUBLANES, tq), axis=0)

    def all_rows(x, op2):
        for shift in (4, 2, 1):
            x = op2(x, pltpu.roll(x, shift, 0))
        return x

    def tile_loop(first, rest, init, unroll):
        def trip(t, carry):
            heads = [first(t * unroll + sub) for sub in range(unroll)]
            for sub in range(unroll):
                carry = rest(t * unroll + sub, heads[sub], carry)
            return carry
        full = nchunk // unroll
        carry = lax.fori_loop(0, full, trip, init)
        return lax.fori_loop(full * unroll, nchunk, lambda c, carry: rest(c, first(c), carry), carry)

    def key_rows(c):
        return pl.ds(pl.multiple_of(c * kc, kc), kc)

    w = wt_ref[0]
    iqt = iqt_ref[0]
    s_scale = IDX_DIM ** -0.5 * IDX_HEADS ** -0.5

    def score_dots(c):
        return jnp.dot(ikb_ref[key_rows(c), :], iqt, preferred_element_type=jnp.float32)

    def score_tile(c, d, carry):
        mn, mx, ge0, eq0 = carry
        s = w[0:1] * jnp.maximum(d[:, 0:tq], 0.0)
        for h in range(1, IDX_HEADS):
            s = s + w[h:h + 1] * jnp.maximum(d[:, h * tq:(h + 1) * tq], 0.0)
        s = s * s_scale
        adm = c * kc + krow <= qpos
        sc_ref[c] = jnp.where(adm, s, NEG_INF)
        return (jnp.minimum(mn, col_reduce(jnp.where(adm, s, jnp.inf), jnp.min)),
                jnp.maximum(mx, col_reduce(jnp.where(adm, s, -jnp.inf), jnp.max)),
                ge0 + col_reduce(jnp.where(jnp.logical_and(adm, s >= 0.0), 1.0, 0.0), jnp.sum),
                eq0 + col_reduce(jnp.where(jnp.logical_and(adm, s == 0.0), 1.0, 0.0), jnp.sum))

    stat = lambda v: jnp.full((SUBLANES, tq), v, jnp.float32)
    mn, mx, ge0, eq0 = tile_loop(score_dots, score_tile, (stat(jnp.inf), stat(-jnp.inf), stat(0.0), stat(0.0)),
                                 K_UNROLL)

    def scan(body, init):
        return tile_loop(lambda c: sc_ref[c], body, init, SCAN_UNROLL)

    def count(pred):
        def body(c, s, acc):
            return acc + col_reduce(jnp.where(pred(s, c * kc + krow), 1.0, 0.0), jnp.sum)
        return all_rows(scan(body, jnp.zeros((SUBLANES, tq), jnp.float32)), jnp.add)

    def count_ge(c):
        return count(lambda s, kpos: s >= c[0:1])

    def count_tie(v, m):
        return count(lambda s, kpos: jnp.logical_and(s == v[0:1], kpos < m[0:1]))

    thr, cut = _topk_threshold(count_ge, count_tie, all_rows(mn, jnp.minimum), all_rows(mx, jnp.maximum),
                               qpos8 + 1, topk, n_keys,
                               zero_counts=(all_rows(ge0, jnp.add), all_rows(eq0, jnp.add)))
    thr_row, cut_row = thr[0:1], cut[0:1]

    m_ref[...] = jnp.full(m_ref.shape, 0.5 * NEG_INF, jnp.float32)
    acc_ref[...] = jnp.zeros(acc_ref.shape, jnp.float32)
    qt = qt_ref[0]
    kw = N_KV_A * HEAD_DIM_A

    def logits(c):
        return jnp.dot(kb_ref[key_rows(c), :], qt, preferred_element_type=jnp.float32)

    def attend_group(cs, lgs):
        biases = []
        for c in cs:
            s = sc_ref[c]
            kpos = c * kc + krow
            sel = jnp.logical_or(s > thr_row, jnp.logical_and(s == thr_row, kpos < cut_row))
            biases.append(jnp.where(jnp.logical_and(sel, kpos <= qpos), 0.0, NEG_INF))
        ps, alphas = [], []
        for h in range(N_HEADS_A):
            cols = slice(h * tq, (h + 1) * tq)
            lghs = [lg[:, cols] + bias for lg, bias in zip(lgs, biases)]
            tile_max = functools.reduce(jnp.maximum, [col_reduce(lgh, jnp.max) for lgh in lghs])
            m_old = m_ref[:, cols]
            m_new = jnp.maximum(m_old, all_rows(tile_max, jnp.maximum))
            alphas.append(jnp.exp2(m_old - m_new)[0:1])
            ps.append(jnp.concatenate([jnp.exp2(lgh - m_new[0:1]).astype(MXU_DTYPE) for lgh in lghs], axis=0))
            m_ref[:, cols] = m_new
        vt = jnp.concatenate([vt_ref[c] for c in cs], axis=1)
        pv = jnp.dot(vt, jnp.concatenate(ps, axis=1), preferred_element_type=jnp.float32)
        acc_ref[...] = acc_ref[...] * jnp.concatenate(alphas, axis=1) + pv

    def attend_trip(t, carry):
        cs = [t * K_UNROLL + sub for sub in range(K_UNROLL)]
        lgs = [logits(c) for c in cs]
        for j in range(0, K_UNROLL, 2):
            attend_group(cs[j:j + 2], lgs[j:j + 2])
        return carry

    def attend_single(c, carry):
        attend_group([c], [logits(c)])
        return carry

    full_trips = nchunk // K_UNROLL
    lax.fori_loop(0, full_trips, attend_trip, 0)
    lax.fori_loop(full_trips * K_UNROLL, nchunk, attend_single, 0)

    acc = acc_ref[...]
    o_t = acc[0:kw] / acc[kw:kw + 1]
    for p in range(N_HEADS_A // 2):
        n = (2 * p) // GROUP_A
        pair = jnp.concatenate([o_t[n * HEAD_DIM_A:(n + 1) * HEAD_DIM_A, (2 * p + e) * tq:(2 * p + e + 1) * tq]
                                for e in range(2)], axis=0)
        o_ref[:, p * LANES:(p + 1) * LANES] = pair.T.astype(o_ref.dtype)


def _dsa_prompt(qt, iqt, wt, kb, ikb, vt, batch, seq):
    tq, kc = Q_TILE, K_TILE
    assert seq % kc == 0
    nq = seq // tq
    nk = seq // kc
    n = batch * seq
    kw = N_KV_A * HEAD_DIM_A
    topk = min(TOPK_MAX, seq // 4)
    return pl.pallas_call(
        functools.partial(_dsa_prompt_kernel, topk),
        grid=(batch, nq),
        in_specs=[
            pl.BlockSpec((1, LANES, N_HEADS_A * tq), lambda b, i: (b * nq + i, 0, 0)),
            pl.BlockSpec((1, IDX_DIM, IDX_HEADS * tq), lambda b, i: (b * nq + i, 0, 0)),
            pl.BlockSpec((1, IDX_HEADS, tq), lambda b, i: (b * nq + i, 0, 0)),
            pl.BlockSpec((seq, kw), lambda b, i: (b, 0)),
            pl.BlockSpec((seq, IDX_DIM), lambda b, i: (b, 0)),
            pl.BlockSpec((nk, kw + ONES_ROWS, kc), lambda b, i: (b, 0, 0)),
        ],
        out_specs=pl.BlockSpec((tq, BRANCH_WIDTH), lambda b, i: (b * nq + i, 0)),
        out_shape=jax.ShapeDtypeStruct((n, BRANCH_WIDTH), MXU_DTYPE),
        scratch_shapes=[
            pltpu.VMEM((nk, kc, tq), jnp.float32),
            pltpu.VMEM((kw + ONES_ROWS, N_HEADS_A * tq), jnp.float32),
            pltpu.VMEM((SUBLANES, N_HEADS_A * tq), jnp.float32),
        ],
        compiler_params=pltpu.CompilerParams(dimension_semantics=("arbitrary", "arbitrary"),
                                             vmem_limit_bytes=VMEM_LIMIT_BYTES),
        name="dsa_prompt",
    )(qt, iqt, wt, kb, ikb, vt)


def _page_copy(pt_ref, cache_ref, buf_ref, sem_ref, seq, page, slot):
    lanes = pl.ds(page * PAGE_SIZE, PAGE_SIZE)
    return pltpu.make_async_copy(cache_ref.at[pt_ref[seq, page]], buf_ref.at[slot, :, lanes], sem_ref.at[slot])


def _pages_start(pt_ref, cache_ref, buf_ref, sem_ref, seq, slot, n_pages):
    for p in range(n_pages):
        _page_copy(pt_ref, cache_ref, buf_ref, sem_ref, seq, p, slot).start()


def _pages_wait(pt_ref, cache_ref, buf_ref, sem_ref, seq, slot, n_pages):
    for p in range(n_pages):
        _page_copy(pt_ref, cache_ref, buf_ref, sem_ref, seq, p, slot).wait()


def _dsa_sample_score_kernel(topk, n_pages, pt_ref, iq_ref, iw_ref, iknew_ref, cache_ik_ref,
                             sc_ref, thr_ref, cut_ref, ikbuf_ref, sem_ref):
    s = pl.program_id(0)
    n_seq = pl.num_programs(0)
    past = n_pages * PAGE_SIZE
    n_tiles = sc_ref.shape[0]
    slot = s % 2

    @pl.when(s == 0)
    def _():
        _pages_start(pt_ref, cache_ik_ref, ikbuf_ref, sem_ref, 0, 0, n_pages)

    @pl.when(s + 1 < n_seq)
    def _():
        _pages_start(pt_ref, cache_ik_ref, ikbuf_ref, sem_ref, s + 1, 1 - slot, n_pages)

    _pages_wait(pt_ref, cache_ik_ref, ikbuf_ref, sem_ref, s, slot, n_pages)

    iq = iq_ref[0]
    w = iw_ref[0]
    s_scale = IDX_DIM ** -0.5 * IDX_HEADS ** -0.5
    d = _mm(iq, ikbuf_ref[slot])
    srow = jnp.sum(w * jnp.maximum(d, 0.0), axis=0, keepdims=True) * s_scale
    for j in range(n_pages):
        sc_ref[j, pl.ds(s, 1), :] = srow[:, j * LANES:(j + 1) * LANES]
    ik_new = iknew_ref[0].astype(MXU_DTYPE).astype(jnp.float32)
    d_self = jnp.sum(iq.astype(jnp.float32) * ik_new, axis=1, keepdims=True)
    s_self = jnp.sum(w * jnp.maximum(d_self, 0.0), axis=0, keepdims=True) * s_scale
    lane1 = lax.broadcasted_iota(jnp.int32, (1, LANES), 1)
    sc_ref[n_tiles - 1, pl.ds(s, 1), :] = jnp.where(lane1 == 0, s_self, NEG_INF)

    @pl.when(s == n_seq - 1)
    def _():
        rows = sc_ref.shape[1]
        lane = lax.broadcasted_iota(jnp.int32, (rows, LANES), 1)

        def count(pred):
            def body(j, acc):
                return acc + jnp.where(pred(sc_ref[j], j * LANES + lane), 1.0, 0.0)
            acc = lax.fori_loop(0, n_tiles, body, jnp.zeros((rows, LANES), jnp.float32))
            return jnp.broadcast_to(jnp.sum(acc, axis=1, keepdims=True), (rows, LANES))

        def count_ge(c):
            return count(lambda t, kpos: t >= c)

        def count_tie(v, m):
            return count(lambda t, kpos: jnp.logical_and(t == v, kpos < m))

        def minmax(j, mm):
            t = sc_ref[j]
            adm = j * LANES + lane <= past
            return (jnp.minimum(mm[0], jnp.where(adm, t, jnp.inf)),
                    jnp.maximum(mm[1], jnp.where(adm, t, -jnp.inf)))

        mn, mx = lax.fori_loop(0, n_tiles, minmax, (jnp.full((rows, LANES), jnp.inf, jnp.float32),
                                                    jnp.full((rows, LANES), -jnp.inf, jnp.float32)))
        row_min = jnp.broadcast_to(jnp.min(mn, axis=1, keepdims=True), (rows, LANES))
        row_max = jnp.broadcast_to(jnp.max(mx, axis=1, keepdims=True), (rows, LANES))
        n_adm = jnp.full((rows, LANES), past + 1, jnp.int32)
        thr, cut = _topk_threshold(count_ge, count_tie, row_min, row_max, n_adm, topk, n_tiles * LANES)
        thr_ref[...] = thr
        cut_ref[...] = cut


def _dsa_sample_attend_kernel(n_pages, pt_ref, q_ref, sc_ref, thr_ref, cut_ref, kvnew_ref,
                              cache_k_ref, cache_v_ref, o_ref, kbuf_ref, vbuf_ref, ksem_ref, vsem_ref):
    s = pl.program_id(0)
    n_seq = pl.num_programs(0)
    past = n_pages * PAGE_SIZE
    slot = s % 2

    def start(seq, sl):
        _pages_start(pt_ref, cache_k_ref, kbuf_ref, ksem_ref, seq, sl, n_pages)
        _pages_start(pt_ref, cache_v_ref, vbuf_ref, vsem_ref, seq, sl, n_pages)

    @pl.when(s == 0)
    def _():
        start(0, 0)

    @pl.when(s + 1 < n_seq)
    def _():
        start(s + 1, 1 - slot)

    _pages_wait(pt_ref, cache_k_ref, kbuf_ref, ksem_ref, s, slot, n_pages)
    _pages_wait(pt_ref, cache_v_ref, vbuf_ref, vsem_ref, s, slot, n_pages)

    q = q_ref[0]
    thr = thr_ref[0][:, 0:1]
    cut = cut_ref[0][:, 0:1]
    kw = N_KV_A * HEAD_DIM_A
    k_new = kvnew_ref[0][:, 0:kw].astype(MXU_DTYPE).astype(jnp.float32)
    v_new = kvnew_ref[0][:, kw:2 * kw].astype(MXU_DTYPE).astype(jnp.float32)
    sc = sc_ref[0]

    def selected(srow, kpos):
        return jnp.logical_or(srow > thr, jnp.logical_and(srow == thr, kpos < cut))

    kpos = lax.broadcasted_iota(jnp.int32, (1, past), 1)
    bias = jnp.where(selected(sc[:, 0:past], kpos), 0.0, NEG_INF)
    lg = _mm(q, kbuf_ref[slot]) + bias
    lg_self = jnp.sum(q.astype(jnp.float32) * k_new, axis=1, keepdims=True)
    lg_self = jnp.where(selected(sc[:, past:past + 1], past), lg_self, NEG_INF)
    m = jnp.maximum(jnp.max(lg, axis=1, keepdims=True), lg_self)
    p = jnp.exp(lg - m)
    p_self = jnp.exp(lg_self - m)
    denom = jnp.sum(p, axis=1, keepdims=True) + p_self
    o = (_mm_nt(p, vbuf_ref[slot]) + p_self * v_new) / denom
    parts = []
    for h in range(N_HEADS_A):
        n = h // GROUP_A
        parts.append(o[h:h + 1, n * HEAD_DIM_A:(n + 1) * HEAD_DIM_A])
    o_ref[0] = jnp.concatenate(parts, axis=1)


def _dsa_sample(qexp, iqhm, small, ik_new, kv_new, cache_k, cache_v, cache_ik, page_table):
    db, n_pages = page_table.shape
    past = n_pages * PAGE_SIZE
    n_pool = cache_ik.shape[0]
    topk = min(TOPK_MAX, (past + 1) // 4)
    n_tiles = n_pages + 1
    kw = N_KV_A * HEAD_DIM_A
    q_s = jnp.swapaxes(qexp, 0, 1)
    iq_s = jnp.swapaxes(iqhm, 0, 1)
    iw_s = small[:, 0:IDX_HEADS].reshape(db, IDX_HEADS, 1)
    ck_t = jnp.transpose(cache_k, (0, 2, 3, 1)).reshape(n_pool, kw, PAGE_SIZE)
    cv_t = jnp.transpose(cache_v, (0, 2, 3, 1)).reshape(n_pool, kw, PAGE_SIZE)
    cik_t = jnp.swapaxes(cache_ik, 1, 2)
    cparams = pltpu.CompilerParams(dimension_semantics=("arbitrary",), vmem_limit_bytes=VMEM_LIMIT_BYTES)
    per_seq = lambda *shape: pl.BlockSpec((1,) + shape, lambda s, pt: (s,) + (0,) * len(shape))
    whole = lambda *shape: pl.BlockSpec(shape, lambda s, pt: (0,) * len(shape))
    any_spec = pl.BlockSpec(memory_space=pl.ANY)

    sc, thr, cut = pl.pallas_call(
        functools.partial(_dsa_sample_score_kernel, topk, n_pages),
        grid_spec=pltpu.PrefetchScalarGridSpec(
            num_scalar_prefetch=1,
            grid=(db,),
            in_specs=[per_seq(IDX_HEADS, IDX_DIM), per_seq(IDX_HEADS, 1), per_seq(1, IDX_DIM), any_spec],
            out_specs=(whole(n_tiles, db, LANES), whole(db, LANES), whole(db, LANES)),
            scratch_shapes=[pltpu.VMEM((2, IDX_DIM, past), jnp.float32), pltpu.SemaphoreType.DMA((2,))],
        ),
        out_shape=(jax.ShapeDtypeStruct((n_tiles, db, LANES), jnp.float32),
                   jax.ShapeDtypeStruct((db, LANES), jnp.float32),
                   jax.ShapeDtypeStruct((db, LANES), jnp.int32)),
        compiler_params=cparams,
        name="dsa_sample_score",
    )(page_table, iq_s, iw_s, ik_new.reshape(db, 1, IDX_DIM), cik_t)

    o = pl.pallas_call(
        functools.partial(_dsa_sample_attend_kernel, n_pages),
        grid_spec=pltpu.PrefetchScalarGridSpec(
            num_scalar_prefetch=1,
            grid=(db,),
            in_specs=[per_seq(N_HEADS_A, LANES), per_seq(1, n_tiles * LANES), per_seq(1, LANES), per_seq(1, LANES),
                      per_seq(1, 2 * kw), any_spec, any_spec],
            out_specs=per_seq(1, BRANCH_WIDTH),
            scratch_shapes=[pltpu.VMEM((2, kw, past), jnp.float32), pltpu.VMEM((2, kw, past), jnp.float32),
                            pltpu.SemaphoreType.DMA((2,)), pltpu.SemaphoreType.DMA((2,))],
        ),
        out_shape=jax.ShapeDtypeStruct((db, 1, BRANCH_WIDTH), jnp.float32),
        compiler_params=cparams,
        name="dsa_sample_attend",
    )(page_table, q_s, jnp.swapaxes(sc, 0, 1).reshape(db, 1, n_tiles * LANES),
      thr.reshape(db, 1, LANES), cut.reshape(db, 1, LANES),
      kv_new.reshape(db, 1, 2 * kw), ck_t, cv_t)
    return o.reshape(db, BRANCH_WIDTH)


GDN_CHUNK = 128
A_LANE = IDX_HEADS
B_LANE = IDX_HEADS + H_B


def _split2(x):
    h = x.astype(MXU_DTYPE)
    return h, (x - h.astype(jnp.float32)).astype(MXU_DTYPE)


def _mm2(a, b):
    a1, a2 = _split2(a)
    b1, b2 = _split2(b)
    d = functools.partial(jnp.dot, preferred_element_type=jnp.float32)
    return d(a1, b1) + (d(a1, b2) + d(a2, b1))


def _unit_lower_inverses(mats):
    n = mats[0].shape[0]
    eye = (lax.broadcasted_iota(jnp.int32, (n, n), 0) == lax.broadcasted_iota(jnp.int32, (n, n), 1))
    ss = [jnp.where(eye, 1.0, 0.0) - a for a in mats]
    ps = [_mm2(a, a) for a in mats]
    k = 2
    while k < n:
        ss = [s + _mm2(s, p) for s, p in zip(ss, ps)]
        k *= 2
        if k < n:
            ps = [_mm2(p, p) for p in ps]
    return ss


def _l2norm(x):
    return x * lax.rsqrt(jnp.sum(x * x, axis=-1, keepdims=True) + NORM_EPS)


def _gdn_prompt_kernel(u_ref, small_ref, z_ref, conv0_ref, s0_ref, wconv_ref, alog_ref, dtb_ref, dnorm_ref,
                       o_ref, conv_out_ref, s_out_ref, ucat_ref, state_ref):
    n = pl.program_id(0)
    nb = u_ref.shape[0]
    c = GDN_CHUNK
    head = SUBLANES
    tail = CONV_K - 1

    @pl.when(n == 0)
    def _():
        ucat_ref[:, head - tail:head, :] = conv0_ref[...]
        state_ref[...] = s0_ref[...]

    row = lax.broadcasted_iota(jnp.int32, (c, c), 0)
    col = lax.broadcasted_iota(jnp.int32, (c, c), 1)
    lower = row >= col
    strict = row > col
    ltri = jnp.where(lower, 1.0, 0.0).astype(MXU_DTYPE)
    d = functools.partial(jnp.dot, preferred_element_type=jnp.float32)

    chains = [(b, h) for b in range(nb) for h in range(H_B)]
    qw = H_B * DK_B
    qs, ks, vs, betas, gcols = [], [], [], [], []
    for b in range(nb):
        ucat_ref[b, head:head + c, :] = u_ref[b]
        y = wconv_ref[tail:tail + 1, :] * ucat_ref[b, head:head + c, :]
        for j in range(tail):
            y = y + wconv_ref[j:j + 1, :] * ucat_ref[b, head - tail + j:head - tail + j + c, :]
        cv = _silu(y)
        carry_rows = ucat_ref[b, head + c - tail:head + c, :]
        ucat_ref[b, head - tail:head, :] = carry_rows
        conv_out_ref[b] = carry_rows
        sm = small_ref[b]
        g_all = -jnp.exp(alog_ref[...]) * _softplus(sm + dtb_ref[...])
        beta_all = _sigmoid(sm)
        a1, a2, a3 = _split3(g_all)
        gc_all = d(ltri, a1) + (d(ltri, a2) + d(ltri, a3))
        for h in range(H_B):
            qs.append(_l2norm(cv[:, h * DK_B:(h + 1) * DK_B]) * (DK_B ** -0.5))
            ks.append(_l2norm(cv[:, qw + h * DK_B:qw + (h + 1) * DK_B]))
            vs.append(cv[:, 2 * qw + h * DV_B:2 * qw + (h + 1) * DV_B])
            betas.append(jnp.broadcast_to(beta_all[:, B_LANE + h:B_LANE + h + 1], (c, LANES)))
            gcols.append(jnp.broadcast_to(gc_all[:, A_LANE + h:A_LANE + h + 1], (c, c)))
    decays = [jnp.where(lower, jnp.exp(jnp.where(lower, g - g.T, 0.0)), 0.0) for g in gcols]
    egs = [jnp.exp(g) for g in gcols]
    g_lasts = [g[c - 1:c, :] for g in gcols]
    kbs = [k * b for k, b in zip(ks, betas)]
    vbs = [v * b for v, b in zip(vs, betas)]
    kks = [_mm_nt(kb, k) for kb, k in zip(kbs, ks)]
    qks = [_mm_nt(q, k) for q, k in zip(qs, ks)]
    t_invs = _unit_lower_inverses([jnp.where(strict, kk * dc, 0.0) for kk, dc in zip(kks, decays)])
    sols = [_mm2(t, jnp.concatenate([vb, kb * eg], axis=1)) for t, vb, kb, eg in zip(t_invs, vbs, kbs, egs)]
    s_olds = [state_ref[b, h] for b, h in chains]
    v_news = [sol[:, 0:DV_B] - _mm(sol[:, DV_B:DV_B + DK_B], s) for sol, s in zip(sols, s_olds)]
    o_hs = [_mm(q * eg, s) + _mm(qk * dc, v_new)
            for q, eg, s, qk, dc, v_new in zip(qs, egs, s_olds, qks, decays, v_news)]
    for i, (b, h) in enumerate(chains):
        k_dec = ks[i] * jnp.exp(g_lasts[i] - gcols[i])
        state_ref[b, h] = s_olds[i] * jnp.exp(g_lasts[i]) + _mm_tn(k_dec, v_news[i])
    for i, (b, h) in enumerate(chains):
        o_h = o_hs[i]
        ms = jnp.mean(o_h * o_h, axis=-1, keepdims=True)
        o_n = o_h * lax.rsqrt(ms + NORM_EPS) * dnorm_ref[...]
        gate = _silu(z_ref[b, :, h * DV_B:(h + 1) * DV_B])
        o_ref[b, :, h * DV_B:(h + 1) * DV_B] = (o_n * gate).astype(o_ref.dtype)

    @pl.when(n == pl.num_programs(0) - 1)
    def _():
        s_out_ref[...] = state_ref[...]


GDN_SEQ_TILE = 8


def _gdn_sample_kernel(u_ref, cb_ref, small_ref, z_ref, s0_ref, wconv_ref, alog_ref, dtb_ref, dnorm_ref,
                       o_ref, conv_out_ref, s_out_ref):
    ts = GDN_SEQ_TILE
    tail = CONV_K - 1
    u_new = u_ref[...]
    y = wconv_ref[tail:tail + 1, :] * u_new
    for j in range(tail):
        y = y + wconv_ref[j:j + 1, :] * cb_ref[j]
    cv = _silu(y)
    for j in range(tail - 1):
        conv_out_ref[j] = cb_ref[j + 1]
    conv_out_ref[tail - 1] = u_new

    sm = small_ref[...]
    eg_all = jnp.exp(-jnp.exp(alog_ref[...]) * _softplus(sm + dtb_ref[...]))
    beta_all = _sigmoid(sm)
    qw = H_B * DK_B
    for h in range(H_B):
        q = _l2norm(cv[:, h * DK_B:(h + 1) * DK_B]) * (DK_B ** -0.5)
        k = _l2norm(cv[:, qw + h * DK_B:qw + (h + 1) * DK_B])
        v = cv[:, 2 * qw + h * DV_B:2 * qw + (h + 1) * DV_B]
        eg = eg_all[:, A_LANE + h:A_LANE + h + 1]
        beta = beta_all[:, B_LANE + h:B_LANE + h + 1]
        qk = jnp.sum(q * k, axis=-1, keepdims=True)
        k_t, q_t = k.T, q.T
        rows = []
        for r in range(ts):
            s_old = s0_ref[r, h]
            kc = k_t[:, r:r + 1]
            ks = jnp.sum(s_old * kc, axis=0, keepdims=True)
            qs = jnp.sum(s_old * q_t[:, r:r + 1], axis=0, keepdims=True)
            eg_r = eg[r:r + 1, :]
            v_new = beta[r:r + 1, :] * (v[r:r + 1, :] - eg_r * ks)
            rows.append(eg_r * qs + qk[r:r + 1, :] * v_new)
            s_out_ref[r, h] = s_old * eg_r + kc * v_new
        o_h = jnp.concatenate(rows, axis=0)
        ms = jnp.mean(o_h * o_h, axis=-1, keepdims=True)
        o_n = o_h * lax.rsqrt(ms + NORM_EPS) * dnorm_ref[...]
        o_ref[:, h * DV_B:(h + 1) * DV_B] = (o_n * _silu(z_ref[:, h * DV_B:(h + 1) * DV_B])).astype(o_ref.dtype)


def _gdn_sample(u, small, z, conv_buf, s0, w_conv, a_log, dt_bias, delta_norm):
    db = u.shape[0]
    ts = GDN_SEQ_TILE
    assert db % ts == 0
    tail = CONV_K - 1
    alog_row, dtb_row = _gate_rows(a_log, dt_bias)
    row = lambda w: pl.BlockSpec((ts, w), lambda i: (i, 0))
    full = lambda *shape: pl.BlockSpec(shape, lambda i: (0,) * len(shape))
    cb_spec = pl.BlockSpec((tail, ts, CONV_DIM), lambda i: (0, i, 0))
    st_spec = pl.BlockSpec((ts, H_B, DK_B, DV_B), lambda i: (i, 0, 0, 0))
    o, conv_t, s_new = pl.pallas_call(
        _gdn_sample_kernel,
        grid=(db // ts,),
        in_specs=[row(CONV_DIM), cb_spec, row(SMALL_W), row(H_B * DV_B), st_spec,
                  full(CONV_K, CONV_DIM), full(1, SMALL_W), full(1, SMALL_W), full(1, DV_B)],
        out_specs=(row(H_B * DV_B), cb_spec, st_spec),
        out_shape=(jax.ShapeDtypeStruct((db, H_B * DV_B), jnp.float32),
                   jax.ShapeDtypeStruct((tail, db, CONV_DIM), jnp.float32),
                   jax.ShapeDtypeStruct((db, H_B, DK_B, DV_B), jnp.float32)),
        compiler_params=pltpu.CompilerParams(dimension_semantics=("arbitrary",),
                                             vmem_limit_bytes=VMEM_LIMIT_BYTES),
        name="gdn_sample",
    )(u, jnp.swapaxes(conv_buf, 0, 1), small, z, s0, w_conv.astype(jnp.float32), alog_row, dtb_row,
      delta_norm.astype(jnp.float32)[None, :])
    return o, jnp.swapaxes(conv_t, 0, 1), s_new


def _gate_rows(a_log, dt_bias):
    alog_row = jnp.zeros((1, SMALL_W), jnp.float32).at[0, A_LANE:A_LANE + H_B].set(a_log.astype(jnp.float32))
    dtb_row = jnp.zeros((1, SMALL_W), jnp.float32).at[0, A_LANE:A_LANE + H_B].set(dt_bias.astype(jnp.float32))
    return alog_row, dtb_row


def _gdn_prompt(u, small, z, conv0, s0, w_conv, a_log, dt_bias, delta_norm, batch, seq):
    c = GDN_CHUNK
    assert seq % c == 0
    alog_row, dtb_row = _gate_rows(a_log, dt_bias)
    row = lambda w: pl.BlockSpec((batch, c, w), lambda i: (0, i, 0))
    full = lambda *shape: pl.BlockSpec(shape, lambda i: (0,) * len(shape))
    o, conv_new, s_new = pl.pallas_call(
        _gdn_prompt_kernel,
        grid=(seq // c,),
        in_specs=[row(CONV_DIM), row(SMALL_W), row(H_B * DV_B), full(batch, CONV_K - 1, CONV_DIM),
                  full(batch, H_B, DK_B, DV_B), full(CONV_K, CONV_DIM), full(1, SMALL_W), full(1, SMALL_W),
                  full(1, DV_B)],
        out_specs=(row(H_B * DV_B), full(batch, CONV_K - 1, CONV_DIM), full(batch, H_B, DK_B, DV_B)),
        out_shape=(jax.ShapeDtypeStruct((batch, seq, H_B * DV_B), MXU_DTYPE),
                   jax.ShapeDtypeStruct((batch, CONV_K - 1, CONV_DIM), jnp.float32),
                   jax.ShapeDtypeStruct((batch, H_B, DK_B, DV_B), jnp.float32)),
        scratch_shapes=[pltpu.VMEM((batch, SUBLANES + c, CONV_DIM), jnp.float32),
                        pltpu.VMEM((batch, H_B, DK_B, DV_B), jnp.float32)],
        compiler_params=pltpu.CompilerParams(dimension_semantics=("arbitrary",),
                                             vmem_limit_bytes=VMEM_LIMIT_BYTES),
        name="gdn_prompt",
    )(u.reshape(batch, seq, CONV_DIM), small.reshape(batch, seq, SMALL_W), z.reshape(batch, seq, H_B * DV_B),
      conv0, s0, w_conv.astype(jnp.float32), alog_row, dtb_row, delta_norm.astype(jnp.float32)[None, :])
    return o.reshape(batch * seq, H_B * DV_B), conv_new, s_new


def _merge_kernel(x_ref, oa_ref, ob_ref, gl_ref, wba_ref, wbb_ref, wout_ref, gain_ref, x1_ref, hn_ref):
    pa = _mm(oa_ref[...], wba_ref[...])
    pb = _mm(ob_ref[...], wbb_ref[...])
    mix = _sigmoid(gl_ref[:, 0:D_MODEL]) * pa + _sigmoid(gl_ref[:, D_MODEL:2 * D_MODEL]) * pb
    x1 = x_ref[...] + _mm(mix, wout_ref[...])
    x1_ref[...] = x1
    ms = jnp.mean(x1 * x1, axis=-1, keepdims=True)
    hn_ref[...] = (x1 * lax.rsqrt(ms + NORM_EPS) * gain_ref[...]).astype(hn_ref.dtype)


def _merge(x2d, o_a, o_b, gl, w_branch, w_out, norm_ffn, tm):
    n = x2d.shape[0]
    assert n % tm == 0
    row = lambda w: pl.BlockSpec((tm, w), lambda i: (i, 0))
    full = lambda *shape: pl.BlockSpec(shape, lambda i: (0,) * len(shape))
    return pl.pallas_call(
        _merge_kernel,
        grid=(n // tm,),
        in_specs=[row(D_MODEL), row(BRANCH_WIDTH), row(BRANCH_WIDTH), row(2 * D_MODEL),
                  full(BRANCH_WIDTH, D_MODEL), full(BRANCH_WIDTH, D_MODEL), full(D_MODEL, D_MODEL),
                  full(1, D_MODEL)],
        out_specs=(row(D_MODEL), row(D_MODEL)),
        out_shape=(jax.ShapeDtypeStruct((n, D_MODEL), jnp.float32),
                   jax.ShapeDtypeStruct((n, D_MODEL), MXU_DTYPE)),
        compiler_params=pltpu.CompilerParams(dimension_semantics=("arbitrary",),
                                             vmem_limit_bytes=VMEM_LIMIT_BYTES),
        name="merge",
    )(x2d, o_a, o_b, gl, w_branch[0].astype(MXU_DTYPE), w_branch[1].astype(MXU_DTYPE),
      w_out.astype(MXU_DTYPE), norm_ffn.astype(jnp.float32)[None, :])


FFN_TILE = D_FF // 2


def _ffn_kernel(hn_ref, x1_ref, wg_ref, wu_ref, wd_ref, y_ref, acc_ref):
    j = pl.program_id(1)

    @pl.when(j == 0)
    def _():
        acc_ref[...] = x1_ref[...]

    hn = hn_ref[...]
    g = jnp.dot(hn, wg_ref[...], preferred_element_type=jnp.float32)
    u = jnp.dot(hn, wu_ref[...], preferred_element_type=jnp.float32)
    acc_ref[...] += _mm(_silu(g) * u, wd_ref[...])

    @pl.when(j == pl.num_programs(1) - 1)
    def _():
        y_ref[...] = acc_ref[...]


def _ffn(hn, x1, w_gate_up, w_down, tm):
    n = hn.shape[0]
    tf = FFN_TILE
    assert n % tm == 0 and D_FF % tf == 0 and tf % LANES == 0
    nf = D_FF // tf
    wgu = w_gate_up.astype(MXU_DTYPE)
    return pl.pallas_call(
        _ffn_kernel,
        grid=(n // tm, nf),
        in_specs=[pl.BlockSpec((tm, D_MODEL), lambda i, j: (i, 0)),
                  pl.BlockSpec((tm, D_MODEL), lambda i, j: (i, 0)),
                  pl.BlockSpec((D_MODEL, tf), lambda i, j: (0, j)),
                  pl.BlockSpec((D_MODEL, tf), lambda i, j: (0, j + nf)),
                  pl.BlockSpec((tf, D_MODEL), lambda i, j: (j, 0))],
        out_specs=pl.BlockSpec((tm, D_MODEL), lambda i, j: (i, 0)),
        out_shape=jax.ShapeDtypeStruct((n, D_MODEL), jnp.float32),
        scratch_shapes=[pltpu.VMEM((tm, D_MODEL), jnp.float32)],
        compiler_params=pltpu.CompilerParams(dimension_semantics=("arbitrary", "arbitrary"),
                                             vmem_limit_bytes=VMEM_LIMIT_BYTES),
        name="ffn",
    )(hn, x1, wgu, wgu, w_down.astype(MXU_DTYPE))


IN_PROJ_TILE = K_TILE
MERGE_TILE = 512
FFN_ROW_TILE = 512


def _layer(x_p, x_s, cache_k, cache_v, cache_ik, conv_s, delta_s, page_table, norm_mix, w_in, q_norm, k_norm,
           w_conv, a_log, dt_bias, delta_norm, w_branch, w_out, norm_ffn, w_gate_up, w_down):
    b, t, d = x_p.shape
    db = x_s.shape[0]
    past = page_table.shape[1] * PAGE_SIZE
    kw = N_KV_A * HEAD_DIM_A
    w_packed = _pack_w_in(w_in)

    xp2 = x_p.reshape(b * t, d)
    tm = IN_PROJ_TILE
    assert t % tm == 0
    kv, ik, small, u, z, gl, qt, iqt, wt, kb, ikb, vt = _in_proj(
        xp2, _rope_tables(jnp.arange(t)), t // tm, tm, norm_mix, w_packed, q_norm, k_norm, key_major=True)
    o_a = _dsa_prompt(qt, iqt, wt, kb, ikb, vt, b, t)
    conv0 = jnp.zeros((b, CONV_K - 1, CONV_DIM), jnp.float32)
    delta0 = jnp.zeros((b, H_B, DK_B, DV_B), jnp.float32)
    o_b, conv_p, delta_p = _gdn_prompt(u, small, z, conv0, delta0, w_conv, a_log, dt_bias, delta_norm, b, t)
    x1, hn = _merge(xp2, o_a, o_b, gl, w_branch, w_out, norm_ffn, min(MERGE_TILE, b * t))
    y_p = _ffn(hn, x1, w_gate_up, w_down, min(FFN_ROW_TILE, b * t)).reshape(b, t, d)
    kv5 = kv.reshape(b, 2, N_KV_A, HEAD_DIM_A, t)
    st_p = (jnp.transpose(kv5[:, 0], (0, 3, 1, 2)), jnp.transpose(kv5[:, 1], (0, 3, 1, 2)),
            jnp.swapaxes(ik, 1, 2), conv_p, delta_p)

    xs2 = x_s.reshape(db, d)
    kv, ik, small, u, z, gl, qexp, iqhm = _in_proj(
        xs2, _rope_tables(jnp.full((db,), past, jnp.int32)), 1, db, norm_mix, w_packed, q_norm, k_norm,
        key_major=False)
    o_a = _dsa_sample(qexp, iqhm, small, ik, kv, cache_k, cache_v, cache_ik, page_table)
    o_b, conv_n, delta_n = _gdn_sample(u, small, z, conv_s, delta_s, w_conv, a_log, dt_bias, delta_norm)
    x1, hn = _merge(xs2, o_a, o_b, gl, w_branch, w_out, norm_ffn, db)
    y_s = _ffn(hn, x1, w_gate_up, w_down, db).reshape(db, 1, d)
    st_s = (kv[:, 0:kw].reshape(db, 1, N_KV_A, HEAD_DIM_A), kv[:, kw:2 * kw].reshape(db, 1, N_KV_A, HEAD_DIM_A),
            ik.reshape(db, 1, IDX_DIM), conv_n, delta_n)
    return y_p, y_s, st_p, st_s


def kernel(x_prompt, x_sample, cache_k, cache_v, cache_idx_k, state_conv, state_delta, page_table,
           norm_mix, w_in, q_norm, k_norm, w_conv, a_log, dt_bias, delta_norm, w_branch, w_out,
           norm_ffn, w_gate_up, w_down):
    assert x_sample.shape[1] == 1, "the sample group decodes one token per sequence"
    y_p, y_s = x_prompt, x_sample
    new_p, new_s = [], []
    for l in range(w_in.shape[0]):
        y_p, y_s, st_p, st_s = _layer(
            y_p, y_s, cache_k[l], cache_v[l], cache_idx_k[l], state_conv[l], state_delta[l], page_table,
            norm_mix[l], w_in[l], q_norm[l], k_norm[l], w_conv[l], a_log[l], dt_bias[l], delta_norm[l],
            w_branch[l], w_out[l], norm_ffn[l], w_gate_up[l], w_down[l])
        new_p.append(st_p)
        new_s.append(st_s)
    k_p, v_p, ik_p, conv_p, delta_p = [jnp.stack(a) for a in zip(*new_p)]
    k_s, v_s, ik_s, conv_s, delta_s = [jnp.stack(a) for a in zip(*new_s)]
    return (y_p, y_s, k_p, v_p, ik_p, conv_p, delta_p, k_s, v_s, ik_s, conv_s, delta_s)
```

```python
import functools
import math

import jax
import jax.numpy as jnp
import numpy as np
from jax import lax
from jax.experimental import pallas as pl
from jax.experimental.pallas import tpu as pltpu

D_MODEL = 1024
PAGE_SIZE = 128
N_HEADS_A = 8
N_KV_A = 2
HEAD_DIM_A = 64
GROUP_A = N_HEADS_A // N_KV_A
IDX_HEADS = 8
IDX_DIM = 64
TOPK_MAX = 256
ROPE_THETA = 500000.0
H_B = 4
DK_B = 128
DV_B = 128
CONV_K = 4
CONV_DIM = 2 * H_B * DK_B + H_B * DV_B
BRANCH_WIDTH = N_HEADS_A * HEAD_DIM_A
D_FF = -(-8 * D_MODEL // (3 * 256)) * 256
NORM_EPS = 1e-6
NEG_INF = -1e30
IN_SIZES = (N_HEADS_A * HEAD_DIM_A, N_KV_A * HEAD_DIM_A, N_KV_A * HEAD_DIM_A,
            IDX_HEADS * IDX_DIM, IDX_DIM, IDX_HEADS,
            CONV_DIM, H_B, H_B, H_B * DV_B, 2 * D_MODEL)

LANES = 128
SUBLANES = 8
VMEM_LIMIT_BYTES = 56 * 1024 * 1024

MXU_DTYPE = jnp.bfloat16

SMALL_W = LANES
SEG_A = BRANCH_WIDTH + 2 * N_KV_A * HEAD_DIM_A + IDX_HEADS * IDX_DIM + IDX_DIM
SEG_A_PAD = -(-SEG_A // LANES) * LANES
OFF_SMALL = SEG_A_PAD
OFF_U = OFF_SMALL + SMALL_W
OFF_Z = OFF_U + CONV_DIM
OFF_GL = OFF_Z + H_B * DV_B
D_IN_PACKED = OFF_GL + 2 * D_MODEL


def _mm(a, b):
    return jnp.dot(a.astype(MXU_DTYPE), b.astype(MXU_DTYPE), preferred_element_type=jnp.float32)


def _mm_nt(a, b):
    return lax.dot_general(a.astype(MXU_DTYPE), b.astype(MXU_DTYPE), (((1,), (1,)), ((), ())),
                           preferred_element_type=jnp.float32)


def _mm_tn(a, b):
    return lax.dot_general(a.astype(MXU_DTYPE), b.astype(MXU_DTYPE), (((0,), (0,)), ((), ())),
                           preferred_element_type=jnp.float32)


def _split3(x):
    x = x.astype(jnp.float32)
    h = x.astype(MXU_DTYPE)
    r = x - h.astype(jnp.float32)
    m = r.astype(MXU_DTYPE)
    l = (r - m.astype(jnp.float32)).astype(MXU_DTYPE)
    return h, m, l


def _sigmoid(x):
    return 1.0 / (1.0 + jnp.exp(-x))


def _silu(x):
    return x * _sigmoid(x)


def _softplus(x):
    return jnp.maximum(x, 0.0) + jnp.log(1.0 + jnp.exp(-jnp.abs(x)))


def _rope_tile(x, cos_t, sin_lo, sin_hi):
    half = HEAD_DIM_A // 8
    up = pltpu.roll(x, LANES - half, 1)
    dn = pltpu.roll(x, half, 1)
    return x * cos_t + up * sin_lo + dn * sin_hi


def _in_proj_kernel(key_major, x_ref, gain_ref, w_ref, bd_ref, qg_ref, kg_ref, cos_ref, slo_ref, shi_ref,
                    kv_ref, ik_ref, small_ref, u_ref, z_ref, gl_ref, *attn_refs):
    x = x_ref[...]
    ms = jnp.mean(x * x, axis=-1, keepdims=True)
    xn = (x * lax.rsqrt(ms + NORM_EPS) * gain_ref[...]).astype(MXU_DTYPE)

    cos_t, sin_lo, sin_hi = cos_ref[...], slo_ref[...], shi_ref[...]
    lane = lax.broadcasted_iota(jnp.int32, (x.shape[0], LANES), 1)
    lo_half = lane < HEAD_DIM_A

    def head_rms(t, gain):
        tt = t * t
        hi = tt.astype(MXU_DTYPE)
        lo = (tt - hi.astype(jnp.float32)).astype(MXU_DTYPE)
        bd = bd_ref[0:t.shape[1], 0:t.shape[1]]
        msq = (jnp.dot(hi, bd, preferred_element_type=jnp.float32)
               + jnp.dot(lo, bd, preferred_element_type=jnp.float32))
        return t * lax.rsqrt(msq + NORM_EPS) * gain

    q = jnp.dot(xn, w_ref[:, 0:BRANCH_WIDTH], preferred_element_type=jnp.float32)
    q = head_rms(q, qg_ref[...])
    if key_major:
        qt_ref, iqt_ref, wt_ref, kb_ref, ikb_ref, vt_ref = attn_refs
        n_qb = x.shape[0] // Q_TILE
        q_scale = HEAD_DIM_A ** -0.5 * math.log2(math.e)
    else:
        qexp_ref, iqhm_ref = attn_refs
        q_scale = HEAD_DIM_A ** -0.5
    for p in range(BRANCH_WIDTH // LANES):
        t = _rope_tile(q[:, p * LANES:(p + 1) * LANES], cos_t, sin_lo, sin_hi) * q_scale
        t_sw = pltpu.roll(t, HEAD_DIM_A, 1)
        for e in range(2):
            h = 2 * p + e
            n = h // GROUP_A
            src = t if e == n else t_sw
            keep = lo_half if n == 0 else jnp.logical_not(lo_half)
            qe = jnp.where(keep, src, 0.0)
            if key_major:
                qe_t = qe.T
                for j in range(n_qb):
                    qt_ref[j, :, h * Q_TILE:(h + 1) * Q_TILE] = qe_t[:, j * Q_TILE:(j + 1) * Q_TILE].astype(qt_ref.dtype)
            else:
                qexp_ref[h] = qe.astype(qexp_ref.dtype)

    c0 = BRANCH_WIDTH
    kw = N_KV_A * HEAD_DIM_A
    k = jnp.dot(xn, w_ref[:, c0:c0 + kw], preferred_element_type=jnp.float32)
    k = _rope_tile(head_rms(k, kg_ref[...]), cos_t, sin_lo, sin_hi)
    v = jnp.dot(xn, w_ref[:, c0 + kw:c0 + 2 * kw], preferred_element_type=jnp.float32)
    if key_major:
        v_t = v.T
        kv_ref[0, 0] = k.T
        kv_ref[0, 1] = v_t
        kb_ref[...] = k.astype(kb_ref.dtype)
        vt_ref[0, 0:kw, :] = v_t.astype(vt_ref.dtype)
        vt_ref[0, kw:kw + ONES_ROWS, :] = jnp.ones((ONES_ROWS, x.shape[0]), vt_ref.dtype)
    else:
        kv_ref[:, 0:kw] = k
        kv_ref[:, kw:2 * kw] = v

    c1 = c0 + 2 * kw
    iqw = IDX_HEADS * IDX_DIM
    iq = jnp.dot(xn, w_ref[:, c1:c1 + iqw], preferred_element_type=jnp.float32)
    for p in range(iqw // LANES):
        t = _rope_tile(iq[:, p * LANES:(p + 1) * LANES], cos_t, sin_lo, sin_hi)
        if key_major:
            t_t = t.T
            for e in range(2):
                h = 2 * p + e
                for j in range(n_qb):
                    iqt_ref[j, :, h * Q_TILE:(h + 1) * Q_TILE] = (
                        t_t[e * IDX_DIM:(e + 1) * IDX_DIM, j * Q_TILE:(j + 1) * Q_TILE].astype(iqt_ref.dtype))
        else:
            t = t.astype(iqhm_ref.dtype)
            iqhm_ref[2 * p] = t[:, 0:IDX_DIM]
            iqhm_ref[2 * p + 1] = t[:, IDX_DIM:2 * IDX_DIM]

    c2 = c1 + iqw
    ik_sm = jnp.dot(xn, w_ref[:, c2:c2 + 2 * LANES], preferred_element_type=jnp.float32)
    ik_tile = _rope_tile(ik_sm[:, 0:LANES], cos_t, sin_lo, sin_hi)
    ik = ik_tile[:, 0:IDX_DIM]
    small = ik_sm[:, LANES:2 * LANES]
    small_ref[...] = small
    if not key_major:
        ik_ref[...] = ik
    else:
        ik_ref[0] = ik_tile.T[0:IDX_DIM]
        ikb_ref[...] = ik.astype(ikb_ref.dtype)
        small_t = small.T
        for j in range(n_qb):
            wt_ref[j] = small_t[0:IDX_HEADS, j * Q_TILE:(j + 1) * Q_TILE]

    u_ref[...] = jnp.dot(xn, w_ref[:, OFF_U:OFF_U + CONV_DIM], preferred_element_type=jnp.float32)
    z_ref[...] = jnp.dot(xn, w_ref[:, OFF_Z:OFF_Z + H_B * DV_B], preferred_element_type=jnp.float32)
    gl_ref[...] = jnp.dot(xn, w_ref[:, OFF_GL:OFF_GL + 2 * D_MODEL], preferred_element_type=jnp.float32)


def _pack_w_in(w_in):
    pts = np.cumsum(IN_SIZES)[:-1].tolist()
    q, k, v, iq, ik, iw, u, a, b, z, gl = jnp.split(w_in, pts, axis=-1)
    d = w_in.shape[0]
    seg_a = jnp.concatenate([q, k, v, iq, ik, jnp.zeros((d, SEG_A_PAD - SEG_A), w_in.dtype)], axis=1)
    small = jnp.concatenate([iw, a, b, jnp.zeros((d, SMALL_W - IDX_HEADS - 2 * H_B), w_in.dtype)], axis=1)
    return jnp.concatenate([seg_a, small, u, z, gl], axis=1).astype(MXU_DTYPE)


def _rope_tables(pos):
    rot = HEAD_DIM_A // 4
    half = rot // 2
    inv_freq = ROPE_THETA ** (-jnp.arange(half, dtype=jnp.float32) / half)
    ang = pos.astype(jnp.float32)[:, None] * inv_freq[None, :]
    cos, sin = jnp.cos(ang), jnp.sin(ang)
    rows = pos.shape[0]
    one = jnp.ones((rows, HEAD_DIM_A - rot), jnp.float32)
    zero = jnp.zeros((rows, HEAD_DIM_A - rot), jnp.float32)
    zh = jnp.zeros((rows, half), jnp.float32)
    cos_h = jnp.concatenate([cos, cos, one], axis=1)
    slo_h = jnp.concatenate([-sin, zh, zero], axis=1)
    shi_h = jnp.concatenate([zh, sin, zero], axis=1)
    rep = LANES // HEAD_DIM_A
    return jnp.tile(cos_h, (1, rep)), jnp.tile(slo_h, (1, rep)), jnp.tile(shi_h, (1, rep))


def _in_proj(x2d, pos_tables, n_table_blocks, tm, norm_mix, w_packed, q_norm, k_norm, key_major):
    n = x2d.shape[0]
    assert n % tm == 0 and (not key_major or tm == K_TILE)
    cos_t, sin_lo, sin_hi = pos_tables
    bd = jnp.kron(jnp.eye(BRANCH_WIDTH // HEAD_DIM_A, dtype=jnp.float32),
                  jnp.full((HEAD_DIM_A, HEAD_DIM_A), 1.0 / HEAD_DIM_A, jnp.float32)).astype(MXU_DTYPE)
    qg = jnp.tile(q_norm.astype(jnp.float32), BRANCH_WIDTH // HEAD_DIM_A)[None, :]
    kg = jnp.tile(k_norm.astype(jnp.float32), N_KV_A)[None, :]
    kw = N_KV_A * HEAD_DIM_A
    row = lambda w: pl.BlockSpec((tm, w), lambda i: (i, 0))
    full = lambda a: pl.BlockSpec(a.shape, lambda i: (0,) * a.ndim)
    tab = pl.BlockSpec((tm, LANES), lambda i: (i % n_table_blocks, 0))
    if key_major:
        seq = n_table_blocks * tm
        nt = n_table_blocks
        kv_shape, ik_shape = (n // seq, 2, kw, seq), (n // seq, IDX_DIM, seq)
        kv_spec = pl.BlockSpec((1, 2, kw, tm), lambda i: (i // nt, 0, 0, i % nt))
        ik_spec = pl.BlockSpec((1, IDX_DIM, tm), lambda i: (i // nt, 0, i % nt))
    else:
        kv_shape, ik_shape = (n, 2 * kw), (n, IDX_DIM)
        kv_spec, ik_spec = row(2 * kw), row(IDX_DIM)
    out_shape = [
        jax.ShapeDtypeStruct(kv_shape, jnp.float32),
        jax.ShapeDtypeStruct(ik_shape, jnp.float32),
        jax.ShapeDtypeStruct((n, SMALL_W), jnp.float32),
        jax.ShapeDtypeStruct((n, CONV_DIM), jnp.float32),
        jax.ShapeDtypeStruct((n, H_B * DV_B), jnp.float32),
        jax.ShapeDtypeStruct((n, 2 * D_MODEL), jnp.float32),
    ]
    out_specs = [kv_spec, ik_spec, row(SMALL_W), row(CONV_DIM), row(H_B * DV_B), row(2 * D_MODEL)]
    if key_major:
        n_qb = tm // Q_TILE
        blk = lambda r, w: pl.BlockSpec((n_qb, r, w), lambda i: (i, 0, 0))
        out_shape += [
            jax.ShapeDtypeStruct((n // Q_TILE, LANES, N_HEADS_A * Q_TILE), MXU_DTYPE),
            jax.ShapeDtypeStruct((n // Q_TILE, IDX_DIM, IDX_HEADS * Q_TILE), MXU_DTYPE),
            jax.ShapeDtypeStruct((n // Q_TILE, IDX_HEADS, Q_TILE), jnp.float32),
            jax.ShapeDtypeStruct((n, kw), MXU_DTYPE),
            jax.ShapeDtypeStruct((n, IDX_DIM), MXU_DTYPE),
            jax.ShapeDtypeStruct((n // K_TILE, kw + ONES_ROWS, K_TILE), MXU_DTYPE),
        ]
        out_specs += [blk(LANES, N_HEADS_A * Q_TILE), blk(IDX_DIM, IDX_HEADS * Q_TILE), blk(IDX_HEADS, Q_TILE),
                      row(kw), row(IDX_DIM), pl.BlockSpec((1, kw + ONES_ROWS, K_TILE), lambda i: (i, 0, 0))]
    else:
        out_shape += [jax.ShapeDtypeStruct((N_HEADS_A, n, LANES), MXU_DTYPE),
                      jax.ShapeDtypeStruct((IDX_HEADS, n, IDX_DIM), MXU_DTYPE)]
        out_specs += [pl.BlockSpec((N_HEADS_A, tm, LANES), lambda i: (0, i, 0)),
                      pl.BlockSpec((IDX_HEADS, tm, IDX_DIM), lambda i: (0, i, 0))]
    return pl.pallas_call(
        functools.partial(_in_proj_kernel, key_major),
        grid=(n // tm,),
        in_specs=[row(D_MODEL), full(norm_mix[None, :]), full(w_packed), full(bd), full(qg), full(kg),
                  tab, tab, tab],
        out_specs=tuple(out_specs),
        out_shape=tuple(out_shape),
        compiler_params=pltpu.CompilerParams(dimension_semantics=("arbitrary",),
                                             vmem_limit_bytes=VMEM_LIMIT_BYTES),
        name="in_proj",
    )(x2d, norm_mix[None, :].astype(jnp.float32), w_packed, bd, qg, kg, cos_t, sin_lo, sin_hi)


_INT_MAG = 0x7FFFFFFF


def _f32_key(x):
    b = lax.bitcast_convert_type(x, jnp.int32)
    return b ^ (lax.shift_right_arithmetic(b, 31) & _INT_MAG)


def _key_f32(k):
    b = k ^ (lax.shift_right_arithmetic(k, 31) & _INT_MAG)
    return lax.bitcast_convert_type(b, jnp.float32)


def _topk_threshold(count_ge, count_tie, row_min, row_max, n_adm, topk, n_keys, zero_counts=None,
                    fixed_steps=18, linear_steps=24):
    kf = jnp.float32(topk)
    need = n_adm > topk
    lo_k = _f32_key(row_min)
    hi_k = _f32_key(row_max) + 1
    thr = jnp.where(need, row_min, -jnp.inf)
    done = jnp.where(need, 0, 1).astype(jnp.int32)

    zero = jnp.zeros_like(row_min)
    if zero_counts is None:
        zero_counts = (count_ge(zero), count_tie(zero, jnp.full_like(lo_k, n_keys + 1)))
    ge0 = zero_counts[0]
    gt0 = ge0 - zero_counts[1]
    live = jnp.logical_and(need, lo_k < hi_k - 1)
    hit0 = jnp.logical_and(live, ge0 == kf)
    tie0 = jnp.logical_and(live, jnp.logical_and(gt0 < kf, ge0 > kf))
    thr = jnp.where(jnp.logical_or(hit0, tie0), zero, thr)
    done = jnp.where(jnp.logical_or(hit0, tie0), 1, done)
    zero_k = _f32_key(zero)
    lo_k = jnp.where(jnp.logical_and(live, ge0 > kf), jnp.maximum(lo_k, zero_k), lo_k)
    hi_k = jnp.where(jnp.logical_and(live, ge0 < kf), jnp.minimum(hi_k, zero_k), hi_k)
    state = (lo_k, hi_k, thr, jnp.where(tie0, gt0, zero), done, jnp.where(tie0, 1, 0).astype(jnp.int32))

    def step(linear, st):
        lo_k, hi_k, thr, cnt_hi, done, tie = st
        adjacent = hi_k == lo_k + 1
        lo_f, hi_f = _key_f32(lo_k), _key_f32(hi_k)
        mid_lin = _f32_key(lo_f + 0.5 * (hi_f - lo_f))
        mid_lin = jnp.minimum(jnp.maximum(mid_lin, lo_k + 1), hi_k - 1)
        mid_int = (lo_k & hi_k) + lax.shift_right_arithmetic(lo_k ^ hi_k, 1)
        mid = mid_lin if linear is True else jnp.where(linear, mid_lin, mid_int)
        mid_f = _key_f32(mid)
        cnt = count_ge(mid_f)
        live = jnp.logical_and(done == 0, jnp.logical_not(adjacent))
        hit = jnp.logical_and(live, cnt == kf)
        up = jnp.logical_and(live, cnt > kf)
        dn = jnp.logical_and(live, cnt < kf)
        new_tie = jnp.logical_and(done == 0, adjacent)
        thr = jnp.where(hit, mid_f, jnp.where(new_tie, lo_f, thr))
        tie = jnp.where(new_tie, 1, tie)
        done = jnp.where(jnp.logical_or(hit, new_tie), 1, done)
        lo_k = jnp.where(up, mid, lo_k)
        hi_k = jnp.where(dn, mid, hi_k)
        cnt_hi = jnp.where(dn, cnt, cnt_hi)
        return (lo_k, hi_k, thr, cnt_hi, done, tie)

    state = lax.fori_loop(0, fixed_steps, lambda _, st: step(True, st), state)

    def cond(st):
        it, active = st[0], st[1]
        return jnp.logical_and(it < 80, active > 0)

    def body(st):
        it = st[0]
        new = step(it < linear_steps, st[2:])
        return (it + 1, jnp.max(1 - new[4])) + new

    st = lax.while_loop(cond, body, (jnp.int32(fixed_steps), jnp.max(1 - state[4])) + state)
    thr, cnt_hi, tie = st[4], st[5], st[7]

    need_ties = kf - cnt_hi
    n_bits = max(1, int(math.ceil(math.log2(n_keys + 1))))
    any_tie = jnp.max(tie)

    def tie_body(_, lm):
        lo_m, hi_m = lm
        mid = lax.shift_right_arithmetic(lo_m + hi_m, 1)
        ge = count_tie(thr, mid) >= need_ties
        return jnp.where(ge, lo_m, mid), jnp.where(ge, mid, hi_m)

    lo_m0 = jnp.zeros_like(lo_k)
    hi_m0 = jnp.full_like(lo_k, n_keys)
    _, hi_m = lax.fori_loop(0, jnp.where(any_tie > 0, n_bits + 1, 0), tie_body, (lo_m0, hi_m0))
    cut = jnp.where(tie > 0, hi_m, n_keys + 1)
    return thr, cut


Q_TILE = 128
K_TILE = 256
K_UNROLL = 4
SCAN_UNROLL = 4
ONES_ROWS = 16


def _dsa_prompt_kernel(topk, qt_ref, iqt_ref, wt_ref, kb_ref, ikb_ref, vt_ref, o_ref, sc_ref, acc_ref, m_ref):
    i = pl.program_id(1)
    tq, kc = Q_TILE, K_TILE
    n_keys = sc_ref.shape[0] * kc
    nchunk = (i + 2) // 2
    qpos = i * tq + lax.broadcasted_iota(jnp.int32, (kc, tq), 1)
    krow = lax.broadcasted_iota(jnp.int32, (kc, tq), 0)
    qpos8 = qpos[0:SUBLANES]

    def col_reduce(x, op):
        return op(x.reshape(kc // SUBLANES, SUBLANES, tq), axis=0)

    def all_rows(x, op2):
        for shift in (4, 2, 1):
            x = op2(x, pltpu.roll(x, shift, 0))
        return x

    def tile_loop(first, rest, init, unroll):
        def trip(t, carry):
            heads = [first(t * unroll + sub) for sub in range(unroll)]
            for sub in range(unroll):
                carry = rest(t * unroll + sub, heads[sub], carry)
            return carry
        full = nchunk // unroll
        carry = lax.fori_loop(0, full, trip, init)
        return lax.fori_loop(full * unroll, nchunk, lambda c, carry: rest(c, first(c), carry), carry)

    def key_rows(c):
        return pl.ds(pl.multiple_of(c * kc, kc), kc)

    w = wt_ref[0]
    iqt = iqt_ref[0]
    s_scale = IDX_DIM ** -0.5 * IDX_HEADS ** -0.5

    def score_dots(c):
        return jnp.dot(ikb_ref[key_rows(c), :], iqt, preferred_element_type=jnp.float32)

    def score_tile(c, d, carry):
        mn, mx, ge0, eq0 = carry
        s = w[0:1] * jnp.maximum(d[:, 0:tq], 0.0)
        for h in range(1, IDX_HEADS):
            s = s + w[h:h + 1] * jnp.maximum(d[:, h * tq:(h + 1) * tq], 0.0)
        s = s * s_scale
        adm = c * kc + krow <= qpos
        sc_ref[c] = jnp.where(adm, s, NEG_INF)
        return (jnp.minimum(mn, col_reduce(jnp.where(adm, s, jnp.inf), jnp.min)),
                jnp.maximum(mx, col_reduce(jnp.where(adm, s, -jnp.inf), jnp.max)),
                ge0 + col_reduce(jnp.where(jnp.logical_and(adm, s >= 0.0), 1.0, 0.0), jnp.sum),
                eq0 + col_reduce(jnp.where(jnp.logical_and(adm, s == 0.0), 1.0, 0.0), jnp.sum))

    stat = lambda v: jnp.full((SUBLANES, tq), v, jnp.float32)
    mn, mx, ge0, eq0 = tile_loop(score_dots, score_tile, (stat(jnp.inf), stat(-jnp.inf), stat(0.0), stat(0.0)),
                                 K_UNROLL)

    n_scan = (nchunk + SCAN_UNROLL - 1) // SCAN_UNROLL

    def pad_tile(c, carry):
        sc_ref[c] = jnp.full((kc, tq), NEG_INF, jnp.float32)
        return carry

    lax.fori_loop(nchunk, n_scan * SCAN_UNROLL, pad_tile, 0)

    def scan(body, init):
        def trip(t, carry):
            for sub in range(SCAN_UNROLL):
                c = t * SCAN_UNROLL + sub
                carry = body(c, sc_ref[c], carry)
            return carry
        return lax.fori_loop(0, n_scan, trip, init)

    def count(pred):
        def body(c, s, acc):
            return acc + col_reduce(jnp.where(pred(s, c * kc + krow), 1.0, 0.0), jnp.sum)
        return all_rows(scan(body, jnp.zeros((SUBLANES, tq), jnp.float32)), jnp.add)

    def count_ge(c):
        return count(lambda s, kpos: s >= c[0:1])

    def count_tie(v, m):
        return count(lambda s, kpos: jnp.logical_and(s == v[0:1], kpos < m[0:1]))

    thr, cut = _topk_threshold(count_ge, count_tie, all_rows(mn, jnp.minimum), all_rows(mx, jnp.maximum),
                               qpos8 + 1, topk, n_keys,
                               zero_counts=(all_rows(ge0, jnp.add), all_rows(eq0, jnp.add)))
    thr_row, cut_row = thr[0:1], cut[0:1]

    m_ref[...] = jnp.full(m_ref.shape, 0.5 * NEG_INF, jnp.float32)
    acc_ref[...] = jnp.zeros(acc_ref.shape, jnp.float32)
    qt = qt_ref[0]
    kw = N_KV_A * HEAD_DIM_A

    def logits(c):
        return jnp.dot(kb_ref[key_rows(c), :], qt, preferred_element_type=jnp.float32)

    def attend_group(cs, lgs):
        biases = []
        for c in cs:
            s = sc_ref[c]
            kpos = c * kc + krow
            sel = jnp.logical_or(s > thr_row, jnp.logical_and(s == thr_row, kpos < cut_row))
            biases.append(jnp.where(jnp.logical_and(sel, kpos <= qpos), 0.0, NEG_INF))
        ps, alphas = [], []
        for h in range(N_HEADS_A):
            cols = slice(h * tq, (h + 1) * tq)
            lghs = [lg[:, cols] + bias for lg, bias in zip(lgs, biases)]
            tile_max = functools.reduce(jnp.maximum, [col_reduce(lgh, jnp.max) for lgh in lghs])
            m_old = m_ref[:, cols]
            m_new = jnp.maximum(m_old, all_rows(tile_max, jnp.maximum))
            alphas.append(jnp.exp2(m_old - m_new)[0:1])
            ps.append(jnp.concatenate([jnp.exp2(lgh - m_new[0:1]).astype(MXU_DTYPE) for lgh in lghs], axis=0))
            m_ref[:, cols] = m_new
        vt = jnp.concatenate([vt_ref[c] for c in cs], axis=1)
        pv = jnp.dot(vt, jnp.concatenate(ps, axis=1), preferred_element_type=jnp.float32)
        acc_ref[...] = acc_ref[...] * jnp.concatenate(alphas, axis=1) + pv

    def attend_trip(t, carry):
        cs = [t * K_UNROLL + sub for sub in range(K_UNROLL)]
        lgs = [logits(c) for c in cs]
        for j in range(0, K_UNROLL, 2):
            attend_group(cs[j:j + 2], lgs[j:j + 2])
        return carry

    def attend_single(c, carry):
        attend_group([c], [logits(c)])
        return carry

    full_trips = nchunk // K_UNROLL
    lax.fori_loop(0, full_trips, attend_trip, 0)
    lax.fori_loop(full_trips * K_UNROLL, nchunk, attend_single, 0)

    acc = acc_ref[...]
    o_t = acc[0:kw] / acc[kw:kw + 1]
    for p in range(N_HEADS_A // 2):
        n = (2 * p) // GROUP_A
        pair = jnp.concatenate([o_t[n * HEAD_DIM_A:(n + 1) * HEAD_DIM_A, (2 * p + e) * tq:(2 * p + e + 1) * tq]
                                for e in range(2)], axis=0)
        o_ref[:, p * LANES:(p + 1) * LANES] = pair.T.astype(o_ref.dtype)


def _dsa_prompt(qt, iqt, wt, kb, ikb, vt, batch, seq):
    tq, kc = Q_TILE, K_TILE
    assert seq % kc == 0
    nq = seq // tq
    nk = seq // kc
    n = batch * seq
    kw = N_KV_A * HEAD_DIM_A
    topk = min(TOPK_MAX, seq // 4)
    return pl.pallas_call(
        functools.partial(_dsa_prompt_kernel, topk),
        grid=(batch, nq),
        in_specs=[
            pl.BlockSpec((1, LANES, N_HEADS_A * tq), lambda b, i: (b * nq + i, 0, 0)),
            pl.BlockSpec((1, IDX_DIM, IDX_HEADS * tq), lambda b, i: (b * nq + i, 0, 0)),
            pl.BlockSpec((1, IDX_HEADS, tq), lambda b, i: (b * nq + i, 0, 0)),
            pl.BlockSpec((seq, kw), lambda b, i: (b, 0)),
            pl.BlockSpec((seq, IDX_DIM), lambda b, i: (b, 0)),
            pl.BlockSpec((nk, kw + ONES_ROWS, kc), lambda b, i: (b, 0, 0)),
        ],
        out_specs=pl.BlockSpec((tq, BRANCH_WIDTH), lambda b, i: (b * nq + i, 0)),
        out_shape=jax.ShapeDtypeStruct((n, BRANCH_WIDTH), MXU_DTYPE),
        scratch_shapes=[
            pltpu.VMEM((-(-nk // SCAN_UNROLL) * SCAN_UNROLL, kc, tq), jnp.float32),
            pltpu.VMEM((kw + ONES_ROWS, N_HEADS_A * tq), jnp.float32),
            pltpu.VMEM((SUBLANES, N_HEADS_A * tq), jnp.float32),
        ],
        compiler_params=pltpu.CompilerParams(dimension_semantics=("arbitrary", "arbitrary"),
                                             vmem_limit_bytes=VMEM_LIMIT_BYTES),
        name="dsa_prompt",
    )(qt, iqt, wt, kb, ikb, vt)


def _page_copy(pt_ref, cache_ref, buf_ref, sem_ref, seq, page, slot):
    lanes = pl.ds(page * PAGE_SIZE, PAGE_SIZE)
    return pltpu.make_async_copy(cache_ref.at[pt_ref[seq, page]], buf_ref.at[slot, :, lanes], sem_ref.at[slot])


def _pages_start(pt_ref, cache_ref, buf_ref, sem_ref, seq, slot, n_pages):
    for p in range(n_pages):
        _page_copy(pt_ref, cache_ref, buf_ref, sem_ref, seq, p, slot).start()


def _pages_wait(pt_ref, cache_ref, buf_ref, sem_ref, seq, slot, n_pages):
    for p in range(n_pages):
        _page_copy(pt_ref, cache_ref, buf_ref, sem_ref, seq, p, slot).wait()


def _dsa_sample_score_kernel(topk, n_pages, pt_ref, iq_ref, iw_ref, iknew_ref, cache_ik_ref,
                             sc_ref, thr_ref, cut_ref, ikbuf_ref, sem_ref):
    s = pl.program_id(0)
    n_seq = pl.num_programs(0)
    past = n_pages * PAGE_SIZE
    n_tiles = sc_ref.shape[0]
    slot = s % 2

    @pl.when(s == 0)
    def _():
        _pages_start(pt_ref, cache_ik_ref, ikbuf_ref, sem_ref, 0, 0, n_pages)

    @pl.when(s + 1 < n_seq)
    def _():
        _pages_start(pt_ref, cache_ik_ref, ikbuf_ref, sem_ref, s + 1, 1 - slot, n_pages)

    _pages_wait(pt_ref, cache_ik_ref, ikbuf_ref, sem_ref, s, slot, n_pages)

    iq = iq_ref[0]
    w = iw_ref[0]
    s_scale = IDX_DIM ** -0.5 * IDX_HEADS ** -0.5
    d = _mm(iq, ikbuf_ref[slot])
    srow = jnp.sum(w * jnp.maximum(d, 0.0), axis=0, keepdims=True) * s_scale
    for j in range(n_pages):
        sc_ref[j, pl.ds(s, 1), :] = srow[:, j * LANES:(j + 1) * LANES]
    ik_new = iknew_ref[0].astype(MXU_DTYPE).astype(jnp.float32)
    d_self = jnp.sum(iq.astype(jnp.float32) * ik_new, axis=1, keepdims=True)
    s_self = jnp.sum(w * jnp.maximum(d_self, 0.0), axis=0, keepdims=True) * s_scale
    lane1 = lax.broadcasted_iota(jnp.int32, (1, LANES), 1)
    sc_ref[n_tiles - 1, pl.ds(s, 1), :] = jnp.where(lane1 == 0, s_self, NEG_INF)

    @pl.when(s == n_seq - 1)
    def _():
        rows = sc_ref.shape[1]
        lane = lax.broadcasted_iota(jnp.int32, (rows, LANES), 1)

        def count(pred):
            def body(j, acc):
                return acc + jnp.where(pred(sc_ref[j], j * LANES + lane), 1.0, 0.0)
            acc = lax.fori_loop(0, n_tiles, body, jnp.zeros((rows, LANES), jnp.float32))
            return jnp.broadcast_to(jnp.sum(acc, axis=1, keepdims=True), (rows, LANES))

        def count_ge(c):
            return count(lambda t, kpos: t >= c)

        def count_tie(v, m):
            return count(lambda t, kpos: jnp.logical_and(t == v, kpos < m))

        def minmax(j, mm):
            t = sc_ref[j]
            adm = j * LANES + lane <= past
            return (jnp.minimum(mm[0], jnp.where(adm, t, jnp.inf)),
                    jnp.maximum(mm[1], jnp.where(adm, t, -jnp.inf)))

        mn, mx = lax.fori_loop(0, n_tiles, minmax, (jnp.full((rows, LANES), jnp.inf, jnp.float32),
                                                    jnp.full((rows, LANES), -jnp.inf, jnp.float32)))
        row_min = jnp.broadcast_to(jnp.min(mn, axis=1, keepdims=True), (rows, LANES))
        row_max = jnp.broadcast_to(jnp.max(mx, axis=1, keepdims=True), (rows, LANES))
        n_adm = jnp.full((rows, LANES), past + 1, jnp.int32)
        thr, cut = _topk_threshold(count_ge, count_tie, row_min, row_max, n_adm, topk, n_tiles * LANES)
        thr_ref[...] = thr
        cut_ref[...] = cut


def _dsa_sample_attend_kernel(n_pages, pt_ref, q_ref, sc_ref, thr_ref, cut_ref, kvnew_ref,
                              cache_k_ref, cache_v_ref, o_ref, kbuf_ref, vbuf_ref, ksem_ref, vsem_ref):
    s = pl.program_id(0)
    n_seq = pl.num_programs(0)
    past = n_pages * PAGE_SIZE
    slot = s % 2

    def start(seq, sl):
        _pages_start(pt_ref, cache_k_ref, kbuf_ref, ksem_ref, seq, sl, n_pages)
        _pages_start(pt_ref, cache_v_ref, vbuf_ref, vsem_ref, seq, sl, n_pages)

    @pl.when(s == 0)
    def _():
        start(0, 0)

    @pl.when(s + 1 < n_seq)
    def _():
        start(s + 1, 1 - slot)

    _pages_wait(pt_ref, cache_k_ref, kbuf_ref, ksem_ref, s, slot, n_pages)
    _pages_wait(pt_ref, cache_v_ref, vbuf_ref, vsem_ref, s, slot, n_pages)

    q = q_ref[0]
    thr = thr_ref[0][:, 0:1]
    cut = cut_ref[0][:, 0:1]
    kw = N_KV_A * HEAD_DIM_A
    k_new = kvnew_ref[0][:, 0:kw].astype(MXU_DTYPE).astype(jnp.float32)
    v_new = kvnew_ref[0][:, kw:2 * kw].astype(MXU_DTYPE).astype(jnp.float32)
    sc = sc_ref[0]

    def selected(srow, kpos):
        return jnp.logical_or(srow > thr, jnp.logical_and(srow == thr, kpos < cut))

    kpos = lax.broadcasted_iota(jnp.int32, (1, past), 1)
    bias = jnp.where(selected(sc[:, 0:past], kpos), 0.0, NEG_INF)
    lg = _mm(q, kbuf_ref[slot]) + bias
    lg_self = jnp.sum(q.astype(jnp.float32) * k_new, axis=1, keepdims=True)
    lg_self = jnp.where(selected(sc[:, past:past + 1], past), lg_self, NEG_INF)
    m = jnp.maximum(jnp.max(lg, axis=1, keepdims=True), lg_self)
    p = jnp.exp(lg - m)
    p_self = jnp.exp(lg_self - m)
    denom = jnp.sum(p, axis=1, keepdims=True) + p_self
    o = (_mm_nt(p, vbuf_ref[slot]) + p_self * v_new) / denom
    parts = []
    for h in range(N_HEADS_A):
        n = h // GROUP_A
        parts.append(o[h:h + 1, n * HEAD_DIM_A:(n + 1) * HEAD_DIM_A])
    o_ref[0] = jnp.concatenate(parts, axis=1)


def _dsa_sample(qexp, iqhm, small, ik_new, kv_new, cache_k, cache_v, cache_ik, page_table):
    db, n_pages = page_table.shape
    past = n_pages * PAGE_SIZE
    n_pool = cache_ik.shape[0]
    topk = min(TOPK_MAX, (past + 1) // 4)
    n_tiles = n_pages + 1
    kw = N_KV_A * HEAD_DIM_A
    q_s = jnp.swapaxes(qexp, 0, 1)
    iq_s = jnp.swapaxes(iqhm, 0, 1)
    iw_s = small[:, 0:IDX_HEADS].reshape(db, IDX_HEADS, 1)
    ck_t = jnp.transpose(cache_k, (0, 2, 3, 1)).reshape(n_pool, kw, PAGE_SIZE)
    cv_t = jnp.transpose(cache_v, (0, 2, 3, 1)).reshape(n_pool, kw, PAGE_SIZE)
    cik_t = jnp.swapaxes(cache_ik, 1, 2)
    cparams = pltpu.CompilerParams(dimension_semantics=("arbitrary",), vmem_limit_bytes=VMEM_LIMIT_BYTES)
    per_seq = lambda *shape: pl.BlockSpec((1,) + shape, lambda s, pt: (s,) + (0,) * len(shape))
    whole = lambda *shape: pl.BlockSpec(shape, lambda s, pt: (0,) * len(shape))
    any_spec = pl.BlockSpec(memory_space=pl.ANY)

    sc, thr, cut = pl.pallas_call(
        functools.partial(_dsa_sample_score_kernel, topk, n_pages),
        grid_spec=pltpu.PrefetchScalarGridSpec(
            num_scalar_prefetch=1,
            grid=(db,),
            in_specs=[per_seq(IDX_HEADS, IDX_DIM), per_seq(IDX_HEADS, 1), per_seq(1, IDX_DIM), any_spec],
            out_specs=(whole(n_tiles, db, LANES), whole(db, LANES), whole(db, LANES)),
            scratch_shapes=[pltpu.VMEM((2, IDX_DIM, past), jnp.float32), pltpu.SemaphoreType.DMA((2,))],
        ),
        out_shape=(jax.ShapeDtypeStruct((n_tiles, db, LANES), jnp.float32),
                   jax.ShapeDtypeStruct((db, LANES), jnp.float32),
                   jax.ShapeDtypeStruct((db, LANES), jnp.int32)),
        compiler_params=cparams,
        name="dsa_sample_score",
    )(page_table, iq_s, iw_s, ik_new.reshape(db, 1, IDX_DIM), cik_t)

    o = pl.pallas_call(
        functools.partial(_dsa_sample_attend_kernel, n_pages),
        grid_spec=pltpu.PrefetchScalarGridSpec(
            num_scalar_prefetch=1,
            grid=(db,),
            in_specs=[per_seq(N_HEADS_A, LANES), per_seq(1, n_tiles * LANES), per_seq(1, LANES), per_seq(1, LANES),
                      per_seq(1, 2 * kw), any_spec, any_spec],
            out_specs=per_seq(1, BRANCH_WIDTH),
            scratch_shapes=[pltpu.VMEM((2, kw, past), jnp.float32), pltpu.VMEM((2, kw, past), jnp.float32),
                            pltpu.SemaphoreType.DMA((2,)), pltpu.SemaphoreType.DMA((2,))],
        ),
        out_shape=jax.ShapeDtypeStruct((db, 1, BRANCH_WIDTH), jnp.float32),
        compiler_params=cparams,
        name="dsa_sample_attend",
    )(page_table, q_s, jnp.swapaxes(sc, 0, 1).reshape(db, 1, n_tiles * LANES),
      thr.reshape(db, 1, LANES), cut.reshape(db, 1, LANES),
      kv_new.reshape(db, 1, 2 * kw), ck_t, cv_t)
    return o.reshape(db, BRANCH_WIDTH)


GDN_CHUNK = 128
A_LANE = IDX_HEADS
B_LANE = IDX_HEADS + H_B


def _split2(x):
    h = x.astype(MXU_DTYPE)
    return h, (x - h.astype(jnp.float32)).astype(MXU_DTYPE)


def _mm2(a, b):
    a1, a2 = _split2(a)
    b1, b2 = _split2(b)
    d = functools.partial(jnp.dot, preferred_element_type=jnp.float32)
    return d(a1, b1) + (d(a1, b2) + d(a2, b1))


def _unit_lower_inverses(mats):
    n = mats[0].shape[0]
    eye = (lax.broadcasted_iota(jnp.int32, (n, n), 0) == lax.broadcasted_iota(jnp.int32, (n, n), 1))
    ss = [jnp.where(eye, 1.0, 0.0) - a for a in mats]
    ps = [_mm2(a, a) for a in mats]
    k = 2
    while k < n:
        ss = [s + _mm2(s, p) for s, p in zip(ss, ps)]
        k *= 2
        if k < n:
            ps = [_mm2(p, p) for p in ps]
    return ss


def _l2norm(x):
    return x * lax.rsqrt(jnp.sum(x * x, axis=-1, keepdims=True) + NORM_EPS)


def _gdn_prompt_kernel(u_ref, small_ref, z_ref, conv0_ref, s0_ref, wconv_ref, alog_ref, dtb_ref, dnorm_ref,
                       o_ref, conv_out_ref, s_out_ref, ucat_ref, state_ref):
    n = pl.program_id(0)
    nb = u_ref.shape[0]
    c = GDN_CHUNK
    head = SUBLANES
    tail = CONV_K - 1

    @pl.when(n == 0)
    def _():
        ucat_ref[:, head - tail:head, :] = conv0_ref[...]
        state_ref[...] = s0_ref[...]

    row = lax.broadcasted_iota(jnp.int32, (c, c), 0)
    col = lax.broadcasted_iota(jnp.int32, (c, c), 1)
    lower = row >= col
    strict = row > col
    ltri = jnp.where(lower, 1.0, 0.0).astype(MXU_DTYPE)
    d = functools.partial(jnp.dot, preferred_element_type=jnp.float32)

    chains = [(b, h) for b in range(nb) for h in range(H_B)]
    qw = H_B * DK_B
    qs, ks, vs, betas, gcols = [], [], [], [], []
    for b in range(nb):
        ucat_ref[b, head:head + c, :] = u_ref[b]
        y = wconv_ref[tail:tail + 1, :] * ucat_ref[b, head:head + c, :]
        for j in range(tail):
            y = y + wconv_ref[j:j + 1, :] * ucat_ref[b, head - tail + j:head - tail + j + c, :]
        cv = _silu(y)
        carry_rows = ucat_ref[b, head + c - tail:head + c, :]
        ucat_ref[b, head - tail:head, :] = carry_rows
        conv_out_ref[b] = carry_rows
        sm = small_ref[b]
        g_all = -jnp.exp(alog_ref[...]) * _softplus(sm + dtb_ref[...])
        beta_all = _sigmoid(sm)
        a1, a2, a3 = _split3(g_all)
        gc_all = d(ltri, a1) + (d(ltri, a2) + d(ltri, a3))
        for h in range(H_B):
            qs.append(_l2norm(cv[:, h * DK_B:(h + 1) * DK_B]) * (DK_B ** -0.5))
            ks.append(_l2norm(cv[:, qw + h * DK_B:qw + (h + 1) * DK_B]))
            vs.append(cv[:, 2 * qw + h * DV_B:2 * qw + (h + 1) * DV_B])
            betas.append(jnp.broadcast_to(beta_all[:, B_LANE + h:B_LANE + h + 1], (c, LANES)))
            gcols.append(jnp.broadcast_to(gc_all[:, A_LANE + h:A_LANE + h + 1], (c, c)))
    decays = [jnp.where(lower, jnp.exp(jnp.where(lower, g - g.T, 0.0)), 0.0) for g in gcols]
    egs = [jnp.exp(g) for g in gcols]
    g_lasts = [g[c - 1:c, :] for g in gcols]
    kbs = [k * b for k, b in zip(ks, betas)]
    vbs = [v * b for v, b in zip(vs, betas)]
    kks = [_mm_nt(kb, k) for kb, k in zip(kbs, ks)]
    qks = [_mm_nt(q, k) for q, k in zip(qs, ks)]
    t_invs = _unit_lower_inverses([jnp.where(strict, kk * dc, 0.0) for kk, dc in zip(kks, decays)])
    sols = [_mm2(t, jnp.concatenate([vb, kb * eg], axis=1)) for t, vb, kb, eg in zip(t_invs, vbs, kbs, egs)]
    s_olds = [state_ref[b, h] for b, h in chains]
    v_news = [sol[:, 0:DV_B] - _mm(sol[:, DV_B:DV_B + DK_B], s) for sol, s in zip(sols, s_olds)]
    o_hs = [_mm(q * eg, s) + _mm(qk * dc, v_new)
            for q, eg, s, qk, dc, v_new in zip(qs, egs, s_olds, qks, decays, v_news)]
    for i, (b, h) in enumerate(chains):
        k_dec = ks[i] * jnp.exp(g_lasts[i] - gcols[i])
        state_ref[b, h] = s_olds[i] * jnp.exp(g_lasts[i]) + _mm_tn(k_dec, v_news[i])
    for i, (b, h) in enumerate(chains):
        o_h = o_hs[i]
        ms = jnp.mean(o_h * o_h, axis=-1, keepdims=True)
        o_n = o_h * lax.rsqrt(ms + NORM_EPS) * dnorm_ref[...]
        gate = _silu(z_ref[b, :, h * DV_B:(h + 1) * DV_B])
        o_ref[b, :, h * DV_B:(h + 1) * DV_B] = (o_n * gate).astype(o_ref.dtype)

    @pl.when(n == pl.num_programs(0) - 1)
    def _():
        s_out_ref[...] = state_ref[...]


GDN_SEQ_TILE = 8


def _gdn_sample_kernel(u_ref, cb_ref, small_ref, z_ref, s0_ref, wconv_ref, alog_ref, dtb_ref, dnorm_ref,
                       o_ref, conv_out_ref, s_out_ref):
    ts = GDN_SEQ_TILE
    tail = CONV_K - 1
    u_new = u_ref[...]
    y = wconv_ref[tail:tail + 1, :] * u_new
    for j in range(tail):
        y = y + wconv_ref[j:j + 1, :] * cb_ref[j]
    cv = _silu(y)
    for j in range(tail - 1):
        conv_out_ref[j] = cb_ref[j + 1]
    conv_out_ref[tail - 1] = u_new

    sm = small_ref[...]
    eg_all = jnp.exp(-jnp.exp(alog_ref[...]) * _softplus(sm + dtb_ref[...]))
    beta_all = _sigmoid(sm)
    qw = H_B * DK_B
    for h in range(H_B):
        q = _l2norm(cv[:, h * DK_B:(h + 1) * DK_B]) * (DK_B ** -0.5)
        k = _l2norm(cv[:, qw + h * DK_B:qw + (h + 1) * DK_B])
        v = cv[:, 2 * qw + h * DV_B:2 * qw + (h + 1) * DV_B]
        eg = eg_all[:, A_LANE + h:A_LANE + h + 1]
        beta = beta_all[:, B_LANE + h:B_LANE + h + 1]
        qk = jnp.sum(q * k, axis=-1, keepdims=True)
        k_t, q_t = k.T, q.T
        rows = []
        for r in range(ts):
            s_old = s0_ref[r, h]
            kc = k_t[:, r:r + 1]
            ks = jnp.sum(s_old * kc, axis=0, keepdims=True)
            qs = jnp.sum(s_old * q_t[:, r:r + 1], axis=0, keepdims=True)
            eg_r = eg[r:r + 1, :]
            v_new = beta[r:r + 1, :] * (v[r:r + 1, :] - eg_r * ks)
            rows.append(eg_r * qs + qk[r:r + 1, :] * v_new)
            s_out_ref[r, h] = s_old * eg_r + kc * v_new
        o_h = jnp.concatenate(rows, axis=0)
        ms = jnp.mean(o_h * o_h, axis=-1, keepdims=True)
        o_n = o_h * lax.rsqrt(ms + NORM_EPS) * dnorm_ref[...]
        o_ref[:, h * DV_B:(h + 1) * DV_B] = (o_n * _silu(z_ref[:, h * DV_B:(h + 1) * DV_B])).astype(o_ref.dtype)


def _gdn_sample(u, small, z, conv_buf, s0, w_conv, a_log, dt_bias, delta_norm):
    db = u.shape[0]
    ts = GDN_SEQ_TILE
    assert db % ts == 0
    tail = CONV_K - 1
    alog_row, dtb_row = _gate_rows(a_log, dt_bias)
    row = lambda w: pl.BlockSpec((ts, w), lambda i: (i, 0))
    full = lambda *shape: pl.BlockSpec(shape, lambda i: (0,) * len(shape))
    cb_spec = pl.BlockSpec((tail, ts, CONV_DIM), lambda i: (0, i, 0))
    st_spec = pl.BlockSpec((ts, H_B, DK_B, DV_B), lambda i: (i, 0, 0, 0))
    o, conv_t, s_new = pl.pallas_call(
        _gdn_sample_kernel,
        grid=(db // ts,),
        in_specs=[row(CONV_DIM), cb_spec, row(SMALL_W), row(H_B * DV_B), st_spec,
                  full(CONV_K, CONV_DIM), full(1, SMALL_W), full(1, SMALL_W), full(1, DV_B)],
        out_specs=(row(H_B * DV_B), cb_spec, st_spec),
        out_shape=(jax.ShapeDtypeStruct((db, H_B * DV_B), jnp.float32),
                   jax.ShapeDtypeStruct((tail, db, CONV_DIM), jnp.float32),
                   jax.ShapeDtypeStruct((db, H_B, DK_B, DV_B), jnp.float32)),
        compiler_params=pltpu.CompilerParams(dimension_semantics=("arbitrary",),
                                             vmem_limit_bytes=VMEM_LIMIT_BYTES),
        name="gdn_sample",
    )(u, jnp.swapaxes(conv_buf, 0, 1), small, z, s0, w_conv.astype(jnp.float32), alog_row, dtb_row,
      delta_norm.astype(jnp.float32)[None, :])
    return o, jnp.swapaxes(conv_t, 0, 1), s_new


def _gate_rows(a_log, dt_bias):
    alog_row = jnp.zeros((1, SMALL_W), jnp.float32).at[0, A_LANE:A_LANE + H_B].set(a_log.astype(jnp.float32))
    dtb_row = jnp.zeros((1, SMALL_W), jnp.float32).at[0, A_LANE:A_LANE + H_B].set(dt_bias.astype(jnp.float32))
    return alog_row, dtb_row


def _gdn_prompt(u, small, z, conv0, s0, w_conv, a_log, dt_bias, delta_norm, batch, seq):
    c = GDN_CHUNK
    assert seq % c == 0
    alog_row, dtb_row = _gate_rows(a_log, dt_bias)
    row = lambda w: pl.BlockSpec((batch, c, w), lambda i: (0, i, 0))
    full = lambda *shape: pl.BlockSpec(shape, lambda i: (0,) * len(shape))
    o, conv_new, s_new = pl.pallas_call(
        _gdn_prompt_kernel,
        grid=(seq // c,),
        in_specs=[row(CONV_DIM), row(SMALL_W), row(H_B * DV_B), full(batch, CONV_K - 1, CONV_DIM),
                  full(batch, H_B, DK_B, DV_B), full(CONV_K, CONV_DIM), full(1, SMALL_W), full(1, SMALL_W),
                  full(1, DV_B)],
        out_specs=(row(H_B * DV_B), full(batch, CONV_K - 1, CONV_DIM), full(batch, H_B, DK_B, DV_B)),
        out_shape=(jax.ShapeDtypeStruct((batch, seq, H_B * DV_B), MXU_DTYPE),
                   jax.ShapeDtypeStruct((batch, CONV_K - 1, CONV_DIM), jnp.float32),
                   jax.ShapeDtypeStruct((batch, H_B, DK_B, DV_B), jnp.float32)),
        scratch_shapes=[pltpu.VMEM((batch, SUBLANES + c, CONV_DIM), jnp.float32),
                        pltpu.VMEM((batch, H_B, DK_B, DV_B), jnp.float32)],
        compiler_params=pltpu.CompilerParams(dimension_semantics=("arbitrary",),
                                             vmem_limit_bytes=VMEM_LIMIT_BYTES),
        name="gdn_prompt",
    )(u.reshape(batch, seq, CONV_DIM), small.reshape(batch, seq, SMALL_W), z.reshape(batch, seq, H_B * DV_B),
      conv0, s0, w_conv.astype(jnp.float32), alog_row, dtb_row, delta_norm.astype(jnp.float32)[None, :])
    return o.reshape(batch * seq, H_B * DV_B), conv_new, s_new


def _merge_kernel(x_ref, oa_ref, ob_ref, gl_ref, wba_ref, wbb_ref, wout_ref, gain_ref, x1_ref, hn_ref):
    pa = _mm(oa_ref[...], wba_ref[...])
    pb = _mm(ob_ref[...], wbb_ref[...])
    mix = _sigmoid(gl_ref[:, 0:D_MODEL]) * pa + _sigmoid(gl_ref[:, D_MODEL:2 * D_MODEL]) * pb
    x1 = x_ref[...] + _mm(mix, wout_ref[...])
    x1_ref[...] = x1
    ms = jnp.mean(x1 * x1, axis=-1, keepdims=True)
    hn_ref[...] = (x1 * lax.rsqrt(ms + NORM_EPS) * gain_ref[...]).astype(hn_ref.dtype)


def _merge(x2d, o_a, o_b, gl, w_branch, w_out, norm_ffn, tm):
    n = x2d.shape[0]
    assert n % tm == 0
    row = lambda w: pl.BlockSpec((tm, w), lambda i: (i, 0))
    full = lambda *shape: pl.BlockSpec(shape, lambda i: (0,) * len(shape))
    return pl.pallas_call(
        _merge_kernel,
        grid=(n // tm,),
        in_specs=[row(D_MODEL), row(BRANCH_WIDTH), row(BRANCH_WIDTH), row(2 * D_MODEL),
                  full(BRANCH_WIDTH, D_MODEL), full(BRANCH_WIDTH, D_MODEL), full(D_MODEL, D_MODEL),
                  full(1, D_MODEL)],
        out_specs=(row(D_MODEL), row(D_MODEL)),
        out_shape=(jax.ShapeDtypeStruct((n, D_MODEL), jnp.float32),
                   jax.ShapeDtypeStruct((n, D_MODEL), MXU_DTYPE)),
        compiler_params=pltpu.CompilerParams(dimension_semantics=("arbitrary",),
                                             vmem_limit_bytes=VMEM_LIMIT_BYTES),
        name="merge",
    )(x2d, o_a, o_b, gl, w_branch[0].astype(MXU_DTYPE), w_branch[1].astype(MXU_DTYPE),
      w_out.astype(MXU_DTYPE), norm_ffn.astype(jnp.float32)[None, :])


FFN_TILE = D_FF // 2


def _ffn_kernel(hn_ref, x1_ref, wg_ref, wu_ref, wd_ref, y_ref, acc_ref):
    j = pl.program_id(1)

    @pl.when(j == 0)
    def _():
        acc_ref[...] = x1_ref[...]

    hn = hn_ref[...]
    g = jnp.dot(hn, wg_ref[...], preferred_element_type=jnp.float32)
    u = jnp.dot(hn, wu_ref[...], preferred_element_type=jnp.float32)
    acc_ref[...] += _mm(_silu(g) * u, wd_ref[...])

    @pl.when(j == pl.num_programs(1) - 1)
    def _():
        y_ref[...] = acc_ref[...]


def _ffn(hn, x1, w_gate_up, w_down, tm):
    n = hn.shape[0]
    tf = FFN_TILE
    assert n % tm == 0 and D_FF % tf == 0 and tf % LANES == 0
    nf = D_FF // tf
    wgu = w_gate_up.astype(MXU_DTYPE)
    return pl.pallas_call(
        _ffn_kernel,
        grid=(n // tm, nf),
        in_specs=[pl.BlockSpec((tm, D_MODEL), lambda i, j: (i, 0)),
                  pl.BlockSpec((tm, D_MODEL), lambda i, j: (i, 0)),
                  pl.BlockSpec((D_MODEL, tf), lambda i, j: (0, j)),
                  pl.BlockSpec((D_MODEL, tf), lambda i, j: (0, j + nf)),
                  pl.BlockSpec((tf, D_MODEL), lambda i, j: (j, 0))],
        out_specs=pl.BlockSpec((tm, D_MODEL), lambda i, j: (i, 0)),
        out_shape=jax.ShapeDtypeStruct((n, D_MODEL), jnp.float32),
        scratch_shapes=[pltpu.VMEM((tm, D_MODEL), jnp.float32)],
        compiler_params=pltpu.CompilerParams(dimension_semantics=("arbitrary", "arbitrary"),
                                             vmem_limit_bytes=VMEM_LIMIT_BYTES),
        name="ffn",
    )(hn, x1, wgu, wgu, w_down.astype(MXU_DTYPE))


IN_PROJ_TILE = K_TILE
MERGE_TILE = 512
FFN_ROW_TILE = 512


def _layer(x_p, x_s, cache_k, cache_v, cache_ik, conv_s, delta_s, page_table, norm_mix, w_in, q_norm, k_norm,
           w_conv, a_log, dt_bias, delta_norm, w_branch, w_out, norm_ffn, w_gate_up, w_down):
    b, t, d = x_p.shape
    db = x_s.shape[0]
    past = page_table.shape[1] * PAGE_SIZE
    kw = N_KV_A * HEAD_DIM_A
    w_packed = _pack_w_in(w_in)

    xp2 = x_p.reshape(b * t, d)
    tm = IN_PROJ_TILE
    assert t % tm == 0
    kv, ik, small, u, z, gl, qt, iqt, wt, kb, ikb, vt = _in_proj(
        xp2, _rope_tables(jnp.arange(t)), t // tm, tm, norm_mix, w_packed, q_norm, k_norm, key_major=True)
    o_a = _dsa_prompt(qt, iqt, wt, kb, ikb, vt, b, t)
    conv0 = jnp.zeros((b, CONV_K - 1, CONV_DIM), jnp.float32)
    delta0 = jnp.zeros((b, H_B, DK_B, DV_B), jnp.float32)
    o_b, conv_p, delta_p = _gdn_prompt(u, small, z, conv0, delta0, w_conv, a_log, dt_bias, delta_norm, b, t)
    x1, hn = _merge(xp2, o_a, o_b, gl, w_branch, w_out, norm_ffn, min(MERGE_TILE, b * t))
    y_p = _ffn(hn, x1, w_gate_up, w_down, min(FFN_ROW_TILE, b * t)).reshape(b, t, d)
    kv5 = kv.reshape(b, 2, N_KV_A, HEAD_DIM_A, t)
    st_p = (jnp.transpose(kv5[:, 0], (0, 3, 1, 2)), jnp.transpose(kv5[:, 1], (0, 3, 1, 2)),
            jnp.swapaxes(ik, 1, 2), conv_p, delta_p)

    xs2 = x_s.reshape(db, d)
    kv, ik, small, u, z, gl, qexp, iqhm = _in_proj(
        xs2, _rope_tables(jnp.full((db,), past, jnp.int32)), 1, db, norm_mix, w_packed, q_norm, k_norm,
        key_major=False)
    o_a = _dsa_sample(qexp, iqhm, small, ik, kv, cache_k, cache_v, cache_ik, page_table)
    o_b, conv_n, delta_n = _gdn_sample(u, small, z, conv_s, delta_s, w_conv, a_log, dt_bias, delta_norm)
    x1, hn = _merge(xs2, o_a, o_b, gl, w_branch, w_out, norm_ffn, db)
    y_s = _ffn(hn, x1, w_gate_up, w_down, db).reshape(db, 1, d)
    st_s = (kv[:, 0:kw].reshape(db, 1, N_KV_A, HEAD_DIM_A), kv[:, kw:2 * kw].reshape(db, 1, N_KV_A, HEAD_DIM_A),
            ik.reshape(db, 1, IDX_DIM), conv_n, delta_n)
    return y_p, y_s, st_p, st_s


def kernel(x_prompt, x_sample, cache_k, cache_v, cache_idx_k, state_conv, state_delta, page_table,
           norm_mix, w_in, q_norm, k_norm, w_conv, a_log, dt_bias, delta_norm, w_branch, w_out,
           norm_ffn, w_gate_up, w_down):
    assert x_sample.shape[1] == 1, "the sample group decodes one token per sequence"
    y_p, y_s = x_prompt, x_sample
    new_p, new_s = [], []
    for l in range(w_in.shape[0]):
        y_p, y_s, st_p, st_s = _layer(
            y_p, y_s, cache_k[l], cache_v[l], cache_idx_k[l], state_conv[l], state_delta[l], page_table,
            norm_mix[l], w_in[l], q_norm[l], k_norm[l], w_conv[l], a_log[l], dt_bias[l], delta_norm[l],
            w_branch[l], w_out[l], norm_ffn[l], w_gate_up[l], w_down[l])
        new_p.append(st_p)
        new_s.append(st_s)
    k_p, v_p, ik_p, conv_p, delta_p = [jnp.stack(a) for a in zip(*new_p)]
    k_s, v_s, ik_s, conv_s, delta_s = [jnp.stack(a) for a in zip(*new_s)]
    return (y_p, y_s, k_p, v_p, ik_p, conv_p, delta_p, k_s, v_s, ik_s, conv_s, delta_s)
```

```python
import functools
import math

import jax
import jax.numpy as jnp
import numpy as np
from jax import lax
from jax.experimental import pallas as pl
from jax.experimental.pallas import tpu as pltpu

D_MODEL = 1024
PAGE_SIZE = 128
N_HEADS_A = 8
N_KV_A = 2
HEAD_DIM_A = 64
GROUP_A = N_HEADS_A // N_KV_A
IDX_HEADS = 8
IDX_DIM = 64
TOPK_MAX = 256
ROPE_THETA = 500000.0
H_B = 4
DK_B = 128
DV_B = 128
CONV_K = 4
CONV_DIM = 2 * H_B * DK_B + H_B * DV_B
BRANCH_WIDTH = N_HEADS_A * HEAD_DIM_A
D_FF = -(-8 * D_MODEL // (3 * 256)) * 256
NORM_EPS = 1e-6
NEG_INF = -1e30
IN_SIZES = (N_HEADS_A * HEAD_DIM_A, N_KV_A * HEAD_DIM_A, N_KV_A * HEAD_DIM_A,
            IDX_HEADS * IDX_DIM, IDX_DIM, IDX_HEADS,
            CONV_DIM, H_B, H_B, H_B * DV_B, 2 * D_MODEL)

LANES = 128
SUBLANES = 8
VMEM_LIMIT_BYTES = 56 * 1024 * 1024

MXU_DTYPE = jnp.bfloat16

SMALL_W = LANES
SEG_A = BRANCH_WIDTH + 2 * N_KV_A * HEAD_DIM_A + IDX_HEADS * IDX_DIM + IDX_DIM
SEG_A_PAD = -(-SEG_A // LANES) * LANES
OFF_SMALL = SEG_A_PAD
OFF_U = OFF_SMALL + SMALL_W
OFF_Z = OFF_U + CONV_DIM
OFF_GL = OFF_Z + H_B * DV_B
D_IN_PACKED = OFF_GL + 2 * D_MODEL


def _mm(a, b):
    return jnp.dot(a.astype(MXU_DTYPE), b.astype(MXU_DTYPE), preferred_element_type=jnp.float32)


def _mm_nt(a, b):
    return lax.dot_general(a.astype(MXU_DTYPE), b.astype(MXU_DTYPE), (((1,), (1,)), ((), ())),
                           preferred_element_type=jnp.float32)


def _mm_tn(a, b):
    return lax.dot_general(a.astype(MXU_DTYPE), b.astype(MXU_DTYPE), (((0,), (0,)), ((), ())),
                           preferred_element_type=jnp.float32)


def _split3(x):
    x = x.astype(jnp.float32)
    h = x.astype(MXU_DTYPE)
    r = x - h.astype(jnp.float32)
    m = r.astype(MXU_DTYPE)
    l = (r - m.astype(jnp.float32)).astype(MXU_DTYPE)
    return h, m, l


def _sigmoid(x):
    return 1.0 / (1.0 + jnp.exp(-x))


def _silu(x):
    return x * _sigmoid(x)


def _softplus(x):
    return jnp.maximum(x, 0.0) + jnp.log(1.0 + jnp.exp(-jnp.abs(x)))


def _rope_tile(x, cos_t, sin_lo, sin_hi):
    half = HEAD_DIM_A // 8
    up = pltpu.roll(x, LANES - half, 1)
    dn = pltpu.roll(x, half, 1)
    return x * cos_t + up * sin_lo + dn * sin_hi


def _in_proj_kernel(key_major, x_ref, gain_ref, w_ref, bd_ref, qg_ref, kg_ref, cos_ref, slo_ref, shi_ref,
                    kv_ref, ik_ref, small_ref, u_ref, z_ref, gl_ref, *attn_refs):
    x = x_ref[...]
    ms = jnp.mean(x * x, axis=-1, keepdims=True)
    xn = (x * lax.rsqrt(ms + NORM_EPS) * gain_ref[...]).astype(MXU_DTYPE)

    cos_t, sin_lo, sin_hi = cos_ref[...], slo_ref[...], shi_ref[...]
    lane = lax.broadcasted_iota(jnp.int32, (x.shape[0], LANES), 1)
    lo_half = lane < HEAD_DIM_A

    def head_rms(t, gain):
        tt = t * t
        hi = tt.astype(MXU_DTYPE)
        lo = (tt - hi.astype(jnp.float32)).astype(MXU_DTYPE)
        bd = bd_ref[0:t.shape[1], 0:t.shape[1]]
        msq = (jnp.dot(hi, bd, preferred_element_type=jnp.float32)
               + jnp.dot(lo, bd, preferred_element_type=jnp.float32))
        return t * lax.rsqrt(msq + NORM_EPS) * gain

    q = jnp.dot(xn, w_ref[:, 0:BRANCH_WIDTH], preferred_element_type=jnp.float32)
    q = head_rms(q, qg_ref[...])
    if key_major:
        qt_ref, iqt_ref, wt_ref, kb_ref, ikb_ref, vt_ref = attn_refs
        n_qb = x.shape[0] // Q_TILE
        q_scale = HEAD_DIM_A ** -0.5 * math.log2(math.e)
    else:
        qexp_ref, iqhm_ref = attn_refs
        q_scale = HEAD_DIM_A ** -0.5
    for p in range(BRANCH_WIDTH // LANES):
        t = _rope_tile(q[:, p * LANES:(p + 1) * LANES], cos_t, sin_lo, sin_hi) * q_scale
        t_sw = pltpu.roll(t, HEAD_DIM_A, 1)
        for e in range(2):
            h = 2 * p + e
            n = h // GROUP_A
            src = t if e == n else t_sw
            keep = lo_half if n == 0 else jnp.logical_not(lo_half)
            qe = jnp.where(keep, src, 0.0)
            if key_major:
                qe_t = qe.T
                for j in range(n_qb):
                    qt_ref[j, :, h * Q_TILE:(h + 1) * Q_TILE] = qe_t[:, j * Q_TILE:(j + 1) * Q_TILE].astype(qt_ref.dtype)
            else:
                qexp_ref[h] = qe.astype(qexp_ref.dtype)

    c0 = BRANCH_WIDTH
    kw = N_KV_A * HEAD_DIM_A
    k = jnp.dot(xn, w_ref[:, c0:c0 + kw], preferred_element_type=jnp.float32)
    k = _rope_tile(head_rms(k, kg_ref[...]), cos_t, sin_lo, sin_hi)
    v = jnp.dot(xn, w_ref[:, c0 + kw:c0 + 2 * kw], preferred_element_type=jnp.float32)
    if key_major:
        v_t = v.T
        kv_ref[0, 0] = k.T
        kv_ref[0, 1] = v_t
        kb_ref[...] = k.astype(kb_ref.dtype)
        vt_ref[0, 0:kw, :] = v_t.astype(vt_ref.dtype)
        vt_ref[0, kw:kw + ONES_ROWS, :] = jnp.ones((ONES_ROWS, x.shape[0]), vt_ref.dtype)
    else:
        kv_ref[:, 0:kw] = k
        kv_ref[:, kw:2 * kw] = v

    c1 = c0 + 2 * kw
    iqw = IDX_HEADS * IDX_DIM
    iq = jnp.dot(xn, w_ref[:, c1:c1 + iqw], preferred_element_type=jnp.float32)
    for p in range(iqw // LANES):
        t = _rope_tile(iq[:, p * LANES:(p + 1) * LANES], cos_t, sin_lo, sin_hi)
        if key_major:
            t_t = t.T
            for e in range(2):
                h = 2 * p + e
                for j in range(n_qb):
                    iqt_ref[j, :, h * Q_TILE:(h + 1) * Q_TILE] = (
                        t_t[e * IDX_DIM:(e + 1) * IDX_DIM, j * Q_TILE:(j + 1) * Q_TILE].astype(iqt_ref.dtype))
        else:
            t = t.astype(iqhm_ref.dtype)
            iqhm_ref[2 * p] = t[:, 0:IDX_DIM]
            iqhm_ref[2 * p + 1] = t[:, IDX_DIM:2 * IDX_DIM]

    c2 = c1 + iqw
    ik_sm = jnp.dot(xn, w_ref[:, c2:c2 + 2 * LANES], preferred_element_type=jnp.float32)
    ik_tile = _rope_tile(ik_sm[:, 0:LANES], cos_t, sin_lo, sin_hi)
    ik = ik_tile[:, 0:IDX_DIM]
    small = ik_sm[:, LANES:2 * LANES]
    small_ref[...] = small
    if not key_major:
        ik_ref[...] = ik
    else:
        ik_ref[0] = ik_tile.T[0:IDX_DIM]
        ikb_ref[...] = ik.astype(ikb_ref.dtype)
        small_t = small.T
        for j in range(n_qb):
            wt_ref[j] = small_t[0:IDX_HEADS, j * Q_TILE:(j + 1) * Q_TILE]

    u_ref[...] = jnp.dot(xn, w_ref[:, OFF_U:OFF_U + CONV_DIM], preferred_element_type=jnp.float32)
    z_ref[...] = jnp.dot(xn, w_ref[:, OFF_Z:OFF_Z + H_B * DV_B], preferred_element_type=jnp.float32)
    gl_ref[...] = jnp.dot(xn, w_ref[:, OFF_GL:OFF_GL + 2 * D_MODEL], preferred_element_type=jnp.float32)


def _pack_w_in(w_in):
    pts = np.cumsum(IN_SIZES)[:-1].tolist()
    q, k, v, iq, ik, iw, u, a, b, z, gl = jnp.split(w_in, pts, axis=-1)
    d = w_in.shape[0]
    seg_a = jnp.concatenate([q, k, v, iq, ik, jnp.zeros((d, SEG_A_PAD - SEG_A), w_in.dtype)], axis=1)
    small = jnp.concatenate([iw, a, b, jnp.zeros((d, SMALL_W - IDX_HEADS - 2 * H_B), w_in.dtype)], axis=1)
    return jnp.concatenate([seg_a, small, u, z, gl], axis=1).astype(MXU_DTYPE)


def _rope_tables(pos):
    rot = HEAD_DIM_A // 4
    half = rot // 2
    inv_freq = ROPE_THETA ** (-jnp.arange(half, dtype=jnp.float32) / half)
    ang = pos.astype(jnp.float32)[:, None] * inv_freq[None, :]
    cos, sin = jnp.cos(ang), jnp.sin(ang)
    rows = pos.shape[0]
    one = jnp.ones((rows, HEAD_DIM_A - rot), jnp.float32)
    zero = jnp.zeros((rows, HEAD_DIM_A - rot), jnp.float32)
    zh = jnp.zeros((rows, half), jnp.float32)
    cos_h = jnp.concatenate([cos, cos, one], axis=1)
    slo_h = jnp.concatenate([-sin, zh, zero], axis=1)
    shi_h = jnp.concatenate([zh, sin, zero], axis=1)
    rep = LANES // HEAD_DIM_A
    return jnp.tile(cos_h, (1, rep)), jnp.tile(slo_h, (1, rep)), jnp.tile(shi_h, (1, rep))


def _in_proj(x2d, pos_tables, n_table_blocks, tm, norm_mix, w_packed, q_norm, k_norm, key_major):
    n = x2d.shape[0]
    assert n % tm == 0 and (not key_major or tm == K_TILE)
    cos_t, sin_lo, sin_hi = pos_tables
    bd = jnp.kron(jnp.eye(BRANCH_WIDTH // HEAD_DIM_A, dtype=jnp.float32),
                  jnp.full((HEAD_DIM_A, HEAD_DIM_A), 1.0 / HEAD_DIM_A, jnp.float32)).astype(MXU_DTYPE)
    qg = jnp.tile(q_norm.astype(jnp.float32), BRANCH_WIDTH // HEAD_DIM_A)[None, :]
    kg = jnp.tile(k_norm.astype(jnp.float32), N_KV_A)[None, :]
    kw = N_KV_A * HEAD_DIM_A
    row = lambda w: pl.BlockSpec((tm, w), lambda i: (i, 0))
    full = lambda a: pl.BlockSpec(a.shape, lambda i: (0,) * a.ndim)
    tab = pl.BlockSpec((tm, LANES), lambda i: (i % n_table_blocks, 0))
    if key_major:
        seq = n_table_blocks * tm
        nt = n_table_blocks
        kv_shape, ik_shape = (n // seq, 2, kw, seq), (n // seq, IDX_DIM, seq)
        kv_spec = pl.BlockSpec((1, 2, kw, tm), lambda i: (i // nt, 0, 0, i % nt))
        ik_spec = pl.BlockSpec((1, IDX_DIM, tm), lambda i: (i // nt, 0, i % nt))
    else:
        kv_shape, ik_shape = (n, 2 * kw), (n, IDX_DIM)
        kv_spec, ik_spec = row(2 * kw), row(IDX_DIM)
    out_shape = [
        jax.ShapeDtypeStruct(kv_shape, jnp.float32),
        jax.ShapeDtypeStruct(ik_shape, jnp.float32),
        jax.ShapeDtypeStruct((n, SMALL_W), jnp.float32),
        jax.ShapeDtypeStruct((n, CONV_DIM), jnp.float32),
        jax.ShapeDtypeStruct((n, H_B * DV_B), jnp.float32),
        jax.ShapeDtypeStruct((n, 2 * D_MODEL), jnp.float32),
    ]
    out_specs = [kv_spec, ik_spec, row(SMALL_W), row(CONV_DIM), row(H_B * DV_B), row(2 * D_MODEL)]
    if key_major:
        n_qb = tm // Q_TILE
        blk = lambda r, w: pl.BlockSpec((n_qb, r, w), lambda i: (i, 0, 0))
        out_shape += [
            jax.ShapeDtypeStruct((n // Q_TILE, LANES, N_HEADS_A * Q_TILE), MXU_DTYPE),
            jax.ShapeDtypeStruct((n // Q_TILE, IDX_DIM, IDX_HEADS * Q_TILE), MXU_DTYPE),
            jax.ShapeDtypeStruct((n // Q_TILE, IDX_HEADS, Q_TILE), jnp.float32),
            jax.ShapeDtypeStruct((n, kw), MXU_DTYPE),
            jax.ShapeDtypeStruct((n, IDX_DIM), MXU_DTYPE),
            jax.ShapeDtypeStruct((n // K_TILE, kw + ONES_ROWS, K_TILE), MXU_DTYPE),
        ]
        out_specs += [blk(LANES, N_HEADS_A * Q_TILE), blk(IDX_DIM, IDX_HEADS * Q_TILE), blk(IDX_HEADS, Q_TILE),
                      row(kw), row(IDX_DIM), pl.BlockSpec((1, kw + ONES_ROWS, K_TILE), lambda i: (i, 0, 0))]
    else:
        out_shape += [jax.ShapeDtypeStruct((N_HEADS_A, n, LANES), MXU_DTYPE),
                      jax.ShapeDtypeStruct((IDX_HEADS, n, IDX_DIM), MXU_DTYPE)]
        out_specs += [pl.BlockSpec((N_HEADS_A, tm, LANES), lambda i: (0, i, 0)),
                      pl.BlockSpec((IDX_HEADS, tm, IDX_DIM), lambda i: (0, i, 0))]
    return pl.pallas_call(
        functools.partial(_in_proj_kernel, key_major),
        grid=(n // tm,),
        in_specs=[row(D_MODEL), full(norm_mix[None, :]), full(w_packed), full(bd), full(qg), full(kg),
                  tab, tab, tab],
        out_specs=tuple(out_specs),
        out_shape=tuple(out_shape),
        compiler_params=pltpu.CompilerParams(dimension_semantics=("arbitrary",),
                                             vmem_limit_bytes=VMEM_LIMIT_BYTES),
        name="in_proj",
    )(x2d, norm_mix[None, :].astype(jnp.float32), w_packed, bd, qg, kg, cos_t, sin_lo, sin_hi)


_INT_MAG = 0x7FFFFFFF


def _f32_key(x):
    b = lax.bitcast_convert_type(x, jnp.int32)
    return b ^ (lax.shift_right_arithmetic(b, 31) & _INT_MAG)


def _key_f32(k):
    b = k ^ (lax.shift_right_arithmetic(k, 31) & _INT_MAG)
    return lax.bitcast_convert_type(b, jnp.float32)


def _topk_threshold(count_ge, count_tie, row_min, row_max, n_adm, topk, n_keys, zero_counts=None,
                    fixed_steps=18, linear_steps=24):
    kf = jnp.float32(topk)
    need = n_adm > topk
    lo_k = _f32_key(row_min)
    hi_k = _f32_key(row_max) + 1
    thr = jnp.where(need, row_min, -jnp.inf)
    done = jnp.where(need, 0, 1).astype(jnp.int32)

    zero = jnp.zeros_like(row_min)
    if zero_counts is None:
        zero_counts = (count_ge(zero), count_tie(zero, jnp.full_like(lo_k, n_keys + 1)))
    ge0 = zero_counts[0]
    gt0 = ge0 - zero_counts[1]
    live = jnp.logical_and(need, lo_k < hi_k - 1)
    hit0 = jnp.logical_and(live, ge0 == kf)
    tie0 = jnp.logical_and(live, jnp.logical_and(gt0 < kf, ge0 > kf))
    thr = jnp.where(jnp.logical_or(hit0, tie0), zero, thr)
    done = jnp.where(jnp.logical_or(hit0, tie0), 1, done)
    zero_k = _f32_key(zero)
    lo_k = jnp.where(jnp.logical_and(live, ge0 > kf), jnp.maximum(lo_k, zero_k), lo_k)
    hi_k = jnp.where(jnp.logical_and(live, ge0 < kf), jnp.minimum(hi_k, zero_k), hi_k)
    state = (lo_k, hi_k, thr, jnp.where(tie0, gt0, zero), done, jnp.where(tie0, 1, 0).astype(jnp.int32))

    def step(linear, st):
        lo_k, hi_k, thr, cnt_hi, done, tie = st
        adjacent = hi_k == lo_k + 1
        lo_f, hi_f = _key_f32(lo_k), _key_f32(hi_k)
        mid_lin = _f32_key(lo_f + 0.5 * (hi_f - lo_f))
        mid_lin = jnp.minimum(jnp.maximum(mid_lin, lo_k + 1), hi_k - 1)
        mid_int = (lo_k & hi_k) + lax.shift_right_arithmetic(lo_k ^ hi_k, 1)
        mid = mid_lin if linear is True else jnp.where(linear, mid_lin, mid_int)
        mid_f = _key_f32(mid)
        cnt = count_ge(mid_f)
        live = jnp.logical_and(done == 0, jnp.logical_not(adjacent))
        hit = jnp.logical_and(live, cnt == kf)
        up = jnp.logical_and(live, cnt > kf)
        dn = jnp.logical_and(live, cnt < kf)
        new_tie = jnp.logical_and(done == 0, adjacent)
        thr = jnp.where(hit, mid_f, jnp.where(new_tie, lo_f, thr))
        tie = jnp.where(new_tie, 1, tie)
        done = jnp.where(jnp.logical_or(hit, new_tie), 1, done)
        lo_k = jnp.where(up, mid, lo_k)
        hi_k = jnp.where(dn, mid, hi_k)
        cnt_hi = jnp.where(dn, cnt, cnt_hi)
        return (lo_k, hi_k, thr, cnt_hi, done, tie)

    state = lax.fori_loop(0, fixed_steps, lambda _, st: step(True, st), state)

    def cond(st):
        it, active = st[0], st[1]
        return jnp.logical_and(it < 80, active > 0)

    def body(st):
        it = st[0]
        new = step(it < linear_steps, st[2:])
        return (it + 1, jnp.max(1 - new[4])) + new

    st = lax.while_loop(cond, body, (jnp.int32(fixed_steps), jnp.max(1 - state[4])) + state)
    thr, cnt_hi, tie = st[4], st[5], st[7]

    need_ties = kf - cnt_hi
    n_bits = max(1, int(math.ceil(math.log2(n_keys + 1))))
    any_tie = jnp.max(tie)

    def tie_body(_, lm):
        lo_m, hi_m = lm
        mid = lax.shift_right_arithmetic(lo_m + hi_m, 1)
        ge = count_tie(thr, mid) >= need_ties
        return jnp.where(ge, lo_m, mid), jnp.where(ge, mid, hi_m)

    lo_m0 = jnp.zeros_like(lo_k)
    hi_m0 = jnp.full_like(lo_k, n_keys)
    _, hi_m = lax.fori_loop(0, jnp.where(any_tie > 0, n_bits + 1, 0), tie_body, (lo_m0, hi_m0))
    cut = jnp.where(tie > 0, hi_m, n_keys + 1)
    return thr, cut


Q_TILE = 128
K_TILE = 256
K_UNROLL = 4
SCAN_UNROLL = 4
ONES_ROWS = 16


def _dsa_prompt_kernel(topk, qt_ref, iqt_ref, wt_ref, kb_ref, ikb_ref, vt_ref, o_ref, sc_ref, acc_ref, m_ref):
    i = pl.program_id(1)
    tq, kc = Q_TILE, K_TILE
    n_keys = sc_ref.shape[0] * kc
    nchunk = (i + 2) // 2
    qpos = i * tq + lax.broadcasted_iota(jnp.int32, (kc, tq), 1)
    krow = lax.broadcasted_iota(jnp.int32, (kc, tq), 0)
    qpos8 = qpos[0:SUBLANES]

    def col_reduce(x, op):
        return op(x.reshape(kc // SUBLANES, SUBLANES, tq), axis=0)

    def all_rows(x, op2):
        for shift in (4, 2, 1):
            x = op2(x, pltpu.roll(x, shift, 0))
        return x

    def tile_loop(first, rest, init, unroll):
        def trip(t, carry):
            heads = [first(t * unroll + sub) for sub in range(unroll)]
            for sub in range(unroll):
                carry = rest(t * unroll + sub, heads[sub], carry)
            return carry
        full = nchunk // unroll
        carry = lax.fori_loop(0, full, trip, init)
        return lax.fori_loop(full * unroll, nchunk, lambda c, carry: rest(c, first(c), carry), carry)

    def key_rows(c):
        return pl.ds(pl.multiple_of(c * kc, kc), kc)

    w = wt_ref[0]
    iqt = iqt_ref[0]
    s_scale = IDX_DIM ** -0.5 * IDX_HEADS ** -0.5

    def score_dots(c):
        return jnp.dot(ikb_ref[key_rows(c), :], iqt, preferred_element_type=jnp.float32)

    def score_tile(c, d, carry):
        mn, mx, ge0, eq0 = carry
        s = w[0:1] * jnp.maximum(d[:, 0:tq], 0.0)
        for h in range(1, IDX_HEADS):
            s = s + w[h:h + 1] * jnp.maximum(d[:, h * tq:(h + 1) * tq], 0.0)
        s = s * s_scale
        adm = c * kc + krow <= qpos
        sc_ref[c] = jnp.where(adm, s, NEG_INF)
        return (jnp.minimum(mn, col_reduce(jnp.where(adm, s, jnp.inf), jnp.min)),
                jnp.maximum(mx, col_reduce(jnp.where(adm, s, -jnp.inf), jnp.max)),
                ge0 + col_reduce(jnp.where(jnp.logical_and(adm, s >= 0.0), 1.0, 0.0), jnp.sum),
                eq0 + col_reduce(jnp.where(jnp.logical_and(adm, s == 0.0), 1.0, 0.0), jnp.sum))

    stat = lambda v: jnp.full((SUBLANES, tq), v, jnp.float32)
    mn, mx, ge0, eq0 = tile_loop(score_dots, score_tile, (stat(jnp.inf), stat(-jnp.inf), stat(0.0), stat(0.0)),
                                 K_UNROLL)

    n_scan = (nchunk + SCAN_UNROLL - 1) // SCAN_UNROLL

    def pad_tile(c, carry):
        sc_ref[c] = jnp.full((kc, tq), NEG_INF, jnp.float32)
        return carry

    lax.fori_loop(nchunk, n_scan * SCAN_UNROLL, pad_tile, 0)

    def scan(body, init):
        def trip(t, carry):
            for sub in range(SCAN_UNROLL):
                c = t * SCAN_UNROLL + sub
                carry = body(c, sc_ref[c], carry)
            return carry
        return lax.fori_loop(0, n_scan, trip, init)

    def count(pred):
        def body(c, s, acc):
            return acc + col_reduce(jnp.where(pred(s, c * kc + krow), 1.0, 0.0), jnp.sum)
        return all_rows(scan(body, jnp.zeros((SUBLANES, tq), jnp.float32)), jnp.add)

    def count_ge(c):
        return count(lambda s, kpos: s >= c[0:1])

    def count_tie(v, m):
        return count(lambda s, kpos: jnp.logical_and(s == v[0:1], kpos < m[0:1]))

    thr, cut = _topk_threshold(count_ge, count_tie, all_rows(mn, jnp.minimum), all_rows(mx, jnp.maximum),
                               qpos8 + 1, topk, n_keys,
                               zero_counts=(all_rows(ge0, jnp.add), all_rows(eq0, jnp.add)))
    thr_row, cut_row = thr[0:1], cut[0:1]

    m_ref[...] = jnp.full(m_ref.shape, 0.5 * NEG_INF, jnp.float32)
    acc_ref[...] = jnp.zeros(acc_ref.shape, jnp.float32)
    qt = qt_ref[0]
    kw = N_KV_A * HEAD_DIM_A

    def logits(c):
        return jnp.dot(kb_ref[key_rows(c), :], qt, preferred_element_type=jnp.float32)

    def attend_group(cs, lgs):
        biases = []
        for c in cs:
            s = sc_ref[c]
            kpos = c * kc + krow
            sel = jnp.logical_or(s > thr_row, jnp.logical_and(s == thr_row, kpos < cut_row))
            biases.append(jnp.where(jnp.logical_and(sel, kpos <= qpos), 0.0, NEG_INF))
        ps, alphas = [], []
        for h in range(N_HEADS_A):
            cols = slice(h * tq, (h + 1) * tq)
            lghs = [lg[:, cols] + bias for lg, bias in zip(lgs, biases)]
            tile_max = functools.reduce(jnp.maximum, [col_reduce(lgh, jnp.max) for lgh in lghs])
            m_old = m_ref[:, cols]
            m_new = jnp.maximum(m_old, all_rows(tile_max, jnp.maximum))
            alphas.append(jnp.exp2(m_old - m_new)[0:1])
            ps.append(jnp.concatenate([jnp.exp2(lgh - m_new[0:1]).astype(MXU_DTYPE) for lgh in lghs], axis=0))
            m_ref[:, cols] = m_new
        vt = jnp.concatenate([vt_ref[c] for c in cs], axis=1)
        pv = jnp.dot(vt, jnp.concatenate(ps, axis=1), preferred_element_type=jnp.float32)
        acc_ref[...] = acc_ref[...] * jnp.concatenate(alphas, axis=1) + pv

    def attend_trip(t, carry):
        cs = [t * K_UNROLL + sub for sub in range(K_UNROLL)]
        lgs = [logits(c) for c in cs]
        for j in range(0, K_UNROLL, 2):
            attend_group(cs[j:j + 2], lgs[j:j + 2])
        return carry

    def attend_single(c, carry):
        attend_group([c], [logits(c)])
        return carry

    full_trips = nchunk // K_UNROLL
    lax.fori_loop(0, full_trips, attend_trip, 0)
    lax.fori_loop(full_trips * K_UNROLL, nchunk, attend_single, 0)

    acc = acc_ref[...]
    o_t = acc[0:kw] / acc[kw:kw + 1]
    for p in range(N_HEADS_A // 2):
        n = (2 * p) // GROUP_A
        pair = jnp.concatenate([o_t[n * HEAD_DIM_A:(n + 1) * HEAD_DIM_A, (2 * p + e) * tq:(2 * p + e + 1) * tq]
                                for e in range(2)], axis=0)
        o_ref[:, p * LANES:(p + 1) * LANES] = pair.T.astype(o_ref.dtype)


def _dsa_prompt(qt, iqt, wt, kb, ikb, vt, batch, seq):
    tq, kc = Q_TILE, K_TILE
    assert seq % kc == 0
    nq = seq // tq
    nk = seq // kc
    n = batch * seq
    kw = N_KV_A * HEAD_DIM_A
    topk = min(TOPK_MAX, seq // 4)
    return pl.pallas_call(
        functools.partial(_dsa_prompt_kernel, topk),
        grid=(batch, nq),
        in_specs=[
            pl.BlockSpec((1, LANES, N_HEADS_A * tq), lambda b, i: (b * nq + i, 0, 0)),
            pl.BlockSpec((1, IDX_DIM, IDX_HEADS * tq), lambda b, i: (b * nq + i, 0, 0)),
            pl.BlockSpec((1, IDX_HEADS, tq), lambda b, i: (b * nq + i, 0, 0)),
            pl.BlockSpec((seq, kw), lambda b, i: (b, 0)),
            pl.BlockSpec((seq, IDX_DIM), lambda b, i: (b, 0)),
            pl.BlockSpec((nk, kw + ONES_ROWS, kc), lambda b, i: (b, 0, 0)),
        ],
        out_specs=pl.BlockSpec((tq, BRANCH_WIDTH), lambda b, i: (b * nq + i, 0)),
        out_shape=jax.ShapeDtypeStruct((n, BRANCH_WIDTH), MXU_DTYPE),
        scratch_shapes=[
            pltpu.VMEM((-(-nk // SCAN_UNROLL) * SCAN_UNROLL, kc, tq), jnp.float32),
            pltpu.VMEM((kw + ONES_ROWS, N_HEADS_A * tq), jnp.float32),
            pltpu.VMEM((SUBLANES, N_HEADS_A * tq), jnp.float32),
        ],
        compiler_params=pltpu.CompilerParams(dimension_semantics=("arbitrary", "arbitrary"),
                                             vmem_limit_bytes=VMEM_LIMIT_BYTES),
        name="dsa_prompt",
    )(qt, iqt, wt, kb, ikb, vt)


def _page_copy(pt_ref, cache_ref, buf_ref, sem_ref, seq, page, slot):
    lanes = pl.ds(page * PAGE_SIZE, PAGE_SIZE)
    return pltpu.make_async_copy(cache_ref.at[pt_ref[seq, page]], buf_ref.at[slot, :, lanes], sem_ref.at[slot])


def _pages_start(pt_ref, cache_ref, buf_ref, sem_ref, seq, slot, n_pages):
    for p in range(n_pages):
        _page_copy(pt_ref, cache_ref, buf_ref, sem_ref, seq, p, slot).start(priority=p % 2)


def _pages_wait(pt_ref, cache_ref, buf_ref, sem_ref, seq, slot, n_pages):
    for p in range(n_pages):
        _page_copy(pt_ref, cache_ref, buf_ref, sem_ref, seq, p, slot).wait()


def _dsa_sample_score_kernel(topk, n_pages, pt_ref, iq_ref, iw_ref, iknew_ref, cache_ik_ref,
                             sc_ref, thr_ref, cut_ref, ikbuf_ref, sem_ref):
    s = pl.program_id(0)
    n_seq = pl.num_programs(0)
    past = n_pages * PAGE_SIZE
    n_tiles = sc_ref.shape[0]
    slot = s % 2

    @pl.when(s == 0)
    def _():
        _pages_start(pt_ref, cache_ik_ref, ikbuf_ref, sem_ref, 0, 0, n_pages)

    @pl.when(s + 1 < n_seq)
    def _():
        _pages_start(pt_ref, cache_ik_ref, ikbuf_ref, sem_ref, s + 1, 1 - slot, n_pages)

    _pages_wait(pt_ref, cache_ik_ref, ikbuf_ref, sem_ref, s, slot, n_pages)

    iq = iq_ref[0]
    w = iw_ref[0]
    s_scale = IDX_DIM ** -0.5 * IDX_HEADS ** -0.5
    d = _mm(iq, ikbuf_ref[slot])
    srow = jnp.sum(w * jnp.maximum(d, 0.0), axis=0, keepdims=True) * s_scale
    for j in range(n_pages):
        sc_ref[j, pl.ds(s, 1), :] = srow[:, j * LANES:(j + 1) * LANES]
    ik_new = iknew_ref[0].astype(MXU_DTYPE).astype(jnp.float32)
    d_self = jnp.sum(iq.astype(jnp.float32) * ik_new, axis=1, keepdims=True)
    s_self = jnp.sum(w * jnp.maximum(d_self, 0.0), axis=0, keepdims=True) * s_scale
    lane1 = lax.broadcasted_iota(jnp.int32, (1, LANES), 1)
    sc_ref[n_tiles - 1, pl.ds(s, 1), :] = jnp.where(lane1 == 0, s_self, NEG_INF)

    @pl.when(s == n_seq - 1)
    def _():
        rows = sc_ref.shape[1]
        lane = lax.broadcasted_iota(jnp.int32, (rows, LANES), 1)

        def count(pred):
            def body(j, acc):
                return acc + jnp.where(pred(sc_ref[j], j * LANES + lane), 1.0, 0.0)
            acc = lax.fori_loop(0, n_tiles, body, jnp.zeros((rows, LANES), jnp.float32))
            return jnp.broadcast_to(jnp.sum(acc, axis=1, keepdims=True), (rows, LANES))

        def count_ge(c):
            return count(lambda t, kpos: t >= c)

        def count_tie(v, m):
            return count(lambda t, kpos: jnp.logical_and(t == v, kpos < m))

        def minmax(j, mm):
            t = sc_ref[j]
            adm = j * LANES + lane <= past
            return (jnp.minimum(mm[0], jnp.where(adm, t, jnp.inf)),
                    jnp.maximum(mm[1], jnp.where(adm, t, -jnp.inf)))

        mn, mx = lax.fori_loop(0, n_tiles, minmax, (jnp.full((rows, LANES), jnp.inf, jnp.float32),
                                                    jnp.full((rows, LANES), -jnp.inf, jnp.float32)))
        row_min = jnp.broadcast_to(jnp.min(mn, axis=1, keepdims=True), (rows, LANES))
        row_max = jnp.broadcast_to(jnp.max(mx, axis=1, keepdims=True), (rows, LANES))
        n_adm = jnp.full((rows, LANES), past + 1, jnp.int32)
        thr, cut = _topk_threshold(count_ge, count_tie, row_min, row_max, n_adm, topk, n_tiles * LANES)
        thr_ref[...] = thr
        cut_ref[...] = cut


def _dsa_sample_attend_kernel(n_pages, pt_ref, q_ref, sc_ref, thr_ref, cut_ref, kvnew_ref,
                              cache_k_ref, cache_v_ref, o_ref, kbuf_ref, vbuf_ref, ksem_ref, vsem_ref):
    s = pl.program_id(0)
    n_seq = pl.num_programs(0)
    past = n_pages * PAGE_SIZE
    slot = s % 2

    def start(seq, sl):
        _pages_start(pt_ref, cache_k_ref, kbuf_ref, ksem_ref, seq, sl, n_pages)
        _pages_start(pt_ref, cache_v_ref, vbuf_ref, vsem_ref, seq, sl, n_pages)

    @pl.when(s == 0)
    def _():
        start(0, 0)

    @pl.when(s + 1 < n_seq)
    def _():
        start(s + 1, 1 - slot)

    _pages_wait(pt_ref, cache_k_ref, kbuf_ref, ksem_ref, s, slot, n_pages)
    _pages_wait(pt_ref, cache_v_ref, vbuf_ref, vsem_ref, s, slot, n_pages)

    q = q_ref[0]
    thr = thr_ref[0][:, 0:1]
    cut = cut_ref[0][:, 0:1]
    kw = N_KV_A * HEAD_DIM_A
    k_new = kvnew_ref[0][:, 0:kw].astype(MXU_DTYPE).astype(jnp.float32)
    v_new = kvnew_ref[0][:, kw:2 * kw].astype(MXU_DTYPE).astype(jnp.float32)
    sc = sc_ref[0]

    def selected(srow, kpos):
        return jnp.logical_or(srow > thr, jnp.logical_and(srow == thr, kpos < cut))

    kpos = lax.broadcasted_iota(jnp.int32, (1, past), 1)
    bias = jnp.where(selected(sc[:, 0:past], kpos), 0.0, NEG_INF)
    lg = _mm(q, kbuf_ref[slot]) + bias
    lg_self = jnp.sum(q.astype(jnp.float32) * k_new, axis=1, keepdims=True)
    lg_self = jnp.where(selected(sc[:, past:past + 1], past), lg_self, NEG_INF)
    m = jnp.maximum(jnp.max(lg, axis=1, keepdims=True), lg_self)
    p = jnp.exp(lg - m)
    p_self = jnp.exp(lg_self - m)
    denom = jnp.sum(p, axis=1, keepdims=True) + p_self
    o = (_mm_nt(p, vbuf_ref[slot]) + p_self * v_new) / denom
    parts = []
    for h in range(N_HEADS_A):
        n = h // GROUP_A
        parts.append(o[h:h + 1, n * HEAD_DIM_A:(n + 1) * HEAD_DIM_A])
    o_ref[0] = jnp.concatenate(parts, axis=1)


def _dsa_sample(qexp, iqhm, small, ik_new, kv_new, cache_k, cache_v, cache_ik, page_table):
    db, n_pages = page_table.shape
    past = n_pages * PAGE_SIZE
    n_pool = cache_ik.shape[0]
    topk = min(TOPK_MAX, (past + 1) // 4)
    n_tiles = n_pages + 1
    kw = N_KV_A * HEAD_DIM_A
    q_s = jnp.swapaxes(qexp, 0, 1)
    iq_s = jnp.swapaxes(iqhm, 0, 1)
    iw_s = small[:, 0:IDX_HEADS].reshape(db, IDX_HEADS, 1)
    ck_t = jnp.transpose(cache_k, (0, 2, 3, 1)).reshape(n_pool, kw, PAGE_SIZE)
    cv_t = jnp.transpose(cache_v, (0, 2, 3, 1)).reshape(n_pool, kw, PAGE_SIZE)
    cik_t = jnp.swapaxes(cache_ik, 1, 2)
    cparams = pltpu.CompilerParams(dimension_semantics=("arbitrary",), vmem_limit_bytes=VMEM_LIMIT_BYTES)
    per_seq = lambda *shape: pl.BlockSpec((1,) + shape, lambda s, pt: (s,) + (0,) * len(shape))
    whole = lambda *shape: pl.BlockSpec(shape, lambda s, pt: (0,) * len(shape))
    any_spec = pl.BlockSpec(memory_space=pl.ANY)

    sc, thr, cut = pl.pallas_call(
        functools.partial(_dsa_sample_score_kernel, topk, n_pages),
        grid_spec=pltpu.PrefetchScalarGridSpec(
            num_scalar_prefetch=1,
            grid=(db,),
            in_specs=[per_seq(IDX_HEADS, IDX_DIM), per_seq(IDX_HEADS, 1), per_seq(1, IDX_DIM), any_spec],
            out_specs=(whole(n_tiles, db, LANES), whole(db, LANES), whole(db, LANES)),
            scratch_shapes=[pltpu.VMEM((2, IDX_DIM, past), jnp.float32), pltpu.SemaphoreType.DMA((2,))],
        ),
        out_shape=(jax.ShapeDtypeStruct((n_tiles, db, LANES), jnp.float32),
                   jax.ShapeDtypeStruct((db, LANES), jnp.float32),
                   jax.ShapeDtypeStruct((db, LANES), jnp.int32)),
        compiler_params=cparams,
        name="dsa_sample_score",
    )(page_table, iq_s, iw_s, ik_new.reshape(db, 1, IDX_DIM), cik_t)

    o = pl.pallas_call(
        functools.partial(_dsa_sample_attend_kernel, n_pages),
        grid_spec=pltpu.PrefetchScalarGridSpec(
            num_scalar_prefetch=1,
            grid=(db,),
            in_specs=[per_seq(N_HEADS_A, LANES), per_seq(1, n_tiles * LANES), per_seq(1, LANES), per_seq(1, LANES),
                      per_seq(1, 2 * kw), any_spec, any_spec],
            out_specs=per_seq(1, BRANCH_WIDTH),
            scratch_shapes=[pltpu.VMEM((2, kw, past), jnp.float32), pltpu.VMEM((2, kw, past), jnp.float32),
                            pltpu.SemaphoreType.DMA((2,)), pltpu.SemaphoreType.DMA((2,))],
        ),
        out_shape=jax.ShapeDtypeStruct((db, 1, BRANCH_WIDTH), jnp.float32),
        compiler_params=cparams,
        name="dsa_sample_attend",
    )(page_table, q_s, jnp.swapaxes(sc, 0, 1).reshape(db, 1, n_tiles * LANES),
      thr.reshape(db, 1, LANES), cut.reshape(db, 1, LANES),
      kv_new.reshape(db, 1, 2 * kw), ck_t, cv_t)
    return o.reshape(db, BRANCH_WIDTH)


GDN_CHUNK = 128
A_LANE = IDX_HEADS
B_LANE = IDX_HEADS + H_B


def _split2(x):
    h = x.astype(MXU_DTYPE)
    return h, (x - h.astype(jnp.float32)).astype(MXU_DTYPE)


def _mm2(a, b):
    a1, a2 = _split2(a)
    b1, b2 = _split2(b)
    d = functools.partial(jnp.dot, preferred_element_type=jnp.float32)
    return d(a1, b1) + (d(a1, b2) + d(a2, b1))


def _unit_lower_inverses(mats):
    n = mats[0].shape[0]
    eye = (lax.broadcasted_iota(jnp.int32, (n, n), 0) == lax.broadcasted_iota(jnp.int32, (n, n), 1))
    ss = [jnp.where(eye, 1.0, 0.0) - a for a in mats]
    ps = [_mm2(a, a) for a in mats]
    k = 2
    while k < n:
        ss = [s + _mm2(s, p) for s, p in zip(ss, ps)]
        k *= 2
        if k < n:
            ps = [_mm2(p, p) for p in ps]
    return ss


def _l2norm(x):
    return x * lax.rsqrt(jnp.sum(x * x, axis=-1, keepdims=True) + NORM_EPS)


def _gdn_prompt_kernel(u_ref, small_ref, z_ref, conv0_ref, s0_ref, wconv_ref, alog_ref, dtb_ref, dnorm_ref,
                       o_ref, conv_out_ref, s_out_ref, ucat_ref, state_ref):
    n = pl.program_id(0)
    nb = u_ref.shape[0]
    c = GDN_CHUNK
    head = SUBLANES
    tail = CONV_K - 1

    @pl.when(n == 0)
    def _():
        ucat_ref[:, head - tail:head, :] = conv0_ref[...]
        state_ref[...] = s0_ref[...]

    row = lax.broadcasted_iota(jnp.int32, (c, c), 0)
    col = lax.broadcasted_iota(jnp.int32, (c, c), 1)
    lower = row >= col
    strict = row > col
    ltri = jnp.where(lower, 1.0, 0.0).astype(MXU_DTYPE)
    d = functools.partial(jnp.dot, preferred_element_type=jnp.float32)

    chains = [(b, h) for b in range(nb) for h in range(H_B)]
    qw = H_B * DK_B
    qs, ks, vs, betas, gcols = [], [], [], [], []
    for b in range(nb):
        ucat_ref[b, head:head + c, :] = u_ref[b]
        y = wconv_ref[tail:tail + 1, :] * ucat_ref[b, head:head + c, :]
        for j in range(tail):
            y = y + wconv_ref[j:j + 1, :] * ucat_ref[b, head - tail + j:head - tail + j + c, :]
        cv = _silu(y)
        carry_rows = ucat_ref[b, head + c - tail:head + c, :]
        ucat_ref[b, head - tail:head, :] = carry_rows
        conv_out_ref[b] = carry_rows
        sm = small_ref[b]
        g_all = -jnp.exp(alog_ref[...]) * _softplus(sm + dtb_ref[...])
        beta_all = _sigmoid(sm)
        a1, a2, a3 = _split3(g_all)
        gc_all = d(ltri, a1) + (d(ltri, a2) + d(ltri, a3))
        for h in range(H_B):
            qs.append(_l2norm(cv[:, h * DK_B:(h + 1) * DK_B]) * (DK_B ** -0.5))
            ks.append(_l2norm(cv[:, qw + h * DK_B:qw + (h + 1) * DK_B]))
            vs.append(cv[:, 2 * qw + h * DV_B:2 * qw + (h + 1) * DV_B])
            betas.append(jnp.broadcast_to(beta_all[:, B_LANE + h:B_LANE + h + 1], (c, LANES)))
            gcols.append(jnp.broadcast_to(gc_all[:, A_LANE + h:A_LANE + h + 1], (c, c)))
    decays = [jnp.where(lower, jnp.exp(jnp.where(lower, g - g.T, 0.0)), 0.0) for g in gcols]
    egs = [jnp.exp(g) for g in gcols]
    g_lasts = [g[c - 1:c, :] for g in gcols]
    kbs = [k * b for k, b in zip(ks, betas)]
    vbs = [v * b for v, b in zip(vs, betas)]
    kks = [_mm_nt(kb, k) for kb, k in zip(kbs, ks)]
    qks = [_mm_nt(q, k) for q, k in zip(qs, ks)]
    t_invs = _unit_lower_inverses([jnp.where(strict, kk * dc, 0.0) for kk, dc in zip(kks, decays)])
    sols = [_mm2(t, jnp.concatenate([vb, kb * eg], axis=1)) for t, vb, kb, eg in zip(t_invs, vbs, kbs, egs)]
    s_olds = [state_ref[b, h] for b, h in chains]
    v_news = [sol[:, 0:DV_B] - _mm(sol[:, DV_B:DV_B + DK_B], s) for sol, s in zip(sols, s_olds)]
    o_hs = [_mm(q * eg, s) + _mm(qk * dc, v_new)
            for q, eg, s, qk, dc, v_new in zip(qs, egs, s_olds, qks, decays, v_news)]
    for i, (b, h) in enumerate(chains):
        k_dec = ks[i] * jnp.exp(g_lasts[i] - gcols[i])
        state_ref[b, h] = s_olds[i] * jnp.exp(g_lasts[i]) + _mm_tn(k_dec, v_news[i])
    for i, (b, h) in enumerate(chains):
        o_h = o_hs[i]
        ms = jnp.mean(o_h * o_h, axis=-1, keepdims=True)
        o_n = o_h * lax.rsqrt(ms + NORM_EPS) * dnorm_ref[...]
        gate = _silu(z_ref[b, :, h * DV_B:(h + 1) * DV_B])
        o_ref[b, :, h * DV_B:(h + 1) * DV_B] = (o_n * gate).astype(o_ref.dtype)

    @pl.when(n == pl.num_programs(0) - 1)
    def _():
        s_out_ref[...] = state_ref[...]


GDN_SEQ_TILE = 8


def _gdn_sample_kernel(u_ref, cb_ref, small_ref, z_ref, s0_ref, wconv_ref, alog_ref, dtb_ref, dnorm_ref,
                       o_ref, conv_out_ref, s_out_ref):
    ts = GDN_SEQ_TILE
    tail = CONV_K - 1
    u_new = u_ref[...]
    y = wconv_ref[tail:tail + 1, :] * u_new
    for j in range(tail):
        y = y + wconv_ref[j:j + 1, :] * cb_ref[j]
    cv = _silu(y)
    for j in range(tail - 1):
        conv_out_ref[j] = cb_ref[j + 1]
    conv_out_ref[tail - 1] = u_new

    sm = small_ref[...]
    eg_all = jnp.exp(-jnp.exp(alog_ref[...]) * _softplus(sm + dtb_ref[...]))
    beta_all = _sigmoid(sm)
    qw = H_B * DK_B
    for h in range(H_B):
        q = _l2norm(cv[:, h * DK_B:(h + 1) * DK_B]) * (DK_B ** -0.5)
        k = _l2norm(cv[:, qw + h * DK_B:qw + (h + 1) * DK_B])
        v = cv[:, 2 * qw + h * DV_B:2 * qw + (h + 1) * DV_B]
        eg = eg_all[:, A_LANE + h:A_LANE + h + 1]
        beta = beta_all[:, B_LANE + h:B_LANE + h + 1]
        qk = jnp.sum(q * k, axis=-1, keepdims=True)
        k_t, q_t = k.T, q.T
        rows = []
        for r in range(ts):
            s_old = s0_ref[r, h]
            kc = k_t[:, r:r + 1]
            ks = jnp.sum(s_old * kc, axis=0, keepdims=True)
            qs = jnp.sum(s_old * q_t[:, r:r + 1], axis=0, keepdims=True)
            eg_r = eg[r:r + 1, :]
            v_new = beta[r:r + 1, :] * (v[r:r + 1, :] - eg_r * ks)
            rows.append(eg_r * qs + qk[r:r + 1, :] * v_new)
            s_out_ref[r, h] = s_old * eg_r + kc * v_new
        o_h = jnp.concatenate(rows, axis=0)
        ms = jnp.mean(o_h * o_h, axis=-1, keepdims=True)
        o_n = o_h * lax.rsqrt(ms + NORM_EPS) * dnorm_ref[...]
        o_ref[:, h * DV_B:(h + 1) * DV_B] = (o_n * _silu(z_ref[:, h * DV_B:(h + 1) * DV_B])).astype(o_ref.dtype)


def _gdn_sample(u, small, z, conv_buf, s0, w_conv, a_log, dt_bias, delta_norm):
    db = u.shape[0]
    ts = GDN_SEQ_TILE
    assert db % ts == 0
    tail = CONV_K - 1
    alog_row, dtb_row = _gate_rows(a_log, dt_bias)
    row = lambda w: pl.BlockSpec((ts, w), lambda i: (i, 0))
    full = lambda *shape: pl.BlockSpec(shape, lambda i: (0,) * len(shape))
    cb_spec = pl.BlockSpec((tail, ts, CONV_DIM), lambda i: (0, i, 0))
    st_spec = pl.BlockSpec((ts, H_B, DK_B, DV_B), lambda i: (i, 0, 0, 0))
    o, conv_t, s_new = pl.pallas_call(
        _gdn_sample_kernel,
        grid=(db // ts,),
        in_specs=[row(CONV_DIM), cb_spec, row(SMALL_W), row(H_B * DV_B), st_spec,
                  full(CONV_K, CONV_DIM), full(1, SMALL_W), full(1, SMALL_W), full(1, DV_B)],
        out_specs=(row(H_B * DV_B), cb_spec, st_spec),
        out_shape=(jax.ShapeDtypeStruct((db, H_B * DV_B), jnp.float32),
                   jax.ShapeDtypeStruct((tail, db, CONV_DIM), jnp.float32),
                   jax.ShapeDtypeStruct((db, H_B, DK_B, DV_B), jnp.float32)),
        compiler_params=pltpu.CompilerParams(dimension_semantics=("arbitrary",),
                                             vmem_limit_bytes=VMEM_LIMIT_BYTES),
        name="gdn_sample",
    )(u, jnp.swapaxes(conv_buf, 0, 1), small, z, s0, w_conv.astype(jnp.float32), alog_row, dtb_row,
      delta_norm.astype(jnp.float32)[None, :])
    return o, jnp.swapaxes(conv_t, 0, 1), s_new


def _gate_rows(a_log, dt_bias):
    alog_row = jnp.zeros((1, SMALL_W), jnp.float32).at[0, A_LANE:A_LANE + H_B].set(a_log.astype(jnp.float32))
    dtb_row = jnp.zeros((1, SMALL_W), jnp.float32).at[0, A_LANE:A_LANE + H_B].set(dt_bias.astype(jnp.float32))
    return alog_row, dtb_row


def _gdn_prompt(u, small, z, conv0, s0, w_conv, a_log, dt_bias, delta_norm, batch, seq):
    c = GDN_CHUNK
    assert seq % c == 0
    alog_row, dtb_row = _gate_rows(a_log, dt_bias)
    row = lambda w: pl.BlockSpec((batch, c, w), lambda i: (0, i, 0))
    full = lambda *shape: pl.BlockSpec(shape, lambda i: (0,) * len(shape))
    o, conv_new, s_new = pl.pallas_call(
        _gdn_prompt_kernel,
        grid=(seq // c,),
        in_specs=[row(CONV_DIM), row(SMALL_W), row(H_B * DV_B), full(batch, CONV_K - 1, CONV_DIM),
                  full(batch, H_B, DK_B, DV_B), full(CONV_K, CONV_DIM), full(1, SMALL_W), full(1, SMALL_W),
                  full(1, DV_B)],
        out_specs=(row(H_B * DV_B), full(batch, CONV_K - 1, CONV_DIM), full(batch, H_B, DK_B, DV_B)),
        out_shape=(jax.ShapeDtypeStruct((batch, seq, H_B * DV_B), MXU_DTYPE),
                   jax.ShapeDtypeStruct((batch, CONV_K - 1, CONV_DIM), jnp.float32),
                   jax.ShapeDtypeStruct((batch, H_B, DK_B, DV_B), jnp.float32)),
        scratch_shapes=[pltpu.VMEM((batch, SUBLANES + c, CONV_DIM), jnp.float32),
                        pltpu.VMEM((batch, H_B, DK_B, DV_B), jnp.float32)],
        compiler_params=pltpu.CompilerParams(dimension_semantics=("arbitrary",),
                                             vmem_limit_bytes=VMEM_LIMIT_BYTES),
        name="gdn_prompt",
    )(u.reshape(batch, seq, CONV_DIM), small.reshape(batch, seq, SMALL_W), z.reshape(batch, seq, H_B * DV_B),
      conv0, s0, w_conv.astype(jnp.float32), alog_row, dtb_row, delta_norm.astype(jnp.float32)[None, :])
    return o.reshape(batch * seq, H_B * DV_B), conv_new, s_new


def _merge_kernel(x_ref, oa_ref, ob_ref, gl_ref, wba_ref, wbb_ref, wout_ref, gain_ref, x1_ref, hn_ref):
    pa = _mm(oa_ref[...], wba_ref[...])
    pb = _mm(ob_ref[...], wbb_ref[...])
    mix = _sigmoid(gl_ref[:, 0:D_MODEL]) * pa + _sigmoid(gl_ref[:, D_MODEL:2 * D_MODEL]) * pb
    x1 = x_ref[...] + _mm(mix, wout_ref[...])
    x1_ref[...] = x1
    ms = jnp.mean(x1 * x1, axis=-1, keepdims=True)
    hn_ref[...] = (x1 * lax.rsqrt(ms + NORM_EPS) * gain_ref[...]).astype(hn_ref.dtype)


def _merge(x2d, o_a, o_b, gl, w_branch, w_out, norm_ffn, tm):
    n = x2d.shape[0]
    assert n % tm == 0
    row = lambda w: pl.BlockSpec((tm, w), lambda i: (i, 0))
    full = lambda *shape: pl.BlockSpec(shape, lambda i: (0,) * len(shape))
    return pl.pallas_call(
        _merge_kernel,
        grid=(n // tm,),
        in_specs=[row(D_MODEL), row(BRANCH_WIDTH), row(BRANCH_WIDTH), row(2 * D_MODEL),
                  full(BRANCH_WIDTH, D_MODEL), full(BRANCH_WIDTH, D_MODEL), full(D_MODEL, D_MODEL),
                  full(1, D_MODEL)],
        out_specs=(row(D_MODEL), row(D_MODEL)),
        out_shape=(jax.ShapeDtypeStruct((n, D_MODEL), jnp.float32),
                   jax.ShapeDtypeStruct((n, D_MODEL), MXU_DTYPE)),
        compiler_params=pltpu.CompilerParams(dimension_semantics=("arbitrary",),
                                             vmem_limit_bytes=VMEM_LIMIT_BYTES),
        name="merge",
    )(x2d, o_a, o_b, gl, w_branch[0].astype(MXU_DTYPE), w_branch[1].astype(MXU_DTYPE),
      w_out.astype(MXU_DTYPE), norm_ffn.astype(jnp.float32)[None, :])


FFN_TILE = D_FF // 2


def _ffn_kernel(hn_ref, x1_ref, wg_ref, wu_ref, wd_ref, y_ref, acc_ref):
    j = pl.program_id(1)

    @pl.when(j == 0)
    def _():
        acc_ref[...] = x1_ref[...]

    hn = hn_ref[...]
    g = jnp.dot(hn, wg_ref[...], preferred_element_type=jnp.float32)
    u = jnp.dot(hn, wu_ref[...], preferred_element_type=jnp.float32)
    acc_ref[...] += _mm(_silu(g) * u, wd_ref[...])

    @pl.when(j == pl.num_programs(1) - 1)
    def _():
        y_ref[...] = acc_ref[...]


def _ffn(hn, x1, w_gate_up, w_down, tm):
    n = hn.shape[0]
    tf = FFN_TILE
    assert n % tm == 0 and D_FF % tf == 0 and tf % LANES == 0
    nf = D_FF // tf
    wgu = w_gate_up.astype(MXU_DTYPE)
    return pl.pallas_call(
        _ffn_kernel,
        grid=(n // tm, nf),
        in_specs=[pl.BlockSpec((tm, D_MODEL), lambda i, j: (i, 0)),
                  pl.BlockSpec((tm, D_MODEL), lambda i, j: (i, 0)),
                  pl.BlockSpec((D_MODEL, tf), lambda i, j: (0, j)),
                  pl.BlockSpec((D_MODEL, tf), lambda i, j: (0, j + nf)),
                  pl.BlockSpec((tf, D_MODEL), lambda i, j: (j, 0))],
        out_specs=pl.BlockSpec((tm, D_MODEL), lambda i, j: (i, 0)),
        out_shape=jax.ShapeDtypeStruct((n, D_MODEL), jnp.float32),
        scratch_shapes=[pltpu.VMEM((tm, D_MODEL), jnp.float32)],
        compiler_params=pltpu.CompilerParams(dimension_semantics=("arbitrary", "arbitrary"),
                                             vmem_limit_bytes=VMEM_LIMIT_BYTES),
        name="ffn",
    )(hn, x1, wgu, wgu, w_down.astype(MXU_DTYPE))


IN_PROJ_TILE = K_TILE
MERGE_TILE = 512
FFN_ROW_TILE = 512


def _layer(x_p, x_s, cache_k, cache_v, cache_ik, conv_s, delta_s, page_table, norm_mix, w_in, q_norm, k_norm,
           w_conv, a_log, dt_bias, delta_norm, w_branch, w_out, norm_ffn, w_gate_up, w_down):
    b, t, d = x_p.shape
    db = x_s.shape[0]
    past = page_table.shape[1] * PAGE_SIZE
    kw = N_KV_A * HEAD_DIM_A
    w_packed = _pack_w_in(w_in)

    xp2 = x_p.reshape(b * t, d)
    tm = IN_PROJ_TILE
    assert t % tm == 0
    kv, ik, small, u, z, gl, qt, iqt, wt, kb, ikb, vt = _in_proj(
        xp2, _rope_tables(jnp.arange(t)), t // tm, tm, norm_mix, w_packed, q_norm, k_norm, key_major=True)
    o_a = _dsa_prompt(qt, iqt, wt, kb, ikb, vt, b, t)
    conv0 = jnp.zeros((b, CONV_K - 1, CONV_DIM), jnp.float32)
    delta0 = jnp.zeros((b, H_B, DK_B, DV_B), jnp.float32)
    o_b, conv_p, delta_p = _gdn_prompt(u, small, z, conv0, delta0, w_conv, a_log, dt_bias, delta_norm, b, t)
    x1, hn = _merge(xp2, o_a, o_b, gl, w_branch, w_out, norm_ffn, min(MERGE_TILE, b * t))
    y_p = _ffn(hn, x1, w_gate_up, w_down, min(FFN_ROW_TILE, b * t)).reshape(b, t, d)
    kv5 = kv.reshape(b, 2, N_KV_A, HEAD_DIM_A, t)
    st_p = (jnp.transpose(kv5[:, 0], (0, 3, 1, 2)), jnp.transpose(kv5[:, 1], (0, 3, 1, 2)),
            jnp.swapaxes(ik, 1, 2), conv_p, delta_p)

    xs2 = x_s.reshape(db, d)
    kv, ik, small, u, z, gl, qexp, iqhm = _in_proj(
        xs2, _rope_tables(jnp.full((db,), past, jnp.int32)), 1, db, norm_mix, w_packed, q_norm, k_norm,
        key_major=False)
    o_a = _dsa_sample(qexp, iqhm, small, ik, kv, cache_k, cache_v, cache_ik, page_table)
    o_b, conv_n, delta_n = _gdn_sample(u, small, z, conv_s, delta_s, w_conv, a_log, dt_bias, delta_norm)
    x1, hn = _merge(xs2, o_a, o_b, gl, w_branch, w_out, norm_ffn, db)
    y_s = _ffn(hn, x1, w_gate_up, w_down, db).reshape(db, 1, d)
    st_s = (kv[:, 0:kw].reshape(db, 1, N_KV_A, HEAD_DIM_A), kv[:, kw:2 * kw].reshape(db, 1, N_KV_A, HEAD_DIM_A),
            ik.reshape(db, 1, IDX_DIM), conv_n, delta_n)
    return y_p, y_s, st_p, st_s


def kernel(x_prompt, x_sample, cache_k, cache_v, cache_idx_k, state_conv, state_delta, page_table,
           norm_mix, w_in, q_norm, k_norm, w_conv, a_log, dt_bias, delta_norm, w_branch, w_out,
           norm_ffn, w_gate_up, w_down):
    assert x_sample.shape[1] == 1, "the sample group decodes one token per sequence"
    y_p, y_s = x_prompt, x_sample
    new_p, new_s = [], []
    for l in range(w_in.shape[0]):
        y_p, y_s, st_p, st_s = _layer(
            y_p, y_s, cache_k[l], cache_v[l], cache_idx_k[l], state_conv[l], state_delta[l], page_table,
            norm_mix[l], w_in[l], q_norm[l], k_norm[l], w_conv[l], a_log[l], dt_bias[l], delta_norm[l],
            w_branch[l], w_out[l], norm_ffn[l], w_gate_up[l], w_down[l])
        new_p.append(st_p)
        new_s.append(st_s)
    k_p, v_p, ik_p, conv_p, delta_p = [jnp.stack(a) for a in zip(*new_p)]
    k_s, v_s, ik_s, conv_s, delta_s = [jnp.stack(a) for a in zip(*new_s)]
    return (y_p, y_s, k_p, v_p, ik_p, conv_p, delta_p, k_s, v_s, ik_s, conv_s, delta_s)
```
